```python
import jax, jax.numpy as jnp
from jax import lax
import numpy as np

D_MODEL = 1024
BATCH = 32
SEQ = 256
DEPTH = 2
DEC_BATCH = 8
DEC_SEQ = 2048
PAST_LEN = 256

GRID_W = 64
EPS = 1e-6
F32 = jnp.float32
A_HEADS = 4
A_DK = 128
A_DV = 128
A_W = A_HEADS * A_DK
CHUNK = 32
LB_FLOOR = 1e-30
NEG_BIG = -1e30
B_W = 512
B_GROUPS = 4
CONV_W = 3
C_HEADS = 8
C_NOPE = 64
C_ROPE = 32
C_V = 64
C_Q_LORA = 384
C_KV_LORA = 256
ROPE_THETA = 10000.0
Q_BLOCK = 128
D_FF = 2816
N_BRANCH = 3
IN_SIZES = (A_W, A_W, A_W, A_W, A_W, B_W, B_W, B_W, C_Q_LORA, C_KV_LORA + C_ROPE, N_BRANCH * D_MODEL)
IN_COLS = 5 * A_W + 3 * B_W + C_Q_LORA + C_KV_LORA + C_ROPE + N_BRANCH * D_MODEL

kernel_name = 'hybrid_flow_backbone_step'


def rmsnorm(x, g):
    xf = x.astype(F32)
    y = xf * lax.rsqrt(jnp.mean(xf * xf, axis=-1, keepdims=True) + EPS)
    return (y * g.astype(F32)).astype(x.dtype)


def split_cols(z, sizes):
    out, o = [], 0
    for s in sizes:
        out.append(z[..., o:o + s])
        o += s
    return out


def dwconv3(u, w):
    up = jnp.pad(u, ((0, 0), (1, 1), (0, 0)))
    return up[:, :-2] * w[0] + up[:, 1:-1] * w[1] + up[:, 2:] * w[2]


def axial_rope_tables(T):
    rows = T // GRID_W
    r = jnp.repeat(jnp.arange(rows, dtype=F32), GRID_W)
    col = jnp.tile(jnp.arange(GRID_W, dtype=F32), rows)
    n_freq = C_ROPE // 4
    freq = ROPE_THETA ** (-jnp.arange(n_freq, dtype=F32) / n_freq)
    ang = jnp.stack([r[:, None] * freq, col[:, None] * freq], axis=1)
    return jnp.cos(ang), jnp.sin(ang)


def apply_axial_rope(x, cos, sin):
    n_freq = C_ROPE // 4
    xf = x.astype(F32).reshape(*x.shape[:-1], 2, 2, n_freq)
    x1, x2 = xf[..., 0, :], xf[..., 1, :]
    out = jnp.stack([x1 * cos - x2 * sin, x2 * cos + x1 * sin], axis=-2)
    return out.reshape(x.shape).astype(x.dtype)


def log_forget(z, lb):
    return jnp.logaddexp(jnp.log(jnp.maximum(lb, LB_FLOOR)),
                         jnp.log1p(-lb) + jax.nn.log_sigmoid(z.astype(F32)))


def chunk_gated_recurrence(q, k, v, logf, s0):
    B, T, H, DK = q.shape
    DV = v.shape[-1]
    n = T // CHUNK

    def to_chunks(a):
        return a.reshape(B, n, CHUNK, H, a.shape[-1]).transpose(1, 0, 3, 2, 4)

    mask = jnp.tril(jnp.ones((CHUNK, CHUNK), dtype=bool))

    def step(S, inp):
        qc, kc, vc, gc = inp
        b = jnp.cumsum(gc, axis=2)
        diff = jnp.where(mask[:, :, None], b[:, :, :, None, :] - b[:, :, None, :, :], NEG_BIG)
        scores = jnp.einsum('bhtd,bhsd,bhtsd->bhts', qc, kc, jnp.exp(diff))
        o = jnp.einsum('bhts,bhsv->bhtv', scores, vc) + jnp.einsum('bhtd,bhdv->bhtv', qc * jnp.exp(b), S)
        b_last = b[:, :, -1, :]
        S_new = jnp.exp(b_last)[..., None] * S + jnp.einsum('bhsd,bhsv->bhdv', kc * jnp.exp(b_last[:, :, None, :] - b), vc)
        return S_new, o

    S_fin, o = lax.scan(step, s0, (to_chunks(q), to_chunks(k), to_chunks(v), to_chunks(logf)))
    return o.transpose(1, 0, 3, 2, 4).reshape(B, T, H, DV), S_fin


def hgrn2_mixer(q_raw, f_raw_fwd, f_raw_bwd, i_raw, g_raw, lb, gnorm, s_init):
    B, T, _ = q_raw.shape

    def heads(a, d):
        return a.reshape(B, T, A_HEADS, d)

    q = heads(jax.nn.silu(q_raw.astype(F32)) * A_DK ** -0.5, A_DK)
    v = heads(i_raw.astype(F32), A_DV)
    outs, states = [], []
    for d, f_raw in enumerate((f_raw_fwd, f_raw_bwd)):
        logf = heads(log_forget(f_raw, lb[d]), A_DK)
        k = -jnp.expm1(logf)
        if s_init is None:
            s0 = jnp.zeros((B, A_HEADS, A_DK, A_DV), F32)
        else:
            s0 = s_init[:, d].astype(F32)
        if d == 0:
            o, s = chunk_gated_recurrence(q, k, v, logf, s0)
        else:
            fl = lambda a: jnp.flip(a, axis=1)
            o, s = chunk_gated_recurrence(fl(q), fl(k), fl(v), fl(logf), s0)
            o = fl(o)
        outs.append(o)
        states.append(s)
    o = (outs[0] + outs[1]).astype(q_raw.dtype)
    o = rmsnorm(o, gnorm) * jax.nn.silu(heads(g_raw, A_DV))
    return o.reshape(B, T, A_W), jnp.stack(states, axis=1)


def expand_kv(c_kv, w_kv_up):
    B, T, _ = c_kv.shape
    kv = (c_kv @ w_kv_up).reshape(B, T, C_HEADS, C_NOPE + C_V)
    return kv[..., :C_NOPE], kv[..., C_NOPE:]


def mla_attention(q_nope, q_rope, k_nope, k_rope, v):
    B, Tq, H, _ = q_nope.shape
    nb = Tq // Q_BLOCK
    scale = (C_NOPE + C_ROPE) ** -0.5

    def blk(a):
        return a.reshape(B, nb, Q_BLOCK, *a.shape[2:]).swapaxes(0, 1)

    def one(args):
        qn, qr = args
        s = jnp.einsum('bqhd,bkhd->bhqk', qn, k_nope) + jnp.einsum('bqhr,bkr->bhqk', qr, k_rope)
        p = jax.nn.softmax(s.astype(F32) * scale, axis=-1).astype(v.dtype)
        return jnp.einsum('bhqk,bkhv->bqhv', p, v)

    o = lax.map(one, (blk(q_nope), blk(q_rope)))
    return o.swapaxes(0, 1).reshape(B, Tq, H * C_V)


def trunk_layer(x, mod, lw, lb, hgrn_init, ctx_cache, rope):
    (norm1, w_in, hgrn_gnorm, w_o_hgrn, conv_w, w_o_conv, q_norm, w_q_up, kv_norm, w_kv_up,
     w_o_mla, w_out, norm2, w_up, ffn_conv_w, w_down) = lw
    B, T, _ = x.shape
    shift1, scale1, gate1, shift2, scale2, gate2 = jnp.split(mod, 6, axis=-1)
    xn = rmsnorm(x, norm1) * (1 + scale1) + shift1
    (qA, fA_f, fA_b, iA, gA, bB, cB, hB, cq, ckv, gates) = split_cols(xn @ w_in, IN_SIZES)

    yA, states = hgrn2_mixer(qA, fA_f, fA_b, iA, gA, lb, hgrn_gnorm, hgrn_init)
    yA = yA @ w_o_hgrn

    yB = (bB * dwconv3(cB * hB, conv_w)) @ w_o_conv

    qh = (rmsnorm(cq, q_norm) @ w_q_up).reshape(B, T, C_HEADS, C_NOPE + C_ROPE)
    q_nope, q_rope = qh[..., :C_NOPE], qh[..., C_NOPE:]
    c_kv = rmsnorm(ckv[..., :C_KV_LORA], kv_norm)
    k_rope = ckv[..., C_KV_LORA:]
    own_cache = jnp.concatenate([c_kv, k_rope], axis=-1)
    if rope is not None:
        cos, sin = rope
        q_rope = apply_axial_rope(q_rope, cos[:, None], sin[:, None])
        k_rope = apply_axial_rope(k_rope, cos, sin)
    k_nope, v = expand_kv(c_kv, w_kv_up)
    if ctx_cache is not None:
        kn_c, v_c = expand_kv(ctx_cache[..., :C_KV_LORA], w_kv_up)
        k_nope = jnp.concatenate([k_nope, kn_c], axis=1)
        k_rope = jnp.concatenate([k_rope, ctx_cache[..., C_KV_LORA:]], axis=1)
        v = jnp.concatenate([v, v_c], axis=1)
    yC = mla_attention(q_nope, q_rope, k_nope, k_rope, v) @ w_o_mla

    g = jax.nn.sigmoid(gates.reshape(B, T, N_BRANCH, D_MODEL))
    h = g[:, :, 0] * yA + g[:, :, 1] * yB + g[:, :, 2] * yC
    x = x + gate1 * (h @ w_out)

    xn2 = rmsnorm(x, norm2) * (1 + scale2) + shift2
    u = dwconv3(xn2 @ w_up, ffn_conv_w)
    a, bval = u[..., :D_FF], u[..., D_FF:]
    x = x + gate2 * ((jax.nn.silu(a) * bval) @ w_down)
    return x, states, own_cache


def setup_inputs(seed: int = 0) -> dict:
    key = jax.random.key(seed)
    ks = jax.random.split(key, 32)

    def nrm(k, shape, scale):
        return jax.random.normal(k, shape, F32) * scale

    def gain(k, shape):
        return 1.0 + 0.05 * jax.random.normal(k, shape, F32)

    return {
        'x_prompt': nrm(ks[0], (BATCH, SEQ, D_MODEL), 1.0),
        'x_sample': nrm(ks[1], (DEC_BATCH, DEC_SEQ, D_MODEL), 1.0),
        'state_hgrn': nrm(ks[2], (DEC_BATCH, DEPTH, 2, A_HEADS, A_DK, A_DV), 0.5),
        'cache_mla': nrm(ks[3], (DEC_BATCH, DEPTH, PAST_LEN, C_KV_LORA + C_ROPE), 1.0),
        'c': nrm(ks[4], (DEC_BATCH, D_MODEL), 1.0),
        'c_ctx': nrm(ks[5], (D_MODEL,), 1.0),
        'w_ada': nrm(ks[6], (DEPTH, D_MODEL, 6 * D_MODEL), 0.3 * D_MODEL ** -0.5),
        'b_ada': nrm(ks[7], (DEPTH, 6 * D_MODEL), 0.01),
        'norm1': gain(ks[8], (DEPTH, D_MODEL)),
        'w_in': nrm(ks[9], (DEPTH, D_MODEL, IN_COLS), D_MODEL ** -0.5),
        'hgrn_lb_logits': nrm(ks[10], (DEPTH, 2, A_W), 0.5),
        'hgrn_gnorm': gain(ks[11], (DEPTH, A_DV)),
        'w_o_hgrn': nrm(ks[12], (DEPTH, A_W, D_MODEL), A_W ** -0.5),
        'conv_w': nrm(ks[13], (DEPTH, CONV_W, B_W), CONV_W ** -0.5),
        'w_o_conv': nrm(ks[14], (DEPTH, B_W, D_MODEL), B_W ** -0.5),
        'mla_q_norm': gain(ks[15], (DEPTH, C_Q_LORA)),
        'w_q_up': nrm(ks[16], (DEPTH, C_Q_LORA, C_HEADS * (C_NOPE + C_ROPE)), C_Q_LORA ** -0.5),
        'mla_kv_norm': gain(ks[17], (DEPTH, C_KV_LORA)),
        'w_kv_up': nrm(ks[18], (DEPTH, C_KV_LORA, C_HEADS * (C_NOPE + C_V)), C_KV_LORA ** -0.5),
        'w_o_mla': nrm(ks[19], (DEPTH, C_HEADS * C_V, D_MODEL), (C_HEADS * C_V) ** -0.5),
        'w_out': nrm(ks[20], (DEPTH, D_MODEL, D_MODEL), D_MODEL ** -0.5),
        'norm2': gain(ks[21], (DEPTH, D_MODEL)),
        'w_up': nrm(ks[22], (DEPTH, D_MODEL, 2 * D_FF), D_MODEL ** -0.5),
        'ffn_conv_w': nrm(ks[23], (DEPTH, CONV_W, 2 * D_FF), CONV_W ** -0.5),
        'w_down': nrm(ks[24], (DEPTH, D_FF, D_MODEL), D_FF ** -0.5),
        'final_norm': gain(ks[25], (D_MODEL,)),
    }


def reference(x_prompt, x_sample, state_hgrn, cache_mla, c, c_ctx, w_ada, b_ada, norm1, w_in,
              hgrn_lb_logits, hgrn_gnorm, w_o_hgrn, conv_w, w_o_conv, mla_q_norm, w_q_up,
              mla_kv_norm, w_kv_up, w_o_mla, w_out, norm2, w_up, ffn_conv_w, w_down, final_norm):
    p = jax.nn.softmax(hgrn_lb_logits.astype(F32), axis=0)
    lb_all = jnp.cumsum(p, axis=0) - p[0]
    rope = axial_rope_tables(x_sample.shape[1])
    xp, xs = x_prompt, x_sample
    new_states, new_caches = [], []
    for l in range(DEPTH):
        lw = (norm1[l], w_in[l], hgrn_gnorm[l], w_o_hgrn[l], conv_w[l], w_o_conv[l], mla_q_norm[l],
              w_q_up[l], mla_kv_norm[l], w_kv_up[l], w_o_mla[l], w_out[l], norm2[l], w_up[l],
              ffn_conv_w[l], w_down[l])
        mod_ctx = (jax.nn.silu(c_ctx) @ w_ada[l] + b_ada[l])[None, None, :]
        xp, st, kvc = trunk_layer(xp, mod_ctx, lw, lb_all[l], None, None, None)
        new_states.append(st.astype(x_prompt.dtype))
        new_caches.append(kvc)
        mod_lat = (jax.nn.silu(c) @ w_ada[l] + b_ada[l])[:, None, :]
        xs, _, _ = trunk_layer(xs, mod_lat, lw, lb_all[l], state_hgrn[:, l], cache_mla[:, l], rope)
    y_prompt = rmsnorm(xp, final_norm)
    y_sample = rmsnorm(xs, final_norm)
    new_state_hgrn = jnp.stack(new_states, axis=1)
    new_cache_mla = jnp.stack(new_caches, axis=1)
    return (y_prompt, y_sample, new_state_hgrn, new_cache_mla)
```

```python
import functools

import numpy as np
import jax
import jax.numpy as jnp
from jax import lax
from jax.experimental import pallas as pl
from jax.experimental.pallas import tpu as pltpu

F32 = jnp.float32
BF16 = jnp.bfloat16

D_MODEL = 1024
GRID_W = 64
EPS = 1e-6
A_HEADS = 4
A_DK = 128
A_DV = 128
A_W = A_HEADS * A_DK
LB_FLOOR = 1e-30
B_W = 512
C_HEADS = 8
C_NOPE = 64
C_ROPE = 32
C_V = 64
C_Q_LORA = 384
C_KV_LORA = 256
ROPE_THETA = 10000.0
D_FF = 2816
N_BRANCH = 3
CACHE_W = C_KV_LORA + C_ROPE

LANES = 128
SUBLANES = 8
VMEM_LIMIT_BYTES = 56 * 1024 * 1024

IN_ROWS = 256
HGRN_CHUNK = 128
ATTN_Q_ROWS = 256
MERGE_ROWS = 256
FFN_ROWS = 1024
FFN_COLS = 256
MOD_ROWS = 16
ADA_COLS = 1536

OFF_CQ = 8 * A_W
OFF_KV = OFF_CQ + C_Q_LORA
KV_EXT = 3 * LANES
OFF_GATE = OFF_KV + KV_EXT
IN_COLS_EXT = OFF_GATE + N_BRANCH * D_MODEL


def _cparams(sem):
    return pltpu.CompilerParams(dimension_semantics=sem, vmem_limit_bytes=VMEM_LIMIT_BYTES)


def _dot(a, b):
    return jnp.dot(a, b, preferred_element_type=F32)


def _dot_nt(a, b):
    return lax.dot_general(a, b, (((1,), (1,)), ((), ())), preferred_element_type=F32)


def _dot_tn(a, b):
    return lax.dot_general(a, b, (((0,), (0,)), ((), ())), preferred_element_type=F32)


def _rms(x, g):
    ms = jnp.mean(x * x, axis=-1, keepdims=True)
    return x * lax.rsqrt(ms + EPS) * g


def _sigmoid_pair(z):
    t = jnp.exp(-jnp.abs(z))
    r = 1.0 / (1.0 + t)
    big, small = r, t * r
    pos = z >= 0
    return jnp.where(pos, big, small), jnp.where(pos, small, big)


def _silu(z):
    s, _ = _sigmoid_pair(z)
    return z * s


def _ada_kernel(c_ref, w_ref, b_ref, o_ref):
    s = _silu(c_ref[...])
    o_ref[...] = jnp.dot(s, w_ref[...], preferred_element_type=F32,
                         precision=lax.Precision.HIGHEST) + b_ref[...]


def _ada_call(cvec, w_ada, b_ada):
    depth = w_ada.shape[0]
    ncol = w_ada.shape[2] // ADA_COLS
    return pl.pallas_call(
        _ada_kernel,
        grid=(depth, ncol),
        in_specs=[
            pl.BlockSpec((MOD_ROWS, D_MODEL), lambda l, j: (0, 0)),
            pl.BlockSpec((None, D_MODEL, ADA_COLS), lambda l, j: (l, 0, j)),
            pl.BlockSpec((None, 1, ADA_COLS), lambda l, j: (l, 0, j)),
        ],
        out_specs=pl.BlockSpec((None, MOD_ROWS, ADA_COLS), lambda l, j: (l, 0, j)),
        out_shape=jax.ShapeDtypeStruct((depth, MOD_ROWS, w_ada.shape[2]), F32),
        compiler_params=_cparams(("arbitrary", "arbitrary")),
    )(cvec, w_ada, b_ada.reshape(depth, 1, -1))


def _lb_kernel(x_ref, o_ref):
    depth = x_ref.shape[0]
    xs = [x_ref[l] for l in range(depth)]
    m = xs[0]
    for l in range(1, depth):
        m = jnp.maximum(m, xs[l])
    es = [jnp.exp(x - m) for x in xs]
    tot = es[0]
    for l in range(1, depth):
        tot = tot + es[l]
    ps = [e / tot for e in es]
    run = jnp.zeros_like(ps[0])
    for l in range(depth):
        run = run + ps[l]
        o_ref[l] = run - ps[0]


def _lb_call(logits):
    return pl.pallas_call(
        _lb_kernel,
        out_shape=jax.ShapeDtypeStruct(logits.shape, F32),
    )(logits)


def _place_rope_key(z3, tabk):
    tmp = z3 * tabk
    rot = tmp + pltpu.roll(tmp, LANES - C_ROPE, axis=1)
    lane = lax.broadcasted_iota(jnp.int32, rot.shape, 1)
    rot = jnp.where(lane < C_ROPE, rot, 0.0)
    return pltpu.roll(rot, C_NOPE, axis=1) + pltpu.roll(rot, C_NOPE + C_ROPE, axis=1)


def _in_kernel(x_ref, mod_ref, n1_ref, w_ref, lb_ref, qn_ref, wq_ref, kvn_ref, wkv_ref,
               tabq_ref, tabk_ref,
               q_ref, gf_ref, gb_ref, kf_ref, kb_ref, v_ref, ga_ref, bb_ref, u_ref,
               qfull_ref, kfull_ref, vv_ref, sg_ref, *cache_ref):
    x = x_ref[...]
    shift1 = mod_ref[0:1, :]
    scale1 = mod_ref[1:2, :]
    xn = _rms(x, n1_ref[...]) * (1.0 + scale1) + shift1
    xb = xn.astype(BF16)

    def proj(off, width):
        return _dot(xb, w_ref[:, off:off + width])

    q_ref[...] = (_silu(proj(0, A_W)) * (A_DK ** -0.5)).astype(BF16)
    for d, (g_ref, k_ref) in enumerate(((gf_ref, kf_ref), (gb_ref, kb_ref))):
        z = proj((1 + d) * A_W, A_W)
        lb = lb_ref[d:d + 1, :]
        sp, sn = _sigmoid_pair(z)
        g_ref[...] = jnp.log(jnp.maximum(lb, LB_FLOOR) + (1.0 - lb) * sp)
        k_ref[...] = (1.0 - lb) * sn - jnp.maximum(LB_FLOOR - lb, 0.0)
    v_ref[...] = proj(3 * A_W, A_W).astype(BF16)
    ga_ref[...] = _silu(proj(4 * A_W, A_W))
    bb_ref[...] = proj(5 * A_W, B_W)
    u_ref[...] = proj(6 * A_W, B_W) * proj(7 * A_W, B_W)
    cqn = _rms(proj(OFF_CQ, C_Q_LORA), qn_ref[...]).astype(BF16)
    tabq = tabq_ref[...]
    for h in range(C_HEADS):
        qz = _dot(cqn, wq_ref[:, h * LANES:(h + 1) * LANES])
        qfull_ref[:, h * LANES:(h + 1) * LANES] = (qz * tabq).astype(BF16)
    zkv = proj(OFF_KV, KV_EXT)
    ckvn = _rms(zkv[:, :C_KV_LORA], kvn_ref[...])
    z3 = zkv[:, C_KV_LORA:]
    if cache_ref:
        cache_ref[0][:, :C_KV_LORA] = ckvn
        cache_ref[0][:, C_KV_LORA:] = z3[:, :C_ROPE]
    krr = _place_rope_key(z3, tabk_ref[...])
    cb = ckvn.astype(BF16)
    for h in range(C_HEADS):
        kz = _dot(cb, wkv_ref[:, h * LANES:(h + 1) * LANES])
        kfull_ref[:, h * LANES:(h + 1) * LANES] = (kz + krr).astype(BF16)
    vv_ref[...] = _dot(cb, wkv_ref[:, C_HEADS * LANES:]).astype(BF16)
    for j in range(N_BRANCH * D_MODEL // A_W):
        sp, _ = _sigmoid_pair(proj(OFF_GATE + j * A_W, A_W))
        sg_ref[:, j * A_W:(j + 1) * A_W] = sp


def _const_spec(shape):
    nd = len(shape)
    return pl.BlockSpec(shape, lambda i: (0,) * nd, pipeline_mode=pl.Buffered(1))


def _in_call(x2d, mod3, mod_row, n1, w_ext, lb, qn, wq, kvn, wkv, tabq, tabk, seq, want_cache):
    n = x2d.shape[0]
    tm = IN_ROWS
    tab_blocks = tabq.shape[0] // tm

    def row(width):
        return pl.BlockSpec((tm, width), lambda i: (i, 0))

    outs = [
        (A_W, BF16),
        (A_W, F32), (A_W, F32),
        (A_W, F32), (A_W, F32),
        (A_W, BF16),
        (A_W, F32),
        (B_W, F32),
        (B_W, F32),
        (C_HEADS * LANES, BF16),
        (C_HEADS * LANES, BF16),
        (C_HEADS * C_V, BF16),
        (N_BRANCH * D_MODEL, F32),
    ]
    if want_cache:
        outs.append((CACHE_W, F32))
    return pl.pallas_call(
        _in_kernel,
        grid=(n // tm,),
        in_specs=[
            row(D_MODEL),
            pl.BlockSpec((None, 6, D_MODEL), lambda i: (mod_row(i * tm), 0, 0)),
            _const_spec((1, D_MODEL)),
            _const_spec((D_MODEL, IN_COLS_EXT)),
            _const_spec((2, A_W)),
            _const_spec((1, C_Q_LORA)),
            _const_spec((C_Q_LORA, C_HEADS * LANES)),
            _const_spec((1, C_KV_LORA)),
            _const_spec((C_KV_LORA, C_HEADS * LANES + C_HEADS * C_V)),
            pl.BlockSpec((tm, LANES), lambda i: (i % tab_blocks, 0)),
            pl.BlockSpec((tm, LANES), lambda i: (i % tab_blocks, 0)),
        ],
        out_specs=[row(w) for w, _ in outs],
        out_shape=[jax.ShapeDtypeStruct((n, w), dt) for w, dt in outs],
        compiler_params=_cparams(("arbitrary",)),
    )(x2d, mod3, n1, w_ext, lb, qn, wq, kvn, wkv, tabq, tabk)


def _cache_kernel(c_ref, wkv_ref, kfull_ref, vv_ref):
    c = c_ref[...]
    cb = c[:, :C_KV_LORA].astype(BF16)
    z3 = c[:, C_KV_LORA:]
    lane = lax.broadcasted_iota(jnp.int32, z3.shape, 1)
    kr = jnp.where(lane < C_ROPE, z3, 0.0)
    krr = pltpu.roll(kr, C_NOPE, axis=1) + pltpu.roll(kr, C_NOPE + C_ROPE, axis=1)
    for h in range(C_HEADS):
        kz = _dot(cb, wkv_ref[:, h * LANES:(h + 1) * LANES])
        kfull_ref[:, h * LANES:(h + 1) * LANES] = (kz + krr).astype(BF16)
    vv_ref[...] = _dot(cb, wkv_ref[:, C_HEADS * LANES:]).astype(BF16)


def _cache_call(cache_pad, wkv):
    b, p, w = cache_pad.shape
    return pl.pallas_call(
        _cache_kernel,
        grid=(b,),
        in_specs=[
            pl.BlockSpec((None, p, w), lambda i: (i, 0, 0)),
            pl.BlockSpec(wkv.shape, lambda i: (0, 0)),
        ],
        out_specs=[
            pl.BlockSpec((None, p, C_HEADS * LANES), lambda i: (i, 0, 0)),
            pl.BlockSpec((None, p, C_HEADS * C_V), lambda i: (i, 0, 0)),
        ],
        out_shape=[
            jax.ShapeDtypeStruct((b, p, C_HEADS * LANES), BF16),
            jax.ShapeDtypeStruct((b, p, C_HEADS * C_V), BF16),
        ],
        compiler_params=_cparams(("arbitrary",)),
    )(cache_pad, wkv)


def _level_matrix(c):
    t = np.arange(c)[:, None]
    s = np.arange(c)[None, :]
    x = np.bitwise_xor(t, s)
    lev = np.floor(np.log2(np.maximum(x, 1))).astype(np.int32)
    lev = np.where(s == t, -1, lev)
    lev = np.where(s > t, -2, lev)
    return lev.astype(np.int32)


def _cumsum_rows(g):
    c = g.shape[0]
    rowi = lax.broadcasted_iota(jnp.int32, g.shape, 0) % SUBLANES
    x = g
    s = 1
    while s < SUBLANES:
        x = x + jnp.where(rowi >= s, pltpu.roll(x, s, axis=0), 0.0)
        s *= 2
    parts = []
    carry = None
    for j in range(c // SUBLANES):
        blk = x[j * SUBLANES:(j + 1) * SUBLANES, :]
        if carry is not None:
            blk = blk + carry
        parts.append(blk)
        carry = blk[SUBLANES - 1:SUBLANES, :]
    return jnp.concatenate(parts, axis=0)


def _group_reference(b, h, rev):
    c = b.shape[0]
    ref = h if rev else h - 1
    if 2 * h >= SUBLANES:
        parts = []
        for gs in range(0, c, 2 * h):
            parts.append(jnp.broadcast_to(b[gs + ref:gs + ref + 1, :], (2 * h, b.shape[1])))
        return jnp.concatenate(parts, axis=0)
    off = lax.broadcasted_iota(jnp.int32, b.shape, 0) % (2 * h)
    out = b
    for o in range(2 * h):
        delta = o - ref
        if delta == 0:
            continue
        out = jnp.where(off == o, pltpu.roll(b, delta % c, axis=0), out)
    return out


def _hgrn_chunk(q, k, g, v, st, lev, rev):
    c = q.shape[0]
    qf = q.astype(F32)
    b = _cumsum_rows(g)
    tot = b[c - 1:c, :]
    if rev:
        b = tot - b + g
    a = jnp.zeros((c, c), F32)
    nlev = int(np.log2(c))
    for hl in range(nlev):
        beta = _group_reference(b, 1 << hl, rev)
        x = jnp.exp(-jnp.abs(b - beta))
        p = _dot_nt((qf * x).astype(BF16), (k * x).astype(BF16))
        a = jnp.where(lev == hl, p, a)
    p0 = _dot_nt(q, k.astype(BF16))
    a = jnp.where(lev == -1, p0, a)
    o = _dot(a.astype(BF16), v) + _dot_nt((qf * jnp.exp(b)).astype(BF16), st.astype(BF16))
    ks = (k * jnp.exp(tot - b)).astype(BF16)
    st_new = st * jnp.exp(tot) + _dot_tn(v, ks)
    return o, st_new


def _hgrn_kernel(*refs, has_init, want_state):
    refs = list(refs)
    q_ref, gf_ref, gb_ref, kf_ref, kb_ref, v_ref, ga_ref, gn_ref, levf_ref, levb_ref = refs[:10]
    pos = 10
    s0_ref = None
    if has_init:
        s0_ref = refs[pos]
        pos += 1
    y_ref = refs[pos]
    pos += 1
    sfin_ref = None
    if want_state:
        sfin_ref = refs[pos]
        pos += 1
    o_scr, st_scr = refs[pos], refs[pos + 1]

    t = q_ref.shape[0]
    c = HGRN_CHUNK
    n = t // c
    for d in range(2):
        if has_init:
            st_scr[d] = s0_ref[d].T
        else:
            st_scr[d] = jnp.zeros((A_DV, A_DK), F32)
    o_scr[...] = jnp.zeros_like(o_scr)

    def body(i, carry):
        for d, (g_ref, k_ref, lev_ref) in enumerate(((gf_ref, kf_ref, levf_ref), (gb_ref, kb_ref, levb_ref))):
            ci = i if d == 0 else n - 1 - i
            sl = pl.ds(pl.multiple_of(ci * c, c), c)
            o, st_new = _hgrn_chunk(q_ref[sl, :], k_ref[sl, :], g_ref[sl, :], v_ref[sl, :],
                                    st_scr[d], lev_ref[...], rev=(d == 1))
            st_scr[d] = st_new
            o_scr[sl, :] += o
        return carry

    lax.fori_loop(0, n, body, 0)
    if want_state:
        for d in range(2):
            sfin_ref[d] = st_scr[d].T
    y_ref[...] = (_rms(o_scr[...], gn_ref[...]) * ga_ref[...]).astype(BF16)


def _hgrn_call(q, gf, gb, kf, kb, v, ga, gnorm, s_init, batch, seq, want_state):
    c = HGRN_CHUNK
    lev = _level_matrix(c)
    levf = jnp.asarray(lev)
    levb = jnp.asarray(lev.T.copy())

    def r3(a):
        return a.reshape(batch, seq, A_W)

    col = pl.BlockSpec((None, seq, A_DK), lambda b, h: (b, 0, h))
    cst = pl.BlockSpec((c, c), lambda b, h: (0, 0))
    st_spec = pl.BlockSpec((None, 2, None, A_DK, A_DV), lambda b, h: (b, 0, h, 0, 0))
    in_specs = [col] * 7 + [pl.BlockSpec((1, A_DV), lambda b, h: (0, 0)), cst, cst]
    args = [r3(q), r3(gf), r3(gb), r3(kf), r3(kb), r3(v), r3(ga), gnorm, levf, levb]
    has_init = s_init is not None
    if has_init:
        in_specs.append(st_spec)
        args.append(s_init)
    out_specs = [col]
    out_shape = [jax.ShapeDtypeStruct((batch, seq, A_W), BF16)]
    if want_state:
        out_specs.append(st_spec)
        out_shape.append(jax.ShapeDtypeStruct((batch, 2, A_HEADS, A_DK, A_DV), F32))
    res = pl.pallas_call(
        functools.partial(_hgrn_kernel, has_init=has_init, want_state=want_state),
        grid=(batch, A_HEADS),
        in_specs=in_specs,
        out_specs=out_specs,
        out_shape=out_shape,
        scratch_shapes=[pltpu.VMEM((seq, A_DV), F32), pltpu.VMEM((2, A_DV, A_DK), F32)],
        compiler_params=_cparams(("arbitrary", "arbitrary")),
    )(*args)
    y = res[0].reshape(batch * seq, A_W)
    return y, (res[1] if want_state else None)


def _attn_kernel(*refs, has_cache):
    if has_cache:
        q_ref, k_ref, v_ref, kc_ref, vc_ref, y_ref = refs
    else:
        q_ref, k_ref, v_ref, y_ref = refs
    lane = lax.broadcasted_iota(jnp.int32, (q_ref.shape[0], LANES), 1)
    for hp in range(C_HEADS // 2):
        vs = slice(hp * LANES, (hp + 1) * LANES)
        outs = []
        for hh in range(2):
            hs = slice((2 * hp + hh) * LANES, (2 * hp + hh + 1) * LANES)
            qh = q_ref[:, hs]
            s1 = _dot_nt(qh, k_ref[:, hs])
            m = jnp.max(s1, axis=-1, keepdims=True)
            if has_cache:
                s2 = _dot_nt(qh, kc_ref[:, hs])
                m = jnp.maximum(m, jnp.max(s2, axis=-1, keepdims=True))
            p1 = jnp.exp(s1 - m)
            l = jnp.sum(p1, axis=-1, keepdims=True)
            o = _dot(p1.astype(BF16), v_ref[:, vs])
            if has_cache:
                p2 = jnp.exp(s2 - m)
                l = l + jnp.sum(p2, axis=-1, keepdims=True)
                o = o + _dot(p2.astype(BF16), vc_ref[:, vs])
            outs.append(o / l)
        y_ref[:, vs] = jnp.where(lane < C_V, outs[0], outs[1]).astype(BF16)


def _attn_call(qfull, kfull, vv, kc, vc, batch, seq):
    tq = ATTN_Q_ROWS
    has_cache = kc is not None
    kw = C_HEADS * LANES
    vw = C_HEADS * C_V
    in_specs = [
        pl.BlockSpec((None, tq, kw), lambda b, i: (b, i, 0)),
        pl.BlockSpec((None, seq, kw), lambda b, i: (b, 0, 0)),
        pl.BlockSpec((None, seq, vw), lambda b, i: (b, 0, 0)),
    ]
    args = [qfull.reshape(batch, seq, kw), kfull.reshape(batch, seq, kw), vv.reshape(batch, seq, vw)]
    if has_cache:
        p = kc.shape[1]
        in_specs += [
            pl.BlockSpec((None, p, kw), lambda b, i: (b, 0, 0)),
            pl.BlockSpec((None, p, vw), lambda b, i: (b, 0, 0)),
        ]
        args += [kc, vc]
    y = pl.pallas_call(
        functools.partial(_attn_kernel, has_cache=has_cache),
        grid=(batch, seq // tq),
        in_specs=in_specs,
        out_specs=pl.BlockSpec((None, tq, vw), lambda b, i: (b, i, 0)),
        out_shape=jax.ShapeDtypeStruct((batch, seq, vw), BF16),
        compiler_params=_cparams(("arbitrary", "arbitrary")),
    )(*args)
    return y.reshape(batch * seq, vw)


def _shift_rows(u, prev_row, next_row, pos, seq):
    tm = u.shape[0]
    rowi = lax.broadcasted_iota(jnp.int32, u.shape, 0)
    up = jnp.where(rowi == 0, prev_row, pltpu.roll(u, 1, axis=0))
    dn = jnp.where(rowi == tm - 1, next_row, pltpu.roll(u, tm - 1, axis=0))
    up = jnp.where(pos == 0, 0.0, up)
    dn = jnp.where(pos == seq - 1, 0.0, dn)
    return up, dn


def _merge_kernel(x_ref, ya_ref, yc_ref, bb_ref, u_ref, up_ref, un_ref, sg_ref, mod_ref,
                  wa_ref, wb_ref, wc_ref, wout_ref, cw_ref, n2_ref, x1_ref, xn2_ref, *, seq):
    tm = x_ref.shape[0]
    i = pl.program_id(0)
    u = u_ref[...]
    pos = (i * tm + lax.broadcasted_iota(jnp.int32, u.shape, 0)) % seq
    up, dn = _shift_rows(u, up_ref[SUBLANES - 1:SUBLANES, :], un_ref[0:1, :], pos, seq)
    conv = up * cw_ref[0:1, :] + u * cw_ref[1:2, :] + dn * cw_ref[2:3, :]
    yb = (bb_ref[...] * conv).astype(BF16)
    h = (sg_ref[:, 0:D_MODEL] * _dot(ya_ref[...], wa_ref[...])
         + sg_ref[:, D_MODEL:2 * D_MODEL] * _dot(yb, wb_ref[...])
         + sg_ref[:, 2 * D_MODEL:3 * D_MODEL] * _dot(yc_ref[...], wc_ref[...]))
    gate1 = mod_ref[2:3, :]
    x1 = x_ref[...] + gate1 * _dot(h.astype(BF16), wout_ref[...])
    x1_ref[...] = x1
    shift2 = mod_ref[3:4, :]
    scale2 = mod_ref[4:5, :]
    xn2_ref[...] = (_rms(x1, n2_ref[...]) * (1.0 + scale2) + shift2).astype(BF16)


def _merge_call(x2d, ya, yc, bb, u, sg, mod3, mod_row, wa, wb, wc, wout, cw, n2, seq):
    n = x2d.shape[0]
    tm = MERGE_ROWS
    nb8 = n // SUBLANES
    per = tm // SUBLANES

    def row(width):
        return pl.BlockSpec((tm, width), lambda i: (i, 0))

    def cst(shape):
        return pl.BlockSpec(shape, lambda i: (0, 0))

    return pl.pallas_call(
        functools.partial(_merge_kernel, seq=seq),
        grid=(n // tm,),
        in_specs=[
            row(D_MODEL), row(A_W), row(C_HEADS * C_V), row(B_W), row(B_W),
            pl.BlockSpec((SUBLANES, B_W), lambda i: (jnp.maximum(i * per - 1, 0), 0)),
            pl.BlockSpec((SUBLANES, B_W), lambda i: (jnp.minimum((i + 1) * per, nb8 - 1), 0)),
            row(N_BRANCH * D_MODEL),
            pl.BlockSpec((None, 6, D_MODEL), lambda i: (mod_row(i * tm), 0, 0)),
            cst(wa.shape), cst(wb.shape), cst(wc.shape), cst(wout.shape), cst(cw.shape), cst(n2.shape),
        ],
        out_specs=[row(D_MODEL), row(D_MODEL)],
        out_shape=[jax.ShapeDtypeStruct((n, D_MODEL), F32), jax.ShapeDtypeStruct((n, D_MODEL), BF16)],
        compiler_params=_cparams(("arbitrary",)),
    )(x2d, ya, yc, bb, u, u, u, sg, mod3, wa, wb, wc, wout, cw, n2)


FFN_HALO = 16


def _ffn_kernel(xn_ref, xp_ref, xx_ref, x1_ref, mod_ref, wa_ref, wb_ref, cwa_ref, cwb_ref, wd_ref,
                fn_ref, out_ref, acc_ref, *, seq, final):
    r = xn_ref.shape[0]
    i = pl.program_id(0)
    j = pl.program_id(1)

    @pl.when(j == 0)
    def _():
        acc_ref[...] = jnp.zeros_like(acc_ref)

    xn = xn_ref[...]
    xp = xp_ref[...]
    xx = xx_ref[...]
    pos = (i * r + lax.broadcasted_iota(jnp.int32, (r, wa_ref.shape[1]), 0)) % seq

    def conv_part(w_ref, cw_ref):
        w = w_ref[...]
        um = _dot(xn, w)
        prev_row = _dot(xp, w)[FFN_HALO - 1:FFN_HALO, :]
        next_row = _dot(xx, w)[0:1, :]
        up, dn = _shift_rows(um, prev_row, next_row, pos, seq)
        return up * cw_ref[0:1, :] + um * cw_ref[1:2, :] + dn * cw_ref[2:3, :]

    a = conv_part(wa_ref, cwa_ref)
    bval = conv_part(wb_ref, cwb_ref)
    hcol = (_silu(a) * bval).astype(BF16)
    acc_ref[...] += _dot(hcol, wd_ref[...])

    @pl.when(j == pl.num_programs(1) - 1)
    def _():
        gate2 = mod_ref[5:6, :]
        y = x1_ref[...] + gate2 * acc_ref[...]
        if final:
            y = _rms(y, fn_ref[...])
        out_ref[...] = y


def _ffn_call(xn2, x1, mod3, mod_row, w_up, cw, w_down, fnorm, seq, final):
    n = xn2.shape[0]
    r = FFN_ROWS
    tn = FFN_COLS
    nj = D_FF // tn
    nbh = n // FFN_HALO
    per = r // FFN_HALO
    return pl.pallas_call(
        functools.partial(_ffn_kernel, seq=seq, final=final),
        grid=(n // r, nj),
        in_specs=[
            pl.BlockSpec((r, D_MODEL), lambda i, j: (i, 0)),
            pl.BlockSpec((FFN_HALO, D_MODEL), lambda i, j: (jnp.maximum(i * per - 1, 0), 0)),
            pl.BlockSpec((FFN_HALO, D_MODEL), lambda i, j: (jnp.minimum((i + 1) * per, nbh - 1), 0)),
            pl.BlockSpec((r, D_MODEL), lambda i, j: (i, 0)),
            pl.BlockSpec((None, 6, D_MODEL), lambda i, j: (mod_row(i * r), 0, 0)),
            pl.BlockSpec((D_MODEL, tn), lambda i, j: (0, j)),
            pl.BlockSpec((D_MODEL, tn), lambda i, j: (0, nj + j)),
            pl.BlockSpec((SUBLANES, tn), lambda i, j: (0, j)),
            pl.BlockSpec((SUBLANES, tn), lambda i, j: (0, nj + j)),
            pl.BlockSpec((tn, D_MODEL), lambda i, j: (j, 0)),
            pl.BlockSpec((1, D_MODEL), lambda i, j: (0, 0)),
        ],
        out_specs=pl.BlockSpec((r, D_MODEL), lambda i, j: (i, 0)),
        out_shape=jax.ShapeDtypeStruct((n, D_MODEL), F32),
        scratch_shapes=[pltpu.VMEM((r, D_MODEL), F32)],
        compiler_params=_cparams(("arbitrary", "arbitrary")),
    )(xn2, xn2, xn2, x1, mod3, w_up, w_up, cw, cw, w_down, fnorm)


def _rope_swap_perm():
    idx = np.arange(C_ROPE).reshape(2, 2, C_ROPE // 4)
    return idx[:, ::-1, :].reshape(-1)


def _pack_w_in(w_in):
    swap = _rope_swap_perm()
    kv = w_in[:, OFF_KV:OFF_KV + CACHE_W]
    kr = kv[:, C_KV_LORA:]
    kv_ext = jnp.concatenate(
        [kv, kr[:, swap], jnp.zeros((D_MODEL, KV_EXT - CACHE_W - C_ROPE), w_in.dtype)], axis=1)
    return jnp.concatenate(
        [w_in[:, :OFF_KV], kv_ext, w_in[:, OFF_KV + CACHE_W:]], axis=1).astype(BF16)


def _pack_w_q_up(w):
    swap = _rope_swap_perm()
    w3 = w.reshape(C_Q_LORA, C_HEADS, C_NOPE + C_ROPE)
    rope = w3[:, :, C_NOPE:]
    return jnp.concatenate([w3, rope[:, :, swap]], axis=2).reshape(C_Q_LORA, C_HEADS * LANES).astype(BF16)


def _pack_w_kv_up(w):
    w3 = w.reshape(C_KV_LORA, C_HEADS, C_NOPE + C_V)
    kn = jnp.concatenate([w3[:, :, :C_NOPE], jnp.zeros((C_KV_LORA, C_HEADS, LANES - C_NOPE), w.dtype)], axis=2)
    return jnp.concatenate(
        [kn.reshape(C_KV_LORA, C_HEADS * LANES), w3[:, :, C_NOPE:].reshape(C_KV_LORA, C_HEADS * C_V)],
        axis=1).astype(BF16)


def _pad_rows(a, rows):
    return jnp.concatenate([a, jnp.zeros((rows - a.shape[0],) + a.shape[1:], a.dtype)], axis=0)


def _rope_tables(seq, rotate):
    scale = (C_NOPE + C_ROPE) ** -0.5
    n_freq = C_ROPE // 4
    if rotate:
        rows = seq // GRID_W
        r = jnp.repeat(jnp.arange(rows, dtype=F32), GRID_W)
        col = jnp.tile(jnp.arange(GRID_W, dtype=F32), rows)
        freq = ROPE_THETA ** (-jnp.arange(n_freq, dtype=F32) / n_freq)
        ang = jnp.stack([r[:, None] * freq, col[:, None] * freq], axis=1)
        cos, sin = jnp.cos(ang), jnp.sin(ang)
    else:
        cos = jnp.ones((seq, 2, n_freq), F32)
        sin = jnp.zeros((seq, 2, n_freq), F32)
    cos_full = jnp.stack([cos, cos], axis=2).reshape(seq, C_ROPE)
    sin_signed = jnp.stack([-sin, sin], axis=2).reshape(seq, C_ROPE)
    tabq = scale * jnp.concatenate([jnp.ones((seq, C_NOPE), F32), cos_full, sin_signed], axis=1)
    tabk = jnp.concatenate([cos_full, sin_signed, jnp.zeros((seq, LANES - 2 * C_ROPE), F32)], axis=1)
    return tabq, tabk


def _layer(x2d, batch, seq, mod3, mod_row, lw, lb, s_init, cache_pad, tabs, want_ctx_outputs, final, fnorm):
    (n1, w_ext, gnorm, wa, cw, wb, qn, wq, kvn, wkv, wc, wout, n2, w_up, fcw, w_down) = lw
    tabq, tabk = tabs
    res = _in_call(x2d, mod3, mod_row, n1, w_ext, lb, qn, wq, kvn, wkv, tabq, tabk, seq, want_ctx_outputs)
    q, gf, gb, kf, kb, v, ga, bb, u, qfull, kfull, vv, sg = res[:13]
    own_cache = res[13].reshape(batch, seq, CACHE_W) if want_ctx_outputs else None
    ya, states = _hgrn_call(q, gf, gb, kf, kb, v, ga, gnorm, s_init, batch, seq, want_ctx_outputs)
    kc = vc = None
    if cache_pad is not None:
        kc, vc = _cache_call(cache_pad, wkv)
    yc = _attn_call(qfull, kfull, vv, kc, vc, batch, seq)
    x1, xn2 = _merge_call(x2d, ya, yc, bb, u, sg, mod3, mod_row, wa, wb, wc, wout, cw, n2, seq)
    x2 = _ffn_call(xn2, x1, mod3, mod_row, w_up, fcw, w_down, fnorm, seq, final)
    return x2, states, own_cache


def kernel(x_prompt, x_sample, state_hgrn, cache_mla, c, c_ctx, w_ada, b_ada, norm1, w_in, hgrn_lb_logits, hgrn_gnorm, w_o_hgrn, conv_w, w_o_conv, mla_q_norm, w_q_up, mla_kv_norm, w_kv_up, w_o_mla, w_out, norm2, w_up, ffn_conv_w, w_down, final_norm):
    depth = w_in.shape[0]
    bp, sp, _ = x_prompt.shape
    bs, ss, _ = x_sample.shape
    assert 1 + bs <= MOD_ROWS and sp % IN_ROWS == 0 and ss % IN_ROWS == 0
    assert (bp * sp) % FFN_ROWS == 0 and (bs * ss) % FFN_ROWS == 0 and FFN_ROWS % sp == 0 and ss % FFN_ROWS == 0

    cvec = _pad_rows(jnp.concatenate([c_ctx[None, :], c], axis=0), MOD_ROWS)
    mod = _ada_call(cvec, w_ada, b_ada)
    lb_all = _lb_call(hgrn_lb_logits)
    tabs_ctx = _rope_tables(IN_ROWS, rotate=False)
    tabs_lat = _rope_tables(ss, rotate=True)
    fnorm = final_norm.reshape(1, D_MODEL)
    cache_pad = jnp.concatenate(
        [cache_mla, jnp.zeros(cache_mla.shape[:-1] + (KV_EXT - CACHE_W,), cache_mla.dtype)], axis=-1)

    xp = x_prompt.reshape(bp * sp, D_MODEL)
    xs = x_sample.reshape(bs * ss, D_MODEL)
    new_states, new_caches = [], []
    for l in range(depth):
        lw = (norm1[l].reshape(1, -1), _pack_w_in(w_in[l]), hgrn_gnorm[l].reshape(1, -1),
              w_o_hgrn[l].astype(BF16), _pad_rows(conv_w[l], SUBLANES), w_o_conv[l].astype(BF16),
              mla_q_norm[l].reshape(1, -1), _pack_w_q_up(w_q_up[l]), mla_kv_norm[l].reshape(1, -1),
              _pack_w_kv_up(w_kv_up[l]), w_o_mla[l].astype(BF16), w_out[l].astype(BF16),
              norm2[l].reshape(1, -1), w_up[l].astype(BF16), _pad_rows(ffn_conv_w[l], SUBLANES),
              w_down[l].astype(BF16))
        mod3 = mod[l].reshape(MOD_ROWS, 6, D_MODEL)
        final = l == depth - 1
        xp, st, kvc = _layer(xp, bp, sp, mod3, lambda r: 0, lw, lb_all[l], None, None, tabs_ctx,
                             True, final, fnorm)
        new_states.append(st)
        new_caches.append(kvc)
        xs, _, _ = _layer(xs, bs, ss, mod3, lambda r: 1 + r // ss, lw, lb_all[l], state_hgrn[:, l],
                          cache_pad[:, l], tabs_lat, False, final, fnorm)
    y_prompt = xp.reshape(bp, sp, D_MODEL)
    y_sample = xs.reshape(bs, ss, D_MODEL)
    return (y_prompt, y_sample, jnp.stack(new_states, axis=1), jnp.stack(new_caches, axis=1))
```

```python
import functools

import numpy as np
import jax
import jax.numpy as jnp
from jax import lax
from jax.experimental import pallas as pl
from jax.experimental.pallas import tpu as pltpu

F32 = jnp.float32
BF16 = jnp.bfloat16

D_MODEL = 1024
GRID_W = 64
EPS = 1e-6
A_HEADS = 4
A_DK = 128
A_DV = 128
A_W = A_HEADS * A_DK
LB_FLOOR = 1e-30
B_W = 512
C_HEADS = 8
C_NOPE = 64
C_ROPE = 32
C_V = 64
C_Q_LORA = 384
C_KV_LORA = 256
ROPE_THETA = 10000.0
D_FF = 2816
N_BRANCH = 3
CACHE_W = C_KV_LORA + C_ROPE

LANES = 128
SUBLANES = 8
VMEM_LIMIT_BYTES = 56 * 1024 * 1024

IN_ROWS = 256
HGRN_CHUNK = 128
ATTN_Q_ROWS = 256
MERGE_ROWS = 256
FFN_ROWS = 1024
FFN_COLS = 256
MOD_ROWS = 16
ADA_COLS = 1536

OFF_CQ = 8 * A_W
OFF_KV = OFF_CQ + C_Q_LORA
KV_EXT = 3 * LANES
OFF_GATE = OFF_KV + KV_EXT
IN_COLS_EXT = OFF_GATE + N_BRANCH * D_MODEL


def _cparams(sem):
    return pltpu.CompilerParams(dimension_semantics=sem, vmem_limit_bytes=VMEM_LIMIT_BYTES)


def _dot(a, b):
    return jnp.dot(a, b, preferred_element_type=F32)


def _dot_nt(a, b):
    return lax.dot_general(a, b, (((1,), (1,)), ((), ())), preferred_element_type=F32)


def _dot_tn(a, b):
    return lax.dot_general(a, b, (((0,), (0,)), ((), ())), preferred_element_type=F32)


def _rms(x, g):
    ms = jnp.mean(x * x, axis=-1, keepdims=True)
    return x * lax.rsqrt(ms + EPS) * g


def _sigmoid_pair(z):
    t = jnp.exp(-jnp.abs(z))
    r = 1.0 / (1.0 + t)
    big, small = r, t * r
    pos = z >= 0
    return jnp.where(pos, big, small), jnp.where(pos, small, big)


def _silu(z):
    s, _ = _sigmoid_pair(z)
    return z * s


def _ada_kernel(c_ref, w_ref, b_ref, o_ref):
    s = _silu(c_ref[...])
    o_ref[...] = jnp.dot(s, w_ref[...], preferred_element_type=F32,
                         precision=lax.Precision.HIGHEST) + b_ref[...]


def _ada_call(cvec, w_ada, b_ada):
    depth = w_ada.shape[0]
    ncol = w_ada.shape[2] // ADA_COLS
    return pl.pallas_call(
        _ada_kernel,
        grid=(depth, ncol),
        in_specs=[
            pl.BlockSpec((MOD_ROWS, D_MODEL), lambda l, j: (0, 0)),
            pl.BlockSpec((None, D_MODEL, ADA_COLS), lambda l, j: (l, 0, j)),
            pl.BlockSpec((None, 1, ADA_COLS), lambda l, j: (l, 0, j)),
        ],
        out_specs=pl.BlockSpec((None, MOD_ROWS, ADA_COLS), lambda l, j: (l, 0, j)),
        out_shape=jax.ShapeDtypeStruct((depth, MOD_ROWS, w_ada.shape[2]), F32),
        compiler_params=_cparams(("arbitrary", "arbitrary")),
        name="ada_mod",
    )(cvec, w_ada, b_ada.reshape(depth, 1, -1))


def _lb_kernel(x_ref, o_ref):
    depth = x_ref.shape[0]
    xs = [x_ref[l] for l in range(depth)]
    m = xs[0]
    for l in range(1, depth):
        m = jnp.maximum(m, xs[l])
    es = [jnp.exp(x - m) for x in xs]
    tot = es[0]
    for l in range(1, depth):
        tot = tot + es[l]
    ps = [e / tot for e in es]
    run = jnp.zeros_like(ps[0])
    for l in range(depth):
        run = run + ps[l]
        o_ref[l] = run - ps[0]


def _lb_call(logits):
    return pl.pallas_call(
        _lb_kernel,
        out_shape=jax.ShapeDtypeStruct(logits.shape, F32),
        name="hgrn_lb",
    )(logits)


def _place_rope_key(z3, tabk):
    tmp = z3 * tabk
    rot = tmp + pltpu.roll(tmp, LANES - C_ROPE, axis=1)
    lane = lax.broadcasted_iota(jnp.int32, rot.shape, 1)
    rot = jnp.where(lane < C_ROPE, rot, 0.0)
    return pltpu.roll(rot, C_NOPE, axis=1) + pltpu.roll(rot, C_NOPE + C_ROPE, axis=1)


def _in_kernel(x_ref, mod_ref, n1_ref, w_ref, lb_ref, qn_ref, wq_ref, kvn_ref, wkv_ref,
               tabq_ref, tabk_ref,
               q_ref, gf_ref, gb_ref, kf_ref, kb_ref, v_ref, ga_ref, bb_ref, u_ref,
               qfull_ref, kfull_ref, vv_ref, sg_ref, *cache_ref):
    x = x_ref[...]
    shift1 = mod_ref[0:1, :]
    scale1 = mod_ref[1:2, :]
    xn = _rms(x, n1_ref[...]) * (1.0 + scale1) + shift1
    xb = xn.astype(BF16)

    def proj(off, width):
        return _dot(xb, w_ref[:, off:off + width])

    q_ref[...] = (_silu(proj(0, A_W)) * (A_DK ** -0.5)).astype(BF16)
    for d, (g_ref, k_ref) in enumerate(((gf_ref, kf_ref), (gb_ref, kb_ref))):
        z = proj((1 + d) * A_W, A_W)
        lb = lb_ref[d:d + 1, :]
        sp, sn = _sigmoid_pair(z)
        g_ref[...] = jnp.log(jnp.maximum(lb, LB_FLOOR) + (1.0 - lb) * sp)
        k_ref[...] = ((1.0 - lb) * sn - jnp.maximum(LB_FLOOR - lb, 0.0)).astype(BF16)
    v_ref[...] = proj(3 * A_W, A_W).astype(BF16)
    ga_ref[...] = _silu(proj(4 * A_W, A_W)).astype(BF16)
    bb_ref[...] = proj(5 * A_W, B_W).astype(BF16)
    u_ref[...] = (proj(6 * A_W, B_W) * proj(7 * A_W, B_W)).astype(BF16)
    cqn = _rms(proj(OFF_CQ, C_Q_LORA), qn_ref[...]).astype(BF16)
    tabq = tabq_ref[...]
    for h in range(C_HEADS):
        qz = _dot(cqn, wq_ref[:, h * LANES:(h + 1) * LANES])
        qfull_ref[:, h * LANES:(h + 1) * LANES] = (qz * tabq).astype(BF16)
    zkv = proj(OFF_KV, KV_EXT)
    ckvn = _rms(zkv[:, :C_KV_LORA], kvn_ref[...])
    z3 = zkv[:, C_KV_LORA:]
    if cache_ref:
        cache_ref[0][:, :C_KV_LORA] = ckvn
        cache_ref[0][:, C_KV_LORA:] = z3[:, :C_ROPE]
    krr = _place_rope_key(z3, tabk_ref[...])
    cb = ckvn.astype(BF16)
    for h in range(C_HEADS):
        kz = _dot(cb, wkv_ref[:, h * LANES:(h + 1) * LANES])
        kfull_ref[:, h * LANES:(h + 1) * LANES] = (kz + krr).astype(BF16)
    vv_ref[...] = _dot(cb, wkv_ref[:, C_HEADS * LANES:]).astype(BF16)
    for j in range(N_BRANCH * D_MODEL // A_W):
        sp, _ = _sigmoid_pair(proj(OFF_GATE + j * A_W, A_W))
        sg_ref[:, j * A_W:(j + 1) * A_W] = sp.astype(BF16)


def _const_spec(shape):
    nd = len(shape)
    return pl.BlockSpec(shape, lambda i: (0,) * nd, pipeline_mode=pl.Buffered(1))


def _in_call(x2d, mod3, mod_row, n1, w_ext, lb, qn, wq, kvn, wkv, tabq, tabk, seq, want_cache):
    n = x2d.shape[0]
    tm = IN_ROWS
    tab_blocks = tabq.shape[0] // tm

    def row(width):
        return pl.BlockSpec((tm, width), lambda i: (i, 0))

    outs = [
        (A_W, BF16),
        (A_W, F32), (A_W, F32),
        (A_W, BF16), (A_W, BF16),
        (A_W, BF16),
        (A_W, BF16),
        (B_W, BF16),
        (B_W, BF16),
        (C_HEADS * LANES, BF16),
        (C_HEADS * LANES, BF16),
        (C_HEADS * C_V, BF16),
        (N_BRANCH * D_MODEL, BF16),
    ]
    if want_cache:
        outs.append((CACHE_W, F32))
    return pl.pallas_call(
        _in_kernel,
        grid=(n // tm,),
        in_specs=[
            row(D_MODEL),
            pl.BlockSpec((None, 6, D_MODEL), lambda i: (mod_row(i * tm), 0, 0)),
            _const_spec((1, D_MODEL)),
            _const_spec((D_MODEL, IN_COLS_EXT)),
            _const_spec((2, A_W)),
            _const_spec((1, C_Q_LORA)),
            _const_spec((C_Q_LORA, C_HEADS * LANES)),
            _const_spec((1, C_KV_LORA)),
            _const_spec((C_KV_LORA, C_HEADS * LANES + C_HEADS * C_V)),
            pl.BlockSpec((tm, LANES), lambda i: (i % tab_blocks, 0)),
            pl.BlockSpec((tm, LANES), lambda i: (i % tab_blocks, 0)),
        ],
        out_specs=[row(w) for w, _ in outs],
        out_shape=[jax.ShapeDtypeStruct((n, w), dt) for w, dt in outs],
        compiler_params=_cparams(("arbitrary",)),
        name="in_proj",
    )(x2d, mod3, n1, w_ext, lb, qn, wq, kvn, wkv, tabq, tabk)


def _cache_kernel(c_ref, wkv_ref, kfull_ref, vv_ref):
    c = c_ref[...]
    cb = c[:, :C_KV_LORA].astype(BF16)
    z3 = c[:, C_KV_LORA:]
    lane = lax.broadcasted_iota(jnp.int32, z3.shape, 1)
    kr = jnp.where(lane < C_ROPE, z3, 0.0)
    krr = pltpu.roll(kr, C_NOPE, axis=1) + pltpu.roll(kr, C_NOPE + C_ROPE, axis=1)
    for h in range(C_HEADS):
        kz = _dot(cb, wkv_ref[:, h * LANES:(h + 1) * LANES])
        kfull_ref[:, h * LANES:(h + 1) * LANES] = (kz + krr).astype(BF16)
    vv_ref[...] = _dot(cb, wkv_ref[:, C_HEADS * LANES:]).astype(BF16)


def _cache_call(cache_pad, wkv):
    b, p, w = cache_pad.shape
    return pl.pallas_call(
        _cache_kernel,
        grid=(b,),
        in_specs=[
            pl.BlockSpec((None, p, w), lambda i: (i, 0, 0)),
            pl.BlockSpec(wkv.shape, lambda i: (0, 0)),
        ],
        out_specs=[
            pl.BlockSpec((None, p, C_HEADS * LANES), lambda i: (i, 0, 0)),
            pl.BlockSpec((None, p, C_HEADS * C_V), lambda i: (i, 0, 0)),
        ],
        out_shape=[
            jax.ShapeDtypeStruct((b, p, C_HEADS * LANES), BF16),
            jax.ShapeDtypeStruct((b, p, C_HEADS * C_V), BF16),
        ],
        compiler_params=_cparams(("arbitrary",)),
        name="cache_expand",
    )(cache_pad, wkv)


def _level_matrix(c):
    t = np.arange(c)[:, None]
    s = np.arange(c)[None, :]
    x = np.bitwise_xor(t, s)
    lev = np.floor(np.log2(np.maximum(x, 1))).astype(np.int32)
    lev = np.where(s == t, -1, lev)
    lev = np.where(s > t, -2, lev)
    return lev.astype(np.int32)


def _cumsum_rows(g):
    c = g.shape[0]
    rowi = lax.broadcasted_iota(jnp.int32, g.shape, 0) % SUBLANES
    x = g
    s = 1
    while s < SUBLANES:
        x = x + jnp.where(rowi >= s, pltpu.roll(x, s, axis=0), 0.0)
        s *= 2
    parts = []
    carry = None
    for j in range(c // SUBLANES):
        blk = x[j * SUBLANES:(j + 1) * SUBLANES, :]
        if carry is not None:
            blk = blk + carry
        parts.append(blk)
        carry = blk[SUBLANES - 1:SUBLANES, :]
    return jnp.concatenate(parts, axis=0)


def _group_reference(b, h, rev):
    c = b.shape[0]
    ref = h if rev else h - 1
    if 2 * h >= SUBLANES:
        parts = []
        for gs in range(0, c, 2 * h):
            parts.append(jnp.broadcast_to(b[gs + ref:gs + ref + 1, :], (2 * h, b.shape[1])))
        return jnp.concatenate(parts, axis=0)
    off = lax.broadcasted_iota(jnp.int32, b.shape, 0) % (2 * h)
    out = b
    for o in range(2 * h):
        delta = o - ref
        if delta == 0:
            continue
        out = jnp.where(off == o, pltpu.roll(b, delta % c, axis=0), out)
    return out


def _hgrn_chunk(q, k, g, v, st, lev, rev):
    c = q.shape[0]
    qf = q.astype(F32)
    b = _cumsum_rows(g)
    tot = b[c - 1:c, :]
    if rev:
        b = tot - b + g
    a = jnp.zeros((c, c), F32)
    nlev = int(np.log2(c))
    for hl in range(nlev):
        beta = _group_reference(b, 1 << hl, rev)
        x = jnp.exp(-jnp.abs(b - beta))
        p = _dot_nt((qf * x).astype(BF16), (k * x).astype(BF16))
        a = jnp.where(lev == hl, p, a)
    p0 = _dot_nt(q, k.astype(BF16))
    a = jnp.where(lev == -1, p0, a)
    o = _dot(a.astype(BF16), v) + _dot_nt((qf * jnp.exp(b)).astype(BF16), st.astype(BF16))
    ks = (k * jnp.exp(tot - b)).astype(BF16)
    st_new = st * jnp.exp(tot) + _dot_tn(v, ks)
    return o, st_new


def _hgrn_kernel(*refs, has_init, want_state):
    refs = list(refs)
    q_ref, gf_ref, gb_ref, kf_ref, kb_ref, v_ref, ga_ref, gn_ref, levf_ref, levb_ref = refs[:10]
    pos = 10
    s0_ref = None
    if has_init:
        s0_ref = refs[pos]
        pos += 1
    y_ref = refs[pos]
    pos += 1
    sfin_ref = None
    if want_state:
        sfin_ref = refs[pos]
        pos += 1
    o_scr, st_scr = refs[pos], refs[pos + 1]

    t = q_ref.shape[0]
    c = HGRN_CHUNK
    n = t // c
    for d in range(2):
        if has_init:
            st_scr[d] = s0_ref[d].T
        else:
            st_scr[d] = jnp.zeros((A_DV, A_DK), F32)
    o_scr[...] = jnp.zeros_like(o_scr)

    def body(i, carry):
        for d, (g_ref, k_ref, lev_ref) in enumerate(((gf_ref, kf_ref, levf_ref), (gb_ref, kb_ref, levb_ref))):
            ci = i if d == 0 else n - 1 - i
            sl = pl.ds(pl.multiple_of(ci * c, c), c)
            o, st_new = _hgrn_chunk(q_ref[sl, :], k_ref[sl, :], g_ref[sl, :], v_ref[sl, :],
                                    st_scr[d], lev_ref[...], rev=(d == 1))
            st_scr[d] = st_new
            o_scr[sl, :] += o
        return carry

    lax.fori_loop(0, n, body, 0)
    if want_state:
        for d in range(2):
            sfin_ref[d] = st_scr[d].T
    y_ref[...] = (_rms(o_scr[...], gn_ref[...]) * ga_ref[...]).astype(BF16)


def _hgrn_call(q, gf, gb, kf, kb, v, ga, gnorm, s_init, batch, seq, want_state):
    c = HGRN_CHUNK
    lev = _level_matrix(c)
    levf = jnp.asarray(lev)
    levb = jnp.asarray(lev.T.copy())

    def r3(a):
        return a.reshape(batch, seq, A_W)

    col = pl.BlockSpec((None, seq, A_DK), lambda b, h: (b, 0, h))
    cst = pl.BlockSpec((c, c), lambda b, h: (0, 0))
    st_spec = pl.BlockSpec((None, 2, None, A_DK, A_DV), lambda b, h: (b, 0, h, 0, 0))
    in_specs = [col] * 7 + [pl.BlockSpec((1, A_DV), lambda b, h: (0, 0)), cst, cst]
    args = [r3(q), r3(gf), r3(gb), r3(kf), r3(kb), r3(v), r3(ga), gnorm, levf, levb]
    has_init = s_init is not None
    if has_init:
        in_specs.append(st_spec)
        args.append(s_init)
    out_specs = [col]
    out_shape = [jax.ShapeDtypeStruct((batch, seq, A_W), BF16)]
    if want_state:
        out_specs.append(st_spec)
        out_shape.append(jax.ShapeDtypeStruct((batch, 2, A_HEADS, A_DK, A_DV), F32))
    res = pl.pallas_call(
        functools.partial(_hgrn_kernel, has_init=has_init, want_state=want_state),
        grid=(batch, A_HEADS),
        in_specs=in_specs,
        out_specs=out_specs,
        out_shape=out_shape,
        scratch_shapes=[pltpu.VMEM((seq, A_DV), F32), pltpu.VMEM((2, A_DV, A_DK), F32)],
        compiler_params=_cparams(("arbitrary", "arbitrary")),
        name="hgrn",
    )(*args)
    y = res[0].reshape(batch * seq, A_W)
    return y, (res[1] if want_state else None)


def _attn_kernel(*refs, has_cache):
    if has_cache:
        q_ref, k_ref, v_ref, kc_ref, vc_ref, y_ref = refs
    else:
        q_ref, k_ref, v_ref, y_ref = refs
    lane = lax.broadcasted_iota(jnp.int32, (q_ref.shape[0], LANES), 1)
    for hp in range(C_HEADS // 2):
        vs = slice(hp * LANES, (hp + 1) * LANES)
        outs = []
        for hh in range(2):
            hs = slice((2 * hp + hh) * LANES, (2 * hp + hh + 1) * LANES)
            qh = q_ref[:, hs]
            s1 = _dot_nt(qh, k_ref[:, hs])
            m = jnp.max(s1, axis=-1, keepdims=True)
            if has_cache:
                s2 = _dot_nt(qh, kc_ref[:, hs])
                m = jnp.maximum(m, jnp.max(s2, axis=-1, keepdims=True))
            p1 = jnp.exp(s1 - m)
            l = jnp.sum(p1, axis=-1, keepdims=True)
            o = _dot(p1.astype(BF16), v_ref[:, vs])
            if has_cache:
                p2 = jnp.exp(s2 - m)
                l = l + jnp.sum(p2, axis=-1, keepdims=True)
                o = o + _dot(p2.astype(BF16), vc_ref[:, vs])
            outs.append(o / l)
        y_ref[:, vs] = jnp.where(lane < C_V, outs[0], outs[1]).astype(BF16)


def _attn_call(qfull, kfull, vv, kc, vc, batch, seq):
    tq = ATTN_Q_ROWS
    has_cache = kc is not None
    kw = C_HEADS * LANES
    vw = C_HEADS * C_V
    in_specs = [
        pl.BlockSpec((None, tq, kw), lambda b, i: (b, i, 0)),
        pl.BlockSpec((None, seq, kw), lambda b, i: (b, 0, 0)),
        pl.BlockSpec((None, seq, vw), lambda b, i: (b, 0, 0)),
    ]
    args = [qfull.reshape(batch, seq, kw), kfull.reshape(batch, seq, kw), vv.reshape(batch, seq, vw)]
    if has_cache:
        p = kc.shape[1]
        in_specs += [
            pl.BlockSpec((None, p, kw), lambda b, i: (b, 0, 0)),
            pl.BlockSpec((None, p, vw), lambda b, i: (b, 0, 0)),
        ]
        args += [kc, vc]
    y = pl.pallas_call(
        functools.partial(_attn_kernel, has_cache=has_cache),
        grid=(batch, seq // tq),
        in_specs=in_specs,
        out_specs=pl.BlockSpec((None, tq, vw), lambda b, i: (b, i, 0)),
        out_shape=jax.ShapeDtypeStruct((batch, seq, vw), BF16),
        compiler_params=_cparams(("arbitrary", "arbitrary")),
        name="mla_attn",
    )(*args)
    return y.reshape(batch * seq, vw)


HALO = 16


def _conv3_rows(u, prev_row, next_row, cw_ref, seq):
    r, w = u.shape
    sub = min(seq, r)
    i8 = lax.broadcasted_iota(jnp.int32, (SUBLANES, w), 0)
    w0, w1, w2 = cw_ref[0:1, :], cw_ref[1:2, :], cw_ref[2:3, :]
    outs = []
    for s in range(r // sub):
        blk = u[s * sub:(s + 1) * sub, :]
        p = prev_row if sub == r else 0.0
        n = next_row if sub == r else 0.0
        up = pltpu.roll(blk, 1, axis=0)
        dn = pltpu.roll(blk, sub - 1, axis=0)
        up = jnp.concatenate([jnp.where(i8 == 0, p, up[:SUBLANES, :]), up[SUBLANES:, :]], axis=0)
        dn = jnp.concatenate([dn[:sub - SUBLANES, :], jnp.where(i8 == SUBLANES - 1, n, dn[sub - SUBLANES:, :])],
                             axis=0)
        outs.append(up * w0 + blk * w1 + dn * w2)
    return outs[0] if len(outs) == 1 else jnp.concatenate(outs, axis=0)


def _seq_edges(i, rows, seq):
    first = i * rows
    return first % seq == 0, (first + rows) % seq == 0


def _merge_kernel(x_ref, ya_ref, yc_ref, bb_ref, u_ref, up_ref, un_ref, sg_ref, mod_ref,
                  wa_ref, wb_ref, wc_ref, wout_ref, cw_ref, n2_ref, x1_ref, xn2_ref, *, seq):
    tm = x_ref.shape[0]
    at_start, at_end = _seq_edges(pl.program_id(0), tm, seq)
    prev_row = jnp.where(at_start, 0.0, up_ref[HALO - 1:HALO, :].astype(F32))
    next_row = jnp.where(at_end, 0.0, un_ref[0:1, :].astype(F32))
    conv = _conv3_rows(u_ref[...].astype(F32), prev_row, next_row, cw_ref, seq)
    yb = (bb_ref[...] * conv).astype(BF16)
    h = (sg_ref[:, 0:D_MODEL] * _dot(ya_ref[...], wa_ref[...])
         + sg_ref[:, D_MODEL:2 * D_MODEL] * _dot(yb, wb_ref[...])
         + sg_ref[:, 2 * D_MODEL:3 * D_MODEL] * _dot(yc_ref[...], wc_ref[...]))
    gate1 = mod_ref[2:3, :]
    x1 = x_ref[...] + gate1 * _dot(h.astype(BF16), wout_ref[...])
    x1_ref[...] = x1
    shift2 = mod_ref[3:4, :]
    scale2 = mod_ref[4:5, :]
    xn2_ref[...] = (_rms(x1, n2_ref[...]) * (1.0 + scale2) + shift2).astype(BF16)


def _merge_call(x2d, ya, yc, bb, u, sg, mod3, mod_row, wa, wb, wc, wout, cw, n2, seq):
    n = x2d.shape[0]
    tm = MERGE_ROWS
    nbh = n // HALO
    per = tm // HALO

    def row(width):
        return pl.BlockSpec((tm, width), lambda i: (i, 0))

    def cst(shape):
        return pl.BlockSpec(shape, lambda i: (0, 0))

    return pl.pallas_call(
        functools.partial(_merge_kernel, seq=seq),
        grid=(n // tm,),
        in_specs=[
            row(D_MODEL), row(A_W), row(C_HEADS * C_V), row(B_W), row(B_W),
            pl.BlockSpec((HALO, B_W), lambda i: (jnp.maximum(i * per - 1, 0), 0)),
            pl.BlockSpec((HALO, B_W), lambda i: (jnp.minimum((i + 1) * per, nbh - 1), 0)),
            row(N_BRANCH * D_MODEL),
            pl.BlockSpec((None, 6, D_MODEL), lambda i: (mod_row(i * tm), 0, 0)),
            cst(wa.shape), cst(wb.shape), cst(wc.shape), cst(wout.shape), cst(cw.shape), cst(n2.shape),
        ],
        out_specs=[row(D_MODEL), row(D_MODEL)],
        out_shape=[jax.ShapeDtypeStruct((n, D_MODEL), F32), jax.ShapeDtypeStruct((n, D_MODEL), BF16)],
        compiler_params=_cparams(("arbitrary",)),
        name="merge",
    )(x2d, ya, yc, bb, u, u, u, sg, mod3, wa, wb, wc, wout, cw, n2)


def _ffn_kernel(xn_ref, xp_ref, xx_ref, x1_ref, mod_ref, wup_ref, cw_ref, wd_ref,
                fn_ref, out_ref, h_scr, *, seq, final):
    r = xn_ref.shape[0]
    tn = FFN_COLS
    at_start, at_end = _seq_edges(pl.program_id(0), r, seq)
    xn = xn_ref[...]
    need_halo = seq > r

    def conv_part(off):
        w = wup_ref[:, off:off + tn]
        um = _dot(xn, w)
        prev_row = next_row = None
        if need_halo:
            prev_row = jnp.where(at_start, 0.0, _dot(xp_ref[...], w)[HALO - 1:HALO, :])
            next_row = jnp.where(at_end, 0.0, _dot(xx_ref[...], w)[0:1, :])
        return _conv3_rows(um, prev_row, next_row, cw_ref.at[:, off:off + tn], seq)

    for j in range(D_FF // tn):
        a = conv_part(j * tn)
        bval = conv_part(D_FF + j * tn)
        h_scr[:, j * tn:(j + 1) * tn] = (_silu(a) * bval).astype(BF16)

    gate2 = mod_ref[5:6, :]
    y = x1_ref[...] + gate2 * _dot(h_scr[...], wd_ref[...])
    if final:
        y = _rms(y, fn_ref[...])
    out_ref[...] = y


def _ffn_call(xn2, x1, mod3, mod_row, w_up, cw, w_down, fnorm, seq, final):
    n = xn2.shape[0]
    r = FFN_ROWS
    nbh = n // HALO
    per = r // HALO
    return pl.pallas_call(
        functools.partial(_ffn_kernel, seq=seq, final=final),
        grid=(n // r,),
        in_specs=[
            pl.BlockSpec((r, D_MODEL), lambda i: (i, 0)),
            pl.BlockSpec((HALO, D_MODEL), lambda i: (jnp.maximum(i * per - 1, 0), 0)),
            pl.BlockSpec((HALO, D_MODEL), lambda i: (jnp.minimum((i + 1) * per, nbh - 1), 0)),
            pl.BlockSpec((r, D_MODEL), lambda i: (i, 0)),
            pl.BlockSpec((None, 6, D_MODEL), lambda i: (mod_row(i * r), 0, 0)),
            _const_spec(w_up.shape),
            _const_spec(cw.shape),
            _const_spec(w_down.shape),
            _const_spec((1, D_MODEL)),
        ],
        out_specs=pl.BlockSpec((r, D_MODEL), lambda i: (i, 0)),
        out_shape=jax.ShapeDtypeStruct((n, D_MODEL), F32),
        scratch_shapes=[pltpu.VMEM((r, D_FF), BF16)],
        compiler_params=_cparams(("arbitrary",)),
        name="conv_mlp",
    )(xn2, xn2, xn2, x1, mod3, w_up, cw, w_down, fnorm)


def _rope_swap_perm():
    idx = np.arange(C_ROPE).reshape(2, 2, C_ROPE // 4)
    return idx[:, ::-1, :].reshape(-1)


def _pack_w_in(w_in):
    swap = _rope_swap_perm()
    kv = w_in[:, OFF_KV:OFF_KV + CACHE_W]
    kr = kv[:, C_KV_LORA:]
    kv_ext = jnp.concatenate(
        [kv, kr[:, swap], jnp.zeros((D_MODEL, KV_EXT - CACHE_W - C_ROPE), w_in.dtype)], axis=1)
    return jnp.concatenate(
        [w_in[:, :OFF_KV], kv_ext, w_in[:, OFF_KV + CACHE_W:]], axis=1).astype(BF16)


def _pack_w_q_up(w):
    swap = _rope_swap_perm()
    w3 = w.reshape(C_Q_LORA, C_HEADS, C_NOPE + C_ROPE)
    rope = w3[:, :, C_NOPE:]
    return jnp.concatenate([w3, rope[:, :, swap]], axis=2).reshape(C_Q_LORA, C_HEADS * LANES).astype(BF16)


def _pack_w_kv_up(w):
    w3 = w.reshape(C_KV_LORA, C_HEADS, C_NOPE + C_V)
    kn = jnp.concatenate([w3[:, :, :C_NOPE], jnp.zeros((C_KV_LORA, C_HEADS, LANES - C_NOPE), w.dtype)], axis=2)
    return jnp.concatenate(
        [kn.reshape(C_KV_LORA, C_HEADS * LANES), w3[:, :, C_NOPE:].reshape(C_KV_LORA, C_HEADS * C_V)],
        axis=1).astype(BF16)


def _pad_rows(a, rows):
    return jnp.concatenate([a, jnp.zeros((rows - a.shape[0],) + a.shape[1:], a.dtype)], axis=0)


def _rope_tables(seq, rotate):
    scale = (C_NOPE + C_ROPE) ** -0.5
    n_freq = C_ROPE // 4
    if rotate:
        rows = seq // GRID_W
        r = jnp.repeat(jnp.arange(rows, dtype=F32), GRID_W)
        col = jnp.tile(jnp.arange(GRID_W, dtype=F32), rows)
        freq = ROPE_THETA ** (-jnp.arange(n_freq, dtype=F32) / n_freq)
        ang = jnp.stack([r[:, None] * freq, col[:, None] * freq], axis=1)
        cos, sin = jnp.cos(ang), jnp.sin(ang)
    else:
        cos = jnp.ones((seq, 2, n_freq), F32)
        sin = jnp.zeros((seq, 2, n_freq), F32)
    cos_full = jnp.stack([cos, cos], axis=2).reshape(seq, C_ROPE)
    sin_signed = jnp.stack([-sin, sin], axis=2).reshape(seq, C_ROPE)
    tabq = scale * jnp.concatenate([jnp.ones((seq, C_NOPE), F32), cos_full, sin_signed], axis=1)
    tabk = jnp.concatenate([cos_full, sin_signed, jnp.zeros((seq, LANES - 2 * C_ROPE), F32)], axis=1)
    return tabq, tabk


def _layer(x2d, batch, seq, mod3, mod_row, lw, lb, s_init, cache_pad, tabs, want_ctx_outputs, final, fnorm):
    (n1, w_ext, gnorm, wa, cw, wb, qn, wq, kvn, wkv, wc, wout, n2, w_up, fcw, w_down) = lw
    tabq, tabk = tabs
    res = _in_call(x2d, mod3, mod_row, n1, w_ext, lb, qn, wq, kvn, wkv, tabq, tabk, seq, want_ctx_outputs)
    q, gf, gb, kf, kb, v, ga, bb, u, qfull, kfull, vv, sg = res[:13]
    own_cache = res[13].reshape(batch, seq, CACHE_W) if want_ctx_outputs else None
    ya, states = _hgrn_call(q, gf, gb, kf, kb, v, ga, gnorm, s_init, batch, seq, want_ctx_outputs)
    kc = vc = None
    if cache_pad is not None:
        kc, vc = _cache_call(cache_pad, wkv)
    yc = _attn_call(qfull, kfull, vv, kc, vc, batch, seq)
    x1, xn2 = _merge_call(x2d, ya, yc, bb, u, sg, mod3, mod_row, wa, wb, wc, wout, cw, n2, seq)
    x2 = _ffn_call(xn2, x1, mod3, mod_row, w_up, fcw, w_down, fnorm, seq, final)
    return x2, states, own_cache


def kernel(x_prompt, x_sample, state_hgrn, cache_mla, c, c_ctx, w_ada, b_ada, norm1, w_in, hgrn_lb_logits, hgrn_gnorm, w_o_hgrn, conv_w, w_o_conv, mla_q_norm, w_q_up, mla_kv_norm, w_kv_up, w_o_mla, w_out, norm2, w_up, ffn_conv_w, w_down, final_norm):
    depth = w_in.shape[0]
    bp, sp, _ = x_prompt.shape
    bs, ss, _ = x_sample.shape
    assert 1 + bs <= MOD_ROWS and sp % IN_ROWS == 0 and ss % IN_ROWS == 0
    assert (bp * sp) % FFN_ROWS == 0 and (bs * ss) % FFN_ROWS == 0 and FFN_ROWS % sp == 0 and ss % FFN_ROWS == 0

    cvec = _pad_rows(jnp.concatenate([c_ctx[None, :], c], axis=0), MOD_ROWS)
    mod = _ada_call(cvec, w_ada, b_ada)
    lb_all = _lb_call(hgrn_lb_logits)
    tabs_ctx = _rope_tables(IN_ROWS, rotate=False)
    tabs_lat = _rope_tables(ss, rotate=True)
    fnorm = final_norm.reshape(1, D_MODEL)
    cache_pad = jnp.concatenate(
        [cache_mla, jnp.zeros(cache_mla.shape[:-1] + (KV_EXT - CACHE_W,), cache_mla.dtype)], axis=-1)

    xp = x_prompt.reshape(bp * sp, D_MODEL)
    xs = x_sample.reshape(bs * ss, D_MODEL)
    new_states, new_caches = [], []
    for l in range(depth):
        lw = (norm1[l].reshape(1, -1), _pack_w_in(w_in[l]), hgrn_gnorm[l].reshape(1, -1),
              w_o_hgrn[l].astype(BF16), _pad_rows(conv_w[l], SUBLANES), w_o_conv[l].astype(BF16),
              mla_q_norm[l].reshape(1, -1), _pack_w_q_up(w_q_up[l]), mla_kv_norm[l].reshape(1, -1),
              _pack_w_kv_up(w_kv_up[l]), w_o_mla[l].astype(BF16), w_out[l].astype(BF16),
              norm2[l].reshape(1, -1), w_up[l].astype(BF16), _pad_rows(ffn_conv_w[l], SUBLANES),
              w_down[l].astype(BF16))
        mod3 = mod[l].reshape(MOD_ROWS, 6, D_MODEL)
        final = l == depth - 1
        xp, st, kvc = _layer(xp, bp, sp, mod3, lambda r: 0, lw, lb_all[l], None, None, tabs_ctx,
                             True, final, fnorm)
        new_states.append(st)
        new_caches.append(kvc)
        xs, _, _ = _layer(xs, bs, ss, mod3, lambda r: 1 + r // ss, lw, lb_all[l], state_hgrn[:, l],
                          cache_pad[:, l], tabs_lat, False, final, fnorm)
    y_prompt = xp.reshape(bp, sp, D_MODEL)
    y_sample = xs.reshape(bs, ss, D_MODEL)
    return (y_prompt, y_sample, jnp.stack(new_states, axis=1), jnp.stack(new_caches, axis=1))
```

```python
import functools

import numpy as np
import jax
import jax.numpy as jnp
from jax import lax
from jax.experimental import pallas as pl
from jax.experimental.pallas import tpu as pltpu

F32 = jnp.float32
BF16 = jnp.bfloat16

D_MODEL = 1024
GRID_W = 64
EPS = 1e-6
A_HEADS = 4
A_DK = 128
A_DV = 128
A_W = A_HEADS * A_DK
LB_FLOOR = 1e-30
B_W = 512
C_HEADS = 8
C_NOPE = 64
C_ROPE = 32
C_V = 64
C_Q_LORA = 384
C_KV_LORA = 256
ROPE_THETA = 10000.0
D_FF = 2816
N_BRANCH = 3
CACHE_W = C_KV_LORA + C_ROPE
LOG2E = 1.4426950408889634

LANES = 128
SUBLANES = 8
HALO = 16
VMEM_LIMIT_BYTES = 56 * 1024 * 1024

IN_ROWS = 512
HGRN_CHUNK = 128
HGRN_HEADS = 2
ATTN_Q_ROWS = 256
MERGE_ROWS = 512
FFN_ROWS = 1024
FFN_COLS = 256
MOD_ROWS = 16
ADA_COLS = 1536

OFF_CQ = 8 * A_W
OFF_KV = OFF_CQ + C_Q_LORA
KV_EXT = 3 * LANES
OFF_GATE = OFF_KV + KV_EXT
IN_COLS_EXT = OFF_GATE + N_BRANCH * D_MODEL


def _cparams(sem):
    return pltpu.CompilerParams(dimension_semantics=sem, vmem_limit_bytes=VMEM_LIMIT_BYTES)


def _dot(a, b):
    return jnp.dot(a, b, preferred_element_type=F32)


def _dot_nt(a, b):
    return lax.dot_general(a, b, (((1,), (1,)), ((), ())), preferred_element_type=F32)


def _dot_tn(a, b):
    return lax.dot_general(a, b, (((0,), (0,)), ((), ())), preferred_element_type=F32)


def _rms(x, g):
    ms = jnp.mean(x * x, axis=-1, keepdims=True)
    return x * lax.rsqrt(ms + EPS) * g


def _sigmoid_pair(z):
    t = jnp.exp(-jnp.abs(z))
    r = 1.0 / (1.0 + t)
    big, small = r, t * r
    pos = z >= 0
    return jnp.where(pos, big, small), jnp.where(pos, small, big)


def _sigmoid(z):
    return 1.0 / (1.0 + jnp.exp(-z))


def _silu(z):
    return z * _sigmoid(z)


def _layer_spec(shape, layer, grid_rank, single=True):
    nd = len(shape)
    if grid_rank == 1:
        imap = lambda i: (layer,) + (0,) * nd
    else:
        imap = lambda i, j: (layer,) + (0,) * nd
    if single:
        return pl.BlockSpec((None,) + tuple(shape), imap, pipeline_mode=pl.Buffered(1))
    return pl.BlockSpec((None,) + tuple(shape), imap)


def _mod_spec(layer, mod_row, rows, grid_rank=1):
    if grid_rank == 1:
        return pl.BlockSpec((None, None, 6, D_MODEL), lambda i: (layer, mod_row(i * rows), 0, 0))
    return pl.BlockSpec((None, None, 6, D_MODEL), lambda i, j: (layer, mod_row(i * rows), 0, 0))


def _ada_kernel(c_ref, w_ref, b_ref, o_ref):
    s = _silu(c_ref[...])
    o_ref[...] = jnp.dot(s, w_ref[...], preferred_element_type=F32,
                         precision=lax.Precision.HIGHEST) + b_ref[...]


def _ada_call(cvec, w_ada, b_ada):
    depth = w_ada.shape[0]
    ncol = w_ada.shape[2] // ADA_COLS
    return pl.pallas_call(
        _ada_kernel,
        grid=(depth, ncol),
        in_specs=[
            pl.BlockSpec((MOD_ROWS, D_MODEL), lambda l, j: (0, 0)),
            pl.BlockSpec((None, D_MODEL, ADA_COLS), lambda l, j: (l, 0, j)),
            pl.BlockSpec((None, 1, ADA_COLS), lambda l, j: (l, 0, j)),
        ],
        out_specs=pl.BlockSpec((None, MOD_ROWS, ADA_COLS), lambda l, j: (l, 0, j)),
        out_shape=jax.ShapeDtypeStruct((depth, MOD_ROWS, w_ada.shape[2]), F32),
        compiler_params=_cparams(("arbitrary", "arbitrary")),
        name="ada_mod",
    )(cvec, w_ada, b_ada.reshape(depth, 1, -1))


def _lb_kernel(x_ref, o_ref):
    depth = x_ref.shape[0]
    xs = [x_ref[l] for l in range(depth)]
    m = xs[0]
    for l in range(1, depth):
        m = jnp.maximum(m, xs[l])
    es = [jnp.exp(x - m) for x in xs]
    tot = es[0]
    for l in range(1, depth):
        tot = tot + es[l]
    ps = [e / tot for e in es]
    run = jnp.zeros_like(ps[0])
    for l in range(depth):
        run = run + ps[l]
        o_ref[l] = run - ps[0]


def _lb_call(logits):
    return pl.pallas_call(
        _lb_kernel,
        out_shape=jax.ShapeDtypeStruct(logits.shape, F32),
        name="hgrn_lb",
    )(logits)


def _place_rope_key(z3, tabk):
    tmp = z3 * tabk
    rot = tmp + pltpu.roll(tmp, LANES - C_ROPE, axis=1)
    lane = lax.broadcasted_iota(jnp.int32, rot.shape, 1)
    rot = jnp.where(lane < C_ROPE, rot, 0.0)
    return pltpu.roll(rot, C_NOPE, axis=1) + pltpu.roll(rot, C_NOPE + C_ROPE, axis=1)


def _in_kernel(x_ref, mod_ref, n1_ref, w_ref, lb_ref, qn_ref, wq_ref, kvn_ref, wkv_ref,
               tabq_ref, tabk_ref,
               q_ref, gf_ref, gb_ref, kf_ref, kb_ref, v_ref, ga_ref, bb_ref, u_ref,
               qfull_ref, kfull_ref, vv_ref, sg_ref, *cache_ref):
    x = x_ref[...]
    shift1 = mod_ref[0:1, :]
    scale1 = mod_ref[1:2, :]
    xn = _rms(x, n1_ref[...]) * (1.0 + scale1) + shift1
    xb = xn.astype(BF16)

    def proj(off, width):
        return _dot(xb, w_ref[:, off:off + width])

    q_ref[...] = (_silu(proj(0, A_W)) * (A_DK ** -0.5)).astype(BF16)
    for d, (g_ref, k_ref) in enumerate(((gf_ref, kf_ref), (gb_ref, kb_ref))):
        z = proj((1 + d) * A_W, A_W)
        lb = lb_ref[d:d + 1, :]
        sp, sn = _sigmoid_pair(z)
        g_ref[...] = jnp.log(jnp.maximum(lb, LB_FLOOR) + (1.0 - lb) * sp) * LOG2E
        k_ref[...] = ((1.0 - lb) * sn - jnp.maximum(LB_FLOOR - lb, 0.0)).astype(BF16)
    v_ref[...] = proj(3 * A_W, A_W).astype(BF16)
    ga_ref[...] = _silu(proj(4 * A_W, A_W)).astype(BF16)
    bb_ref[...] = proj(5 * A_W, B_W).astype(BF16)
    u_ref[...] = (proj(6 * A_W, B_W) * proj(7 * A_W, B_W)).astype(BF16)
    cqn = _rms(proj(OFF_CQ, C_Q_LORA), qn_ref[...]).astype(BF16)
    tabq = tabq_ref[...]
    for h in range(C_HEADS):
        qz = _dot(cqn, wq_ref[:, h * LANES:(h + 1) * LANES])
        qfull_ref[:, h * LANES:(h + 1) * LANES] = (qz * tabq).astype(BF16)
    zkv = proj(OFF_KV, KV_EXT)
    ckvn = _rms(zkv[:, :C_KV_LORA], kvn_ref[...])
    z3 = zkv[:, C_KV_LORA:]
    if cache_ref:
        cache_ref[0][:, :C_KV_LORA] = ckvn
        cache_ref[0][:, C_KV_LORA:] = z3[:, :C_ROPE]
    krr = _place_rope_key(z3, tabk_ref[...])
    cb = ckvn.astype(BF16)
    for h in range(C_HEADS):
        kz = _dot(cb, wkv_ref[:, h * LANES:(h + 1) * LANES])
        kfull_ref[:, h * LANES:(h + 1) * LANES] = (kz + krr).astype(BF16)
    vv_ref[...] = _dot(cb, wkv_ref[:, C_HEADS * LANES:]).astype(BF16)
    for j in range(N_BRANCH * D_MODEL // A_W):
        sg_ref[:, j * A_W:(j + 1) * A_W] = _sigmoid(proj(OFF_GATE + j * A_W, A_W)).astype(BF16)


def _in_call(x2d, layer, mod4, mod_row, pw, lb_all, tabq, tabk, want_cache):
    n = x2d.shape[0]
    tm = IN_ROWS
    tab_blocks = tabq.shape[0] // tm

    def row(width):
        return pl.BlockSpec((tm, width), lambda i: (i, 0))

    outs = [
        (A_W, BF16),
        (A_W, F32), (A_W, F32),
        (A_W, BF16), (A_W, BF16),
        (A_W, BF16),
        (A_W, BF16),
        (B_W, BF16),
        (B_W, BF16),
        (C_HEADS * LANES, BF16),
        (C_HEADS * LANES, BF16),
        (C_HEADS * C_V, BF16),
        (N_BRANCH * D_MODEL, BF16),
    ]
    if want_cache:
        outs.append((CACHE_W, F32))
    return pl.pallas_call(
        _in_kernel,
        grid=(n // tm,),
        in_specs=[
            row(D_MODEL),
            _mod_spec(layer, mod_row, tm),
            _layer_spec((1, D_MODEL), layer, 1),
            _layer_spec((D_MODEL, IN_COLS_EXT), layer, 1),
            _layer_spec((2, A_W), layer, 1),
            _layer_spec((1, C_Q_LORA), layer, 1),
            _layer_spec((C_Q_LORA, C_HEADS * LANES), layer, 1),
            _layer_spec((1, C_KV_LORA), layer, 1),
            _layer_spec((C_KV_LORA, C_HEADS * LANES + C_HEADS * C_V), layer, 1),
            pl.BlockSpec((tm, LANES), lambda i: (i % tab_blocks, 0)),
            pl.BlockSpec((tm, LANES), lambda i: (i % tab_blocks, 0)),
        ],
        out_specs=[row(w) for w, _ in outs],
        out_shape=[jax.ShapeDtypeStruct((n, w), dt) for w, dt in outs],
        compiler_params=_cparams(("arbitrary",)),
        name="in_proj",
    )(x2d, mod4, pw["norm1"], pw["w_in"], lb_all, pw["q_norm"], pw["w_q_up"], pw["kv_norm"], pw["w_kv_up"],
      tabq, tabk)


def _cache_kernel(c_ref, wkv_ref, kfull_ref, vv_ref):
    c = c_ref[...]
    cb = c[:, :C_KV_LORA].astype(BF16)
    z3 = c[:, C_KV_LORA:]
    lane = lax.broadcasted_iota(jnp.int32, z3.shape, 1)
    kr = jnp.where(lane < C_ROPE, z3, 0.0)
    krr = pltpu.roll(kr, C_NOPE, axis=1) + pltpu.roll(kr, C_NOPE + C_ROPE, axis=1)
    for h in range(C_HEADS):
        kz = _dot(cb, wkv_ref[:, h * LANES:(h + 1) * LANES])
        kfull_ref[:, h * LANES:(h + 1) * LANES] = (kz + krr).astype(BF16)
    vv_ref[...] = _dot(cb, wkv_ref[:, C_HEADS * LANES:]).astype(BF16)


def _cache_call(cache_pad, layer, wkv):
    b, _, p, w = cache_pad.shape
    return pl.pallas_call(
        _cache_kernel,
        grid=(b,),
        in_specs=[
            pl.BlockSpec((None, None, p, w), lambda i: (i, layer, 0, 0)),
            _layer_spec(wkv.shape[1:], layer, 1, single=False),
        ],
        out_specs=[
            pl.BlockSpec((None, p, C_HEADS * LANES), lambda i: (i, 0, 0)),
            pl.BlockSpec((None, p, C_HEADS * C_V), lambda i: (i, 0, 0)),
        ],
        out_shape=[
            jax.ShapeDtypeStruct((b, p, C_HEADS * LANES), BF16),
            jax.ShapeDtypeStruct((b, p, C_HEADS * C_V), BF16),
        ],
        compiler_params=_cparams(("arbitrary",)),
        name="cache_expand",
    )(cache_pad, wkv)


def _level_matrix(c, rev):
    t = np.arange(c)[:, None]
    s = np.arange(c)[None, :]
    lev = np.floor(np.log2(np.maximum(np.bitwise_xor(t, s), 1))).astype(np.int32)
    lev = np.where(s == t, -1, lev)
    lev = np.where((s < t) if rev else (s > t), -2, lev)
    return lev.astype(np.int32)


def _prefix_matrix(c, rev):
    t = np.arange(c)[:, None]
    s = np.arange(c)[None, :]
    incl = ((s >= t) if rev else (s <= t)).astype(np.float32)
    return np.concatenate([incl, incl], axis=1)


def _level_exponent(b, g, h, rev):
    c, dk = b.shape
    ref = h if rev else h - 1
    if 2 * h > SUBLANES:
        parts = []
        for gs in range(0, c, 2 * h):
            beta = jnp.broadcast_to(b[gs + ref:gs + ref + 1, :], (h, dk))
            lo = b[gs:gs + h, :]
            hi = b[gs + h:gs + 2 * h, :]
            parts += [lo - beta, beta - hi] if rev else [beta - lo, hi - beta]
        return jnp.concatenate(parts, axis=0)
    row = lax.broadcasted_iota(jnp.int32, (SUBLANES, dk), 0)
    upper = (row % (2 * h)) >= h
    keep = jnp.logical_not(upper) if rev else upper
    if h == 1:
        parts = [jnp.where(keep, g[j:j + SUBLANES, :], 0.0) for j in range(0, c, SUBLANES)]
        return jnp.concatenate(parts, axis=0)
    sign = jnp.where(keep, 1.0, -1.0)
    parts = []
    for j in range(0, c, SUBLANES):
        bj = b[j:j + SUBLANES, :]
        beta = jnp.broadcast_to(bj[ref:ref + 1, :], (SUBLANES, dk))
        for gs in range(2 * h, SUBLANES, 2 * h):
            beta = jnp.where(row >= gs, jnp.broadcast_to(bj[gs + ref:gs + ref + 1, :], (SUBLANES, dk)), beta)
        parts.append((bj - beta) * sign)
    return jnp.concatenate(parts, axis=0)


def _hgrn_chunks(streams, lev, pmat):
    c = streams[0][0].shape[0]
    bs = []
    for q, k, g, v, st, rev in streams:
        g_hi = g.astype(BF16)
        g_lo = (g - g_hi.astype(F32)).astype(BF16)
        bs.append(_dot(pmat[rev], jnp.concatenate([g_hi, g_lo], axis=0)))
    a_s = [jnp.zeros((c, c), F32) for _ in streams]
    for hl in range(int(np.log2(c))):
        ps = []
        for (q, k, g, v, st, rev), b in zip(streams, bs):
            xb = jnp.exp2(_level_exponent(b, g, 1 << hl, rev)).astype(BF16)
            ps.append(_dot_nt(q * xb, k * xb))
        a_s = [jnp.where(lev[s[5]] == hl, p, a) for s, p, a in zip(streams, ps, a_s)]
    ps = [_dot_nt(s[0], s[1]) for s in streams]
    a_s = [jnp.where(lev[s[5]] == -1, p, a) for s, p, a in zip(streams, ps, a_s)]
    outs = []
    for (q, k, g, v, st, rev), b, a in zip(streams, bs, a_s):
        outs.append(_dot(a.astype(BF16), v) + _dot_nt(q * jnp.exp2(b).astype(BF16), st.astype(BF16)))
    new_states = []
    for (q, k, g, v, st, rev), b in zip(streams, bs):
        tot = b[0:1, :] if rev else b[c - 1:c, :]
        ks = (k.astype(F32) * jnp.exp2(tot - b)).astype(BF16)
        new_states.append(st * jnp.exp2(tot) + _dot_tn(v, ks))
    return outs, new_states


def _hgrn_kernel(*refs, has_init, want_state):
    refs = list(refs)
    (q_ref, gf_ref, gb_ref, kf_ref, kb_ref, v_ref, ga_ref, gn_ref,
     levf_ref, levb_ref, pmf_ref, pmb_ref) = refs[:12]
    pos = 12
    s0_ref = None
    if has_init:
        s0_ref = refs[pos]
        pos += 1
    y_ref = refs[pos]
    pos += 1
    sfin_ref = None
    if want_state:
        sfin_ref = refs[pos]
        pos += 1
    o_scr, st_scr = refs[pos], refs[pos + 1]

    t = q_ref.shape[0]
    c = HGRN_CHUNK
    n = t // c
    hp = HGRN_HEADS
    for d in range(2):
        for h in range(hp):
            st_scr[d, h] = s0_ref[d, h].T if has_init else jnp.zeros((A_DV, A_DK), F32)
    lev = (levf_ref, levb_ref)
    pmat = (pmf_ref, pmb_ref)
    g_refs = (gf_ref, gb_ref)
    k_refs = (kf_ref, kb_ref)

    def body(i, carry):
        streams = []
        rows = []
        for d in range(2):
            ci = i if d == 0 else n - 1 - i
            sl = pl.ds(pl.multiple_of(ci * c, c), c)
            rows.append(sl)
            for h in range(hp):
                cs = slice(h * A_DK, (h + 1) * A_DK)
                streams.append((q_ref[sl, cs], k_refs[d][sl, cs], g_refs[d][sl, cs], v_ref[sl, cs],
                                st_scr[d, h], d))
        outs, new_states = _hgrn_chunks(streams, (lev[0][...], lev[1][...]), (pmat[0][...], pmat[1][...]))
        for idx, (o, st_new) in enumerate(zip(outs, new_states)):
            d, h = divmod(idx, hp)
            st_scr[d, h] = st_new
            o_scr[d, rows[d], h * A_DV:(h + 1) * A_DV] = o
        return carry

    lax.fori_loop(0, n, body, 0)
    for h in range(hp):
        cs = slice(h * A_DV, (h + 1) * A_DV)
        if want_state:
            for d in range(2):
                sfin_ref[d, h] = st_scr[d, h].T
        o = o_scr[0, :, cs] + o_scr[1, :, cs]
        y_ref[:, cs] = (_rms(o, gn_ref[...]) * ga_ref[:, cs]).astype(BF16)


def _hgrn_call(q, gf, gb, kf, kb, v, ga, layer, gnorm, s_init, batch, seq, want_state):
    c = HGRN_CHUNK
    hp = HGRN_HEADS
    consts = [jnp.asarray(_level_matrix(c, False)), jnp.asarray(_level_matrix(c, True)),
              jnp.asarray(_prefix_matrix(c, False), BF16), jnp.asarray(_prefix_matrix(c, True), BF16)]

    def r3(a):
        return a.reshape(batch, seq, A_W)

    col = pl.BlockSpec((None, seq, hp * A_DK), lambda b, h: (b, 0, h))
    in_specs = [col] * 7 + [_layer_spec((1, A_DV), layer, 2, single=False)]
    in_specs += [pl.BlockSpec(m.shape, lambda b, h: (0, 0)) for m in consts]
    args = [r3(q), r3(gf), r3(gb), r3(kf), r3(kb), r3(v), r3(ga), gnorm] + consts
    has_init = s_init is not None
    if has_init:
        in_specs.append(pl.BlockSpec((None, None, 2, hp, A_DK, A_DV), lambda b, h: (b, layer, 0, h, 0, 0)))
        args.append(s_init)
    out_specs = [col]
    out_shape = [jax.ShapeDtypeStruct((batch, seq, A_W), BF16)]
    if want_state:
        out_specs.append(pl.BlockSpec((None, 2, hp, A_DK, A_DV), lambda b, h: (b, 0, h, 0, 0)))
        out_shape.append(jax.ShapeDtypeStruct((batch, 2, A_HEADS, A_DK, A_DV), F32))
    res = pl.pallas_call(
        functools.partial(_hgrn_kernel, has_init=has_init, want_state=want_state),
        grid=(batch, A_HEADS // hp),
        in_specs=in_specs,
        out_specs=out_specs,
        out_shape=out_shape,
        scratch_shapes=[pltpu.VMEM((2, seq, hp * A_DV), F32), pltpu.VMEM((2, hp, A_DV, A_DK), F32)],
        compiler_params=_cparams(("arbitrary", "arbitrary")),
        name="hgrn",
    )(*args)
    y = res[0].reshape(batch * seq, A_W)
    return y, (res[1] if want_state else None)


def _attn_kernel(*refs, has_cache):
    if has_cache:
        q_ref, k_ref, v_ref, kc_ref, vc_ref, y_ref = refs
    else:
        q_ref, k_ref, v_ref, y_ref = refs
    lane = lax.broadcasted_iota(jnp.int32, (q_ref.shape[0], LANES), 1)

    def scores(h):
        hs = slice(h * LANES, (h + 1) * LANES)
        qh = q_ref[:, hs]
        s1 = _dot_nt(qh, k_ref[:, hs])
        s2 = _dot_nt(qh, kc_ref[:, hs]) if has_cache else None
        return s1, s2

    def attend(h, s1, s2):
        vs = slice((h // 2) * LANES, (h // 2 + 1) * LANES)
        m = jnp.max(s1, axis=-1, keepdims=True)
        if has_cache:
            m = jnp.maximum(m, jnp.max(s2, axis=-1, keepdims=True))
        p1 = jnp.exp2(s1 - m)
        l = jnp.sum(p1, axis=-1, keepdims=True)
        o = _dot(p1.astype(BF16), v_ref[:, vs])
        if has_cache:
            p2 = jnp.exp2(s2 - m)
            l = l + jnp.sum(p2, axis=-1, keepdims=True)
            o = o + _dot(p2.astype(BF16), vc_ref[:, vs])
        return o / l

    pending = scores(0)
    outs = []
    for h in range(C_HEADS):
        nxt = scores(h + 1) if h + 1 < C_HEADS else None
        outs.append(attend(h, *pending))
        pending = nxt
        if h % 2 == 1:
            vs = slice((h // 2) * LANES, (h // 2 + 1) * LANES)
            y_ref[:, vs] = jnp.where(lane < C_V, outs[h - 1], outs[h]).astype(BF16)


def _attn_call(qfull, kfull, vv, kc, vc, batch, seq):
    tq = ATTN_Q_ROWS
    has_cache = kc is not None
    kw = C_HEADS * LANES
    vw = C_HEADS * C_V
    in_specs = [
        pl.BlockSpec((None, tq, kw), lambda b, i: (b, i, 0)),
        pl.BlockSpec((None, seq, kw), lambda b, i: (b, 0, 0)),
        pl.BlockSpec((None, seq, vw), lambda b, i: (b, 0, 0)),
    ]
    args = [qfull.reshape(batch, seq, kw), kfull.reshape(batch, seq, kw), vv.reshape(batch, seq, vw)]
    if has_cache:
        p = kc.shape[1]
        in_specs += [
            pl.BlockSpec((None, p, kw), lambda b, i: (b, 0, 0)),
            pl.BlockSpec((None, p, vw), lambda b, i: (b, 0, 0)),
        ]
        args += [kc, vc]
    y = pl.pallas_call(
        functools.partial(_attn_kernel, has_cache=has_cache),
        grid=(batch, seq // tq),
        in_specs=in_specs,
        out_specs=pl.BlockSpec((None, tq, vw), lambda b, i: (b, i, 0)),
        out_shape=jax.ShapeDtypeStruct((batch, seq, vw), BF16),
        compiler_params=_cparams(("arbitrary", "arbitrary")),
        name="mla_attn",
    )(*args)
    return y.reshape(batch * seq, vw)


def _conv3_rows(u, prev_row, next_row, cw_ref, seq):
    r, w = u.shape
    sub = min(seq, r)
    whole = seq <= r
    i8 = lax.broadcasted_iota(jnp.int32, (SUBLANES, w), 0)
    w0, w1, w2 = cw_ref[0:1, :], cw_ref[1:2, :], cw_ref[2:3, :]
    outs = []
    for s in range(r // sub):
        blk = u[s * sub:(s + 1) * sub, :]
        p = 0.0 if whole else prev_row
        n = 0.0 if whole else next_row
        up = pltpu.roll(blk, 1, axis=0)
        dn = pltpu.roll(blk, sub - 1, axis=0)
        up = jnp.concatenate([jnp.where(i8 == 0, p, up[:SUBLANES, :]), up[SUBLANES:, :]], axis=0)
        dn = jnp.concatenate([dn[:sub - SUBLANES, :], jnp.where(i8 == SUBLANES - 1, n, dn[sub - SUBLANES:, :])],
                             axis=0)
        outs.append(up * w0 + blk * w1 + dn * w2)
    return outs[0] if len(outs) == 1 else jnp.concatenate(outs, axis=0)


def _seq_edges(i, rows, seq):
    first = i * rows
    return first % seq == 0, (first + rows) % seq == 0


def _merge_kernel(x_ref, ya_ref, yc_ref, bb_ref, u_ref, up_ref, un_ref, sg_ref, mod_ref,
                  wa_ref, wb_ref, wc_ref, wout_ref, cw_ref, n2_ref, x1_ref, xn2_ref, *, seq):
    tm = x_ref.shape[0]
    at_start, at_end = _seq_edges(pl.program_id(0), tm, seq)
    prev_row = jnp.where(at_start, 0.0, up_ref[HALO - 1:HALO, :].astype(F32))
    next_row = jnp.where(at_end, 0.0, un_ref[0:1, :].astype(F32))
    conv = _conv3_rows(u_ref[...].astype(F32), prev_row, next_row, cw_ref, seq)
    yb = (bb_ref[...] * conv).astype(BF16)
    half = tm // 2
    hs = []
    for s in range(2):
        rs = slice(s * half, (s + 1) * half)
        pa = _dot(ya_ref[rs, :], wa_ref[...])
        pc = _dot(yc_ref[rs, :], wc_ref[...])
        pb = _dot(yb[rs, :], wb_ref[...])
        hs.append((sg_ref[rs, 0:D_MODEL] * pa + sg_ref[rs, D_MODEL:2 * D_MODEL] * pb
                   + sg_ref[rs, 2 * D_MODEL:3 * D_MODEL] * pc).astype(BF16))
    gate1 = mod_ref[2:3, :]
    shift2 = mod_ref[3:4, :]
    scale2 = mod_ref[4:5, :]
    for s in range(2):
        rs = slice(s * half, (s + 1) * half)
        x1 = x_ref[rs, :] + gate1 * _dot(hs[s], wout_ref[...])
        x1_ref[rs, :] = x1
        xn2_ref[rs, :] = (_rms(x1, n2_ref[...]) * (1.0 + scale2) + shift2).astype(BF16)


def _merge_call(x2d, ya, yc, bb, u, sg, layer, mod4, mod_row, pw, seq):
    n = x2d.shape[0]
    tm = MERGE_ROWS
    nbh = n // HALO
    per = tm // HALO

    def row(width):
        return pl.BlockSpec((tm, width), lambda i: (i, 0))

    def wspec(name):
        return _layer_spec(pw[name].shape[1:], layer, 1)

    return pl.pallas_call(
        functools.partial(_merge_kernel, seq=seq),
        grid=(n // tm,),
        in_specs=[
            row(D_MODEL), row(A_W), row(C_HEADS * C_V), row(B_W), row(B_W),
            pl.BlockSpec((HALO, B_W), lambda i: (jnp.maximum(i * per - 1, 0), 0)),
            pl.BlockSpec((HALO, B_W), lambda i: (jnp.minimum((i + 1) * per, nbh - 1), 0)),
            row(N_BRANCH * D_MODEL),
            _mod_spec(layer, mod_row, tm),
            wspec("w_o_hgrn"), wspec("w_o_conv"), wspec("w_o_mla"), wspec("w_out"), wspec("conv_w"),
            wspec("norm2"),
        ],
        out_specs=[row(D_MODEL), row(D_MODEL)],
        out_shape=[jax.ShapeDtypeStruct((n, D_MODEL), F32), jax.ShapeDtypeStruct((n, D_MODEL), BF16)],
        compiler_params=_cparams(("arbitrary",)),
        name="merge",
    )(x2d, ya, yc, bb, u, u, u, sg, mod4, pw["w_o_hgrn"], pw["w_o_conv"], pw["w_o_mla"], pw["w_out"],
      pw["conv_w"], pw["norm2"])


def _ffn_kernel(xn_ref, xp_ref, xx_ref, x1_ref, mod_ref, wup_ref, cw_ref, wd_ref,
                fn_ref, out_ref, h_scr, *, seq, final):
    r = xn_ref.shape[0]
    tn = FFN_COLS
    at_start, at_end = _seq_edges(pl.program_id(0), r, seq)
    xn = xn_ref[...]
    need_halo = seq > r

    def conv_part(off):
        w = wup_ref[:, off:off + tn]
        um = _dot(xn, w)
        prev_row = next_row = None
        if need_halo:
            prev_row = jnp.where(at_start, 0.0, _dot(xp_ref[...], w)[HALO - 1:HALO, :])
            next_row = jnp.where(at_end, 0.0, _dot(xx_ref[...], w)[0:1, :])
        return _conv3_rows(um, prev_row, next_row, cw_ref.at[:, off:off + tn], seq)

    for j in range(D_FF // tn):
        a = conv_part(j * tn)
        bval = conv_part(D_FF + j * tn)
        h_scr[:, j * tn:(j + 1) * tn] = (_silu(a) * bval).astype(BF16)

    gate2 = mod_ref[5:6, :]
    y = x1_ref[...] + gate2 * _dot(h_scr[...], wd_ref[...])
    if final:
        y = _rms(y, fn_ref[...])
    out_ref[...] = y


def _ffn_call(xn2, x1, layer, mod4, mod_row, pw, fnorm, seq, final):
    n = xn2.shape[0]
    r = FFN_ROWS
    nbh = n // HALO
    per = r // HALO
    return pl.pallas_call(
        functools.partial(_ffn_kernel, seq=seq, final=final),
        grid=(n // r,),
        in_specs=[
            pl.BlockSpec((r, D_MODEL), lambda i: (i, 0)),
            pl.BlockSpec((HALO, D_MODEL), lambda i: (jnp.maximum(i * per - 1, 0), 0)),
            pl.BlockSpec((HALO, D_MODEL), lambda i: (jnp.minimum((i + 1) * per, nbh - 1), 0)),
            pl.BlockSpec((r, D_MODEL), lambda i: (i, 0)),
            _mod_spec(layer, mod_row, r),
            _layer_spec(pw["w_up"].shape[1:], layer, 1),
            _layer_spec(pw["ffn_conv_w"].shape[1:], layer, 1),
            _layer_spec(pw["w_down"].shape[1:], layer, 1),
            pl.BlockSpec((1, D_MODEL), lambda i: (0, 0)),
        ],
        out_specs=pl.BlockSpec((r, D_MODEL), lambda i: (i, 0)),
        out_shape=jax.ShapeDtypeStruct((n, D_MODEL), F32),
        scratch_shapes=[pltpu.VMEM((r, D_FF), BF16)],
        compiler_params=_cparams(("arbitrary",)),
        name="conv_mlp",
    )(xn2, xn2, xn2, x1, mod4, pw["w_up"], pw["ffn_conv_w"], pw["w_down"], fnorm)


def _rope_swap_perm():
    idx = np.arange(C_ROPE).reshape(2, 2, C_ROPE // 4)
    return idx[:, ::-1, :].reshape(-1)


def _pack_w_in(w_in):
    w = w_in.astype(BF16)
    depth = w.shape[0]
    swap = _rope_swap_perm()
    kv = w[:, :, OFF_KV:OFF_KV + CACHE_W]
    kr = kv[:, :, C_KV_LORA:]
    pad = jnp.zeros((depth, D_MODEL, KV_EXT - CACHE_W - C_ROPE), BF16)
    return jnp.concatenate([w[:, :, :OFF_KV], kv, kr[:, :, swap], pad, w[:, :, OFF_KV + CACHE_W:]], axis=2)


def _pack_w_q_up(w):
    depth = w.shape[0]
    swap = _rope_swap_perm()
    w4 = w.astype(BF16).reshape(depth, C_Q_LORA, C_HEADS, C_NOPE + C_ROPE)
    rope = w4[..., C_NOPE:]
    return jnp.concatenate([w4, rope[..., swap]], axis=3).reshape(depth, C_Q_LORA, C_HEADS * LANES)


def _pack_w_kv_up(w):
    depth = w.shape[0]
    w4 = w.astype(BF16).reshape(depth, C_KV_LORA, C_HEADS, C_NOPE + C_V)
    kn = jnp.concatenate([w4[..., :C_NOPE], jnp.zeros((depth, C_KV_LORA, C_HEADS, LANES - C_NOPE), BF16)], axis=3)
    return jnp.concatenate(
        [kn.reshape(depth, C_KV_LORA, C_HEADS * LANES), w4[..., C_NOPE:].reshape(depth, C_KV_LORA, C_HEADS * C_V)],
        axis=2)


def _pad_axis(a, size, axis):
    shape = list(a.shape)
    shape[axis] = size - a.shape[axis]
    return jnp.concatenate([a, jnp.zeros(shape, a.dtype)], axis=axis)


def _rope_tables(seq, rotate):
    scale = (C_NOPE + C_ROPE) ** -0.5 * LOG2E
    n_freq = C_ROPE // 4
    if rotate:
        rows = seq // GRID_W
        r = jnp.repeat(jnp.arange(rows, dtype=F32), GRID_W)
        col = jnp.tile(jnp.arange(GRID_W, dtype=F32), rows)
        freq = ROPE_THETA ** (-jnp.arange(n_freq, dtype=F32) / n_freq)
        ang = jnp.stack([r[:, None] * freq, col[:, None] * freq], axis=1)
        cos, sin = jnp.cos(ang), jnp.sin(ang)
    else:
        cos = jnp.ones((seq, 2, n_freq), F32)
        sin = jnp.zeros((seq, 2, n_freq), F32)
    cos_full = jnp.stack([cos, cos], axis=2).reshape(seq, C_ROPE)
    sin_signed = jnp.stack([-sin, sin], axis=2).reshape(seq, C_ROPE)
    tabq = scale * jnp.concatenate([jnp.ones((seq, C_NOPE), F32), cos_full, sin_signed], axis=1)
    tabk = jnp.concatenate([cos_full, sin_signed, jnp.zeros((seq, LANES - 2 * C_ROPE), F32)], axis=1)
    return tabq, tabk


def _layer(x2d, batch, seq, layer, mod4, mod_row, pw, lb_all, s_init, cache_pad, tabs, want_ctx_outputs,
           final, fnorm):
    tabq, tabk = tabs
    res = _in_call(x2d, layer, mod4, mod_row, pw, lb_all, tabq, tabk, want_ctx_outputs)
    q, gf, gb, kf, kb, v, ga, bb, u, qfull, kfull, vv, sg = res[:13]
    own_cache = res[13].reshape(batch, seq, CACHE_W) if want_ctx_outputs else None
    ya, states = _hgrn_call(q, gf, gb, kf, kb, v, ga, layer, pw["gnorm"], s_init, batch, seq, want_ctx_outputs)
    kc = vc = None
    if cache_pad is not None:
        kc, vc = _cache_call(cache_pad, layer, pw["w_kv_up"])
    yc = _attn_call(qfull, kfull, vv, kc, vc, batch, seq)
    x1, xn2 = _merge_call(x2d, ya, yc, bb, u, sg, layer, mod4, mod_row, pw, seq)
    x2 = _ffn_call(xn2, x1, layer, mod4, mod_row, pw, fnorm, seq, final)
    return x2, states, own_cache


def kernel(x_prompt, x_sample, state_hgrn, cache_mla, c, c_ctx, w_ada, b_ada, norm1, w_in, hgrn_lb_logits, hgrn_gnorm, w_o_hgrn, conv_w, w_o_conv, mla_q_norm, w_q_up, mla_kv_norm, w_kv_up, w_o_mla, w_out, norm2, w_up, ffn_conv_w, w_down, final_norm):
    depth = w_in.shape[0]
    bp, sp, _ = x_prompt.shape
    bs, ss, _ = x_sample.shape
    assert 1 + bs <= MOD_ROWS
    for rows in (IN_ROWS, MERGE_ROWS, FFN_ROWS):
        assert (bp * sp) % rows == 0 and (bs * ss) % rows == 0
        assert (rows % sp == 0 or sp % rows == 0) and ss % rows == 0

    cvec = _pad_axis(jnp.concatenate([c_ctx[None, :], c], axis=0), MOD_ROWS, 0)
    mod4 = _ada_call(cvec, w_ada, b_ada).reshape(depth, MOD_ROWS, 6, D_MODEL)
    lb_all = _lb_call(hgrn_lb_logits)
    tabs_ctx = _rope_tables(IN_ROWS, rotate=False)
    tabs_lat = _rope_tables(ss, rotate=True)
    fnorm = final_norm.reshape(1, D_MODEL)
    cache_pad = _pad_axis(cache_mla, KV_EXT, 3)

    def vec(a):
        return a.reshape(depth, 1, -1)

    pw = {
        "norm1": vec(norm1), "w_in": _pack_w_in(w_in), "gnorm": vec(hgrn_gnorm),
        "w_o_hgrn": w_o_hgrn.astype(BF16), "conv_w": _pad_axis(conv_w, SUBLANES, 1),
        "w_o_conv": w_o_conv.astype(BF16), "q_norm": vec(mla_q_norm), "w_q_up": _pack_w_q_up(w_q_up),
        "kv_norm": vec(mla_kv_norm), "w_kv_up": _pack_w_kv_up(w_kv_up), "w_o_mla": w_o_mla.astype(BF16),
        "w_out": w_out.astype(BF16), "norm2": vec(norm2), "w_up": w_up.astype(BF16),
        "ffn_conv_w": _pad_axis(ffn_conv_w, SUBLANES, 1), "w_down": w_down.astype(BF16),
    }

    xp = x_prompt.reshape(bp * sp, D_MODEL)
    xs = x_sample.reshape(bs * ss, D_MODEL)
    new_states, new_caches = [], []
    for l in range(depth):
        final = l == depth - 1
        xp, st, kvc = _layer(xp, bp, sp, l, mod4, lambda r: 0, pw, lb_all, None, None, tabs_ctx,
                             True, final, fnorm)
        new_states.append(st)
        new_caches.append(kvc)
        xs, _, _ = _layer(xs, bs, ss, l, mod4, lambda r: 1 + r // ss, pw, lb_all, state_hgrn,
                          cache_pad, tabs_lat, False, final, fnorm)
    y_prompt = xp.reshape(bp, sp, D_MODEL)
    y_sample = xs.reshape(bs, ss, D_MODEL)
    return (y_prompt, y_sample, jnp.stack(new_states, axis=1), jnp.stack(new_caches, axis=1))
```

```python
import functools

import numpy as np
import jax
import jax.numpy as jnp
from jax import lax
from jax.experimental import pallas as pl
from jax.experimental.pallas import tpu as pltpu

F32 = jnp.float32
BF16 = jnp.bfloat16

D_MODEL = 1024
GRID_W = 64
EPS = 1e-6
A_HEADS = 4
A_DK = 128
A_DV = 128
A_W = A_HEADS * A_DK
LB_FLOOR = 1e-30
B_W = 512
C_HEADS = 8
C_NOPE = 64
C_ROPE = 32
C_V = 64
C_Q_LORA = 384
C_KV_LORA = 256
ROPE_THETA = 10000.0
D_FF = 2816
N_BRANCH = 3
CACHE_W = C_KV_LORA + C_ROPE
LOG2E = 1.4426950408889634

LANES = 128
SUBLANES = 8
HALO = 16
VMEM_LIMIT_BYTES = 56 * 1024 * 1024

IN_ROWS = 512
HGRN_CHUNK = 128
HGRN_VMEM_BUDGET = 32 * 1024 * 1024
ATTN_Q_ROWS = 512
MERGE_ROWS = 512
FFN_ROWS = 1024
FFN_COLS = 256
MOD_ROWS = 16
ADA_COLS = 1536

OFF_CQ = 8 * A_W
OFF_KV = OFF_CQ + C_Q_LORA
KV_EXT = 3 * LANES
OFF_GATE = OFF_KV + KV_EXT
IN_COLS_EXT = OFF_GATE + N_BRANCH * D_MODEL


def _cparams(sem):
    return pltpu.CompilerParams(dimension_semantics=sem, vmem_limit_bytes=VMEM_LIMIT_BYTES)


def _dot(a, b):
    return jnp.dot(a, b, preferred_element_type=F32)


def _dot_nt(a, b):
    return lax.dot_general(a, b, (((1,), (1,)), ((), ())), preferred_element_type=F32)


def _dot_tn(a, b):
    return lax.dot_general(a, b, (((0,), (0,)), ((), ())), preferred_element_type=F32)


def _rms(x, g):
    ms = jnp.mean(x * x, axis=-1, keepdims=True)
    return x * lax.rsqrt(ms + EPS) * g


def _sigmoid_pair(z):
    t = jnp.exp(-jnp.abs(z))
    r = 1.0 / (1.0 + t)
    big, small = r, t * r
    pos = z >= 0
    return jnp.where(pos, big, small), jnp.where(pos, small, big)


def _sigmoid(z):
    return 1.0 / (1.0 + jnp.exp(-z))


def _silu(z):
    return z * _sigmoid(z)


def _layer_spec(shape, layer, grid_rank, single=True):
    nd = len(shape)
    if grid_rank == 1:
        imap = lambda i: (layer,) + (0,) * nd
    else:
        imap = lambda i, j: (layer,) + (0,) * nd
    if single:
        return pl.BlockSpec((None,) + tuple(shape), imap, pipeline_mode=pl.Buffered(1))
    return pl.BlockSpec((None,) + tuple(shape), imap)


def _mod_spec(layer, mod_row, rows, grid_rank=1):
    if grid_rank == 1:
        return pl.BlockSpec((None, None, 6, D_MODEL), lambda i: (layer, mod_row(i * rows), 0, 0))
    return pl.BlockSpec((None, None, 6, D_MODEL), lambda i, j: (layer, mod_row(i * rows), 0, 0))


def _ada_kernel(c_ref, w_ref, b_ref, o_ref):
    s = _silu(c_ref[...])
    o_ref[...] = jnp.dot(s, w_ref[...], preferred_element_type=F32,
                         precision=lax.Precision.HIGHEST) + b_ref[...]


def _ada_call(cvec, w_ada, b_ada):
    depth = w_ada.shape[0]
    ncol = w_ada.shape[2] // ADA_COLS
    return pl.pallas_call(
        _ada_kernel,
        grid=(depth, ncol),
        in_specs=[
            pl.BlockSpec((MOD_ROWS, D_MODEL), lambda l, j: (0, 0)),
            pl.BlockSpec((None, D_MODEL, ADA_COLS), lambda l, j: (l, 0, j)),
            pl.BlockSpec((None, 1, ADA_COLS), lambda l, j: (l, 0, j)),
        ],
        out_specs=pl.BlockSpec((None, MOD_ROWS, ADA_COLS), lambda l, j: (l, 0, j)),
        out_shape=jax.ShapeDtypeStruct((depth, MOD_ROWS, w_ada.shape[2]), F32),
        compiler_params=_cparams(("arbitrary", "arbitrary")),
        name="ada_mod",
    )(cvec, w_ada, b_ada.reshape(depth, 1, -1))


def _lb_kernel(x_ref, o_ref):
    depth = x_ref.shape[0]
    xs = [x_ref[l] for l in range(depth)]
    m = xs[0]
    for l in range(1, depth):
        m = jnp.maximum(m, xs[l])
    es = [jnp.exp(x - m) for x in xs]
    tot = es[0]
    for l in range(1, depth):
        tot = tot + es[l]
    ps = [e / tot for e in es]
    run = jnp.zeros_like(ps[0])
    for l in range(depth):
        run = run + ps[l]
        o_ref[l] = run - ps[0]


def _lb_call(logits):
    return pl.pallas_call(
        _lb_kernel,
        out_shape=jax.ShapeDtypeStruct(logits.shape, F32),
        name="hgrn_lb",
    )(logits)


def _place_rope_key(z3, tabk):
    tmp = z3 * tabk
    rot = tmp + pltpu.roll(tmp, LANES - C_ROPE, axis=1)
    lane = lax.broadcasted_iota(jnp.int32, rot.shape, 1)
    rot = jnp.where(lane < C_ROPE, rot, 0.0)
    return pltpu.roll(rot, C_NOPE, axis=1) + pltpu.roll(rot, C_NOPE + C_ROPE, axis=1)


def _in_kernel(x_ref, mod_ref, n1_ref, w_ref, lb_ref, qn_ref, wq_ref, kvn_ref, wkv_ref,
               tabq_ref, tabk_ref,
               q_ref, gf_ref, gb_ref, kf_ref, kb_ref, v_ref, ga_ref, bb_ref, u_ref,
               qfull_ref, kfull_ref, vv_ref, sg_ref, *cache_ref):
    x = x_ref[...]
    shift1 = mod_ref[0:1, :]
    scale1 = mod_ref[1:2, :]
    xn = _rms(x, n1_ref[...]) * (1.0 + scale1) + shift1
    xb = xn.astype(BF16)

    def proj(off, width):
        return _dot(xb, w_ref[:, off:off + width])

    q_ref[...] = (_silu(proj(0, A_W)) * (A_DK ** -0.5)).astype(BF16)
    for d, (g_ref, k_ref) in enumerate(((gf_ref, kf_ref), (gb_ref, kb_ref))):
        z = proj((1 + d) * A_W, A_W)
        lb = lb_ref[d:d + 1, :]
        sp, sn = _sigmoid_pair(z)
        g_ref[...] = jnp.log(jnp.maximum(lb, LB_FLOOR) + (1.0 - lb) * sp) * LOG2E
        k_ref[...] = ((1.0 - lb) * sn - jnp.maximum(LB_FLOOR - lb, 0.0)).astype(BF16)
    v_ref[...] = proj(3 * A_W, A_W).astype(BF16)
    ga_ref[...] = _silu(proj(4 * A_W, A_W)).astype(BF16)
    bb_ref[...] = proj(5 * A_W, B_W).astype(BF16)
    u_ref[...] = (proj(6 * A_W, B_W) * proj(7 * A_W, B_W)).astype(BF16)
    cqn = _rms(proj(OFF_CQ, C_Q_LORA), qn_ref[...]).astype(BF16)
    tabq = jnp.concatenate([tabq_ref[...]] * 2, axis=1)
    for h in range(0, C_HEADS, 2):
        qz = _dot(cqn, wq_ref[:, h * LANES:(h + 2) * LANES])
        qfull_ref[:, h * LANES:(h + 2) * LANES] = (qz * tabq).astype(BF16)
    zkv = proj(OFF_KV, KV_EXT)
    ckvn = _rms(zkv[:, :C_KV_LORA], kvn_ref[...])
    z3 = zkv[:, C_KV_LORA:]
    if cache_ref:
        cache_ref[0][:, :C_KV_LORA] = ckvn
        cache_ref[0][:, C_KV_LORA:] = z3[:, :C_ROPE]
    krr = _place_rope_key(z3, tabk_ref[...])
    krr = jnp.concatenate([krr, krr], axis=1)
    cb = ckvn.astype(BF16)
    for h in range(0, C_HEADS, 2):
        kz = _dot(cb, wkv_ref[:, h * LANES:(h + 2) * LANES])
        kfull_ref[:, h * LANES:(h + 2) * LANES] = (kz + krr).astype(BF16)
    vv_ref[...] = _dot(cb, wkv_ref[:, C_HEADS * LANES:]).astype(BF16)
    for j in range(N_BRANCH * D_MODEL // A_W):
        sg_ref[:, j * A_W:(j + 1) * A_W] = _sigmoid(proj(OFF_GATE + j * A_W, A_W)).astype(BF16)


def _in_call(x2d, layer, mod4, mod_row, pw, lb_all, tabq, tabk, want_cache):
    n = x2d.shape[0]
    tm = IN_ROWS
    tab_blocks = tabq.shape[0] // tm

    def row(width):
        return pl.BlockSpec((tm, width), lambda i: (i, 0))

    outs = [
        (A_W, BF16),
        (A_W, F32), (A_W, F32),
        (A_W, BF16), (A_W, BF16),
        (A_W, BF16),
        (A_W, BF16),
        (B_W, BF16),
        (B_W, BF16),
        (C_HEADS * LANES, BF16),
        (C_HEADS * LANES, BF16),
        (C_HEADS * C_V, BF16),
        (N_BRANCH * D_MODEL, BF16),
    ]
    if want_cache:
        outs.append((CACHE_W, F32))
    return pl.pallas_call(
        _in_kernel,
        grid=(n // tm,),
        in_specs=[
            row(D_MODEL),
            _mod_spec(layer, mod_row, tm),
            _layer_spec((1, D_MODEL), layer, 1),
            _layer_spec((D_MODEL, IN_COLS_EXT), layer, 1),
            _layer_spec((2, A_W), layer, 1),
            _layer_spec((1, C_Q_LORA), layer, 1),
            _layer_spec((C_Q_LORA, C_HEADS * LANES), layer, 1),
            _layer_spec((1, C_KV_LORA), layer, 1),
            _layer_spec((C_KV_LORA, C_HEADS * LANES + C_HEADS * C_V), layer, 1),
            pl.BlockSpec((tm, LANES), lambda i: (i % tab_blocks, 0)),
            pl.BlockSpec((tm, LANES), lambda i: (i % tab_blocks, 0)),
        ],
        out_specs=[row(w) for w, _ in outs],
        out_shape=[jax.ShapeDtypeStruct((n, w), dt) for w, dt in outs],
        compiler_params=_cparams(("arbitrary",)),
        name="in_proj",
    )(x2d, mod4, pw["norm1"], pw["w_in"], lb_all, pw["q_norm"], pw["w_q_up"], pw["kv_norm"], pw["w_kv_up"],
      tabq, tabk)


def _cache_kernel(c_ref, wkv_ref, kfull_ref, vv_ref):
    c = c_ref[...]
    cb = c[:, :C_KV_LORA].astype(BF16)
    z3 = c[:, C_KV_LORA:]
    lane = lax.broadcasted_iota(jnp.int32, z3.shape, 1)
    kr = jnp.where(lane < C_ROPE, z3, 0.0)
    krr = pltpu.roll(kr, C_NOPE, axis=1) + pltpu.roll(kr, C_NOPE + C_ROPE, axis=1)
    for h in range(C_HEADS):
        kz = _dot(cb, wkv_ref[:, h * LANES:(h + 1) * LANES])
        kfull_ref[:, h * LANES:(h + 1) * LANES] = (kz + krr).astype(BF16)
    vv_ref[...] = _dot(cb, wkv_ref[:, C_HEADS * LANES:]).astype(BF16)


def _cache_call(cache_pad, layer, wkv):
    b, _, p, w = cache_pad.shape
    return pl.pallas_call(
        _cache_kernel,
        grid=(b,),
        in_specs=[
            pl.BlockSpec((None, None, p, w), lambda i: (i, layer, 0, 0)),
            _layer_spec(wkv.shape[1:], layer, 1, single=False),
        ],
        out_specs=[
            pl.BlockSpec((None, p, C_HEADS * LANES), lambda i: (i, 0, 0)),
            pl.BlockSpec((None, p, C_HEADS * C_V), lambda i: (i, 0, 0)),
        ],
        out_shape=[
            jax.ShapeDtypeStruct((b, p, C_HEADS * LANES), BF16),
            jax.ShapeDtypeStruct((b, p, C_HEADS * C_V), BF16),
        ],
        compiler_params=_cparams(("arbitrary",)),
        name="cache_expand",
    )(cache_pad, wkv)


def _level_matrix(c, rev):
    t = np.arange(c)[:, None]
    s = np.arange(c)[None, :]
    lev = np.floor(np.log2(np.maximum(np.bitwise_xor(t, s), 1))).astype(np.int32)
    lev = np.where(s == t, -1, lev)
    lev = np.where((s < t) if rev else (s > t), -2, lev)
    return lev.astype(np.int32)


def _prefix_matrix(c, rev):
    t = np.arange(c)[:, None]
    s = np.arange(c)[None, :]
    incl = ((s >= t) if rev else (s <= t)).astype(np.float32)
    return np.concatenate([incl, incl], axis=1)


def _level_exponent(b, g, h, rev):
    c, dk = b.shape
    ref = h if rev else h - 1
    if 2 * h > SUBLANES:
        parts = []
        for gs in range(0, c, 2 * h):
            beta = jnp.broadcast_to(b[gs + ref:gs + ref + 1, :], (h, dk))
            lo = b[gs:gs + h, :]
            hi = b[gs + h:gs + 2 * h, :]
            parts += [lo - beta, beta - hi] if rev else [beta - lo, hi - beta]
        return jnp.concatenate(parts, axis=0)
    row = lax.broadcasted_iota(jnp.int32, (SUBLANES, dk), 0)
    upper = (row % (2 * h)) >= h
    keep = jnp.logical_not(upper) if rev else upper
    if h == 1:
        parts = [jnp.where(keep, g[j:j + SUBLANES, :], 0.0) for j in range(0, c, SUBLANES)]
        return jnp.concatenate(parts, axis=0)
    sign = jnp.where(keep, 1.0, -1.0)
    parts = []
    for j in range(0, c, SUBLANES):
        bj = b[j:j + SUBLANES, :]
        beta = jnp.broadcast_to(bj[ref:ref + 1, :], (SUBLANES, dk))
        for gs in range(2 * h, SUBLANES, 2 * h):
            beta = jnp.where(row >= gs, jnp.broadcast_to(bj[gs + ref:gs + ref + 1, :], (SUBLANES, dk)), beta)
        parts.append((bj - beta) * sign)
    return jnp.concatenate(parts, axis=0)


def _hgrn_chunks(streams, lev, pmat):
    c = streams[0][0].shape[0]
    bs = []
    for q, k, g, v, st, rev in streams:
        g_hi = g.astype(BF16)
        g_lo = (g - g_hi.astype(F32)).astype(BF16)
        bs.append(_dot(pmat[rev], jnp.concatenate([g_hi, g_lo], axis=0)))
    qfs = [s[0].astype(F32) for s in streams]
    kfs = [s[1].astype(F32) for s in streams]
    a_s = [jnp.zeros((c, c), F32) for _ in streams]
    for hl in range(int(np.log2(c))):
        ps = []
        for (q, k, g, v, st, rev), b, qf, kf in zip(streams, bs, qfs, kfs):
            x = jnp.exp2(_level_exponent(b, g, 1 << hl, rev))
            ps.append(_dot_nt((qf * x).astype(BF16), (kf * x).astype(BF16)))
        a_s = [jnp.where(lev[s[5]] == hl, p, a) for s, p, a in zip(streams, ps, a_s)]
    ps = [_dot_nt(s[0], s[1]) for s in streams]
    a_s = [jnp.where(lev[s[5]] == -1, p, a) for s, p, a in zip(streams, ps, a_s)]
    outs = []
    for (q, k, g, v, st, rev), b, a, qf in zip(streams, bs, a_s, qfs):
        outs.append(_dot(a.astype(BF16), v) + _dot_nt((qf * jnp.exp2(b)).astype(BF16), st.astype(BF16)))
    new_states = []
    for (q, k, g, v, st, rev), b, kf in zip(streams, bs, kfs):
        tot = b[0:1, :] if rev else b[c - 1:c, :]
        ks = (kf * jnp.exp2(tot - b)).astype(BF16)
        new_states.append(st * jnp.exp2(tot) + _dot_tn(v, ks))
    return outs, new_states


def _hgrn_kernel(*refs, has_init, want_state):
    refs = list(refs)
    (q_ref, gf_ref, gb_ref, kf_ref, kb_ref, v_ref, ga_ref, gn_ref,
     levf_ref, levb_ref, pmf_ref, pmb_ref) = refs[:12]
    pos = 12
    s0_ref = None
    if has_init:
        s0_ref = refs[pos]
        pos += 1
    y_ref = refs[pos]
    pos += 1
    sfin_ref = None
    if want_state:
        sfin_ref = refs[pos]
        pos += 1
    o_scr, st_scr = refs[pos], refs[pos + 1]

    t = q_ref.shape[0]
    c = HGRN_CHUNK
    n = t // c
    hp = q_ref.shape[1] // A_DK
    for d in range(2):
        for h in range(hp):
            st_scr[d, h] = s0_ref[d, h].T if has_init else jnp.zeros((A_DV, A_DK), F32)
    lev = (levf_ref, levb_ref)
    pmat = (pmf_ref, pmb_ref)
    g_refs = (gf_ref, gb_ref)
    k_refs = (kf_ref, kb_ref)

    def body(i, carry):
        streams = []
        rows = []
        for d in range(2):
            ci = i if d == 0 else n - 1 - i
            sl = pl.ds(pl.multiple_of(ci * c, c), c)
            rows.append(sl)
            for h in range(hp):
                cs = slice(h * A_DK, (h + 1) * A_DK)
                streams.append((q_ref[sl, cs], k_refs[d][sl, cs], g_refs[d][sl, cs], v_ref[sl, cs],
                                st_scr[d, h], d))
        outs, new_states = _hgrn_chunks(streams, (lev[0][...], lev[1][...]), (pmat[0][...], pmat[1][...]))
        for idx, (o, st_new) in enumerate(zip(outs, new_states)):
            d, h = divmod(idx, hp)
            st_scr[d, h] = st_new
            o_scr[d, rows[d], h * A_DV:(h + 1) * A_DV] = o
        return carry

    lax.fori_loop(0, n, body, 0)
    for h in range(hp):
        cs = slice(h * A_DV, (h + 1) * A_DV)
        if want_state:
            for d in range(2):
                sfin_ref[d, h] = st_scr[d, h].T
        o = o_scr[0, :, cs] + o_scr[1, :, cs]
        y_ref[:, cs] = (_rms(o, gn_ref[...]) * ga_ref[:, cs]).astype(BF16)


def _hgrn_heads_per_step(seq):
    per_head_row = 2 * (5 * 2 + 2 * 4) * A_DK + 2 * 2 * A_DV + 2 * 4 * A_DV
    hp = A_HEADS
    while hp > 1 and hp * seq * per_head_row > HGRN_VMEM_BUDGET:
        hp //= 2
    return hp


def _hgrn_call(q, gf, gb, kf, kb, v, ga, layer, gnorm, s_init, batch, seq, want_state):
    c = HGRN_CHUNK
    hp = _hgrn_heads_per_step(seq)
    consts =[jnp.asarray(_level_matrix(c, False)), jnp.asarray(_level_matrix(c, True)),
              jnp.asarray(_prefix_matrix(c, False), BF16), jnp.asarray(_prefix_matrix(c, True), BF16)]

    def r3(a):
        return a.reshape(batch, seq, A_W)

    col = pl.BlockSpec((None, seq, hp * A_DK), lambda b, h: (b, 0, h))
    in_specs = [col] * 7 + [_layer_spec((1, A_DV), layer, 2, single=False)]
    in_specs += [pl.BlockSpec(m.shape, lambda b, h: (0, 0)) for m in consts]
    args = [r3(q), r3(gf), r3(gb), r3(kf), r3(kb), r3(v), r3(ga), gnorm] + consts
    has_init = s_init is not None
    if has_init:
        in_specs.append(pl.BlockSpec((None, None, 2, hp, A_DK, A_DV), lambda b, h: (b, layer, 0, h, 0, 0)))
        args.append(s_init)
    out_specs = [col]
    out_shape = [jax.ShapeDtypeStruct((batch, seq, A_W), BF16)]
    if want_state:
        out_specs.append(pl.BlockSpec((None, 2, hp, A_DK, A_DV), lambda b, h: (b, 0, h, 0, 0)))
        out_shape.append(jax.ShapeDtypeStruct((batch, 2, A_HEADS, A_DK, A_DV), F32))
    res = pl.pallas_call(
        functools.partial(_hgrn_kernel, has_init=has_init, want_state=want_state),
        grid=(batch, A_HEADS // hp),
        in_specs=in_specs,
        out_specs=out_specs,
        out_shape=out_shape,
        scratch_shapes=[pltpu.VMEM((2, seq, hp * A_DV), F32), pltpu.VMEM((2, hp, A_DV, A_DK), F32)],
        compiler_params=_cparams(("arbitrary", "arbitrary")),
        name="hgrn",
    )(*args)
    y = res[0].reshape(batch * seq, A_W)
    return y, (res[1] if want_state else None)


def _attn_kernel(*refs, has_cache):
    if has_cache:
        q_ref, k_ref, v_ref, kc_ref, vc_ref, y_ref = refs
    else:
        q_ref, k_ref, v_ref, y_ref = refs
    lane = lax.broadcasted_iota(jnp.int32, (q_ref.shape[0], LANES), 1)

    def scores(h):
        hs = slice(h * LANES, (h + 1) * LANES)
        qh = q_ref[:, hs]
        s1 = _dot_nt(qh, k_ref[:, hs])
        s2 = _dot_nt(qh, kc_ref[:, hs]) if has_cache else None
        return s1, s2

    def attend(h, s1, s2):
        vs = slice((h // 2) * LANES, (h // 2 + 1) * LANES)
        m = jnp.max(s1, axis=-1, keepdims=True)
        if has_cache:
            m = jnp.maximum(m, jnp.max(s2, axis=-1, keepdims=True))
        p1 = jnp.exp2(s1 - m)
        l = jnp.sum(p1, axis=-1, keepdims=True)
        o = _dot(p1.astype(BF16), v_ref[:, vs])
        if has_cache:
            p2 = jnp.exp2(s2 - m)
            l = l + jnp.sum(p2, axis=-1, keepdims=True)
            o = o + _dot(p2.astype(BF16), vc_ref[:, vs])
        return o / l

    pending = scores(0)
    outs = []
    for h in range(C_HEADS):
        nxt = scores(h + 1) if h + 1 < C_HEADS else None
        outs.append(attend(h, *pending))
        pending = nxt
        if h % 2 == 1:
            vs = slice((h // 2) * LANES, (h // 2 + 1) * LANES)
            y_ref[:, vs] = jnp.where(lane < C_V, outs[h - 1], outs[h]).astype(BF16)


def _attn_call(qfull, kfull, vv, kc, vc, batch, seq):
    tq = min(ATTN_Q_ROWS, seq)
    has_cache = kc is not None
    kw = C_HEADS * LANES
    vw = C_HEADS * C_V
    in_specs = [
        pl.BlockSpec((None, tq, kw), lambda b, i: (b, i, 0)),
        pl.BlockSpec((None, seq, kw), lambda b, i: (b, 0, 0)),
        pl.BlockSpec((None, seq, vw), lambda b, i: (b, 0, 0)),
    ]
    args = [qfull.reshape(batch, seq, kw), kfull.reshape(batch, seq, kw), vv.reshape(batch, seq, vw)]
    if has_cache:
        p = kc.shape[1]
        in_specs += [
            pl.BlockSpec((None, p, kw), lambda b, i: (b, 0, 0)),
            pl.BlockSpec((None, p, vw), lambda b, i: (b, 0, 0)),
        ]
        args += [kc, vc]
    y = pl.pallas_call(
        functools.partial(_attn_kernel, has_cache=has_cache),
        grid=(batch, seq // tq),
        in_specs=in_specs,
        out_specs=pl.BlockSpec((None, tq, vw), lambda b, i: (b, i, 0)),
        out_shape=jax.ShapeDtypeStruct((batch, seq, vw), BF16),
        compiler_params=_cparams(("arbitrary", "arbitrary")),
        name="mla_attn",
    )(*args)
    return y.reshape(batch * seq, vw)


def _conv3_rows(u, prev_row, next_row, cw_ref, seq):
    r, w = u.shape
    sub = min(seq, r)
    whole = seq <= r
    i8 = lax.broadcasted_iota(jnp.int32, (SUBLANES, w), 0)
    w0, w1, w2 = cw_ref[0:1, :], cw_ref[1:2, :], cw_ref[2:3, :]
    outs = []
    for s in range(r // sub):
        blk = u[s * sub:(s + 1) * sub, :]
        p = 0.0 if whole else prev_row
        n = 0.0 if whole else next_row
        up = pltpu.roll(blk, 1, axis=0)
        dn = pltpu.roll(blk, sub - 1, axis=0)
        up = jnp.concatenate([jnp.where(i8 == 0, p, up[:SUBLANES, :]), up[SUBLANES:, :]], axis=0)
        dn = jnp.concatenate([dn[:sub - SUBLANES, :], jnp.where(i8 == SUBLANES - 1, n, dn[sub - SUBLANES:, :])],
                             axis=0)
        outs.append(up * w0 + blk * w1 + dn * w2)
    return outs[0] if len(outs) == 1 else jnp.concatenate(outs, axis=0)


def _seq_edges(i, rows, seq):
    first = i * rows
    return first % seq == 0, (first + rows) % seq == 0


def _merge_kernel(x_ref, ya_ref, yc_ref, bb_ref, u_ref, up_ref, un_ref, sg_ref, mod_ref,
                  wa_ref, wb_ref, wc_ref, wout_ref, cw_ref, n2_ref, x1_ref, xn2_ref, *, seq):
    tm = x_ref.shape[0]
    at_start, at_end = _seq_edges(pl.program_id(0), tm, seq)
    prev_row = jnp.where(at_start, 0.0, up_ref[HALO - 1:HALO, :].astype(F32))
    next_row = jnp.where(at_end, 0.0, un_ref[0:1, :].astype(F32))
    conv = _conv3_rows(u_ref[...].astype(F32), prev_row, next_row, cw_ref, seq)
    yb = (bb_ref[...] * conv).astype(BF16)
    half = tm // 2
    hs = []
    for s in range(2):
        rs = slice(s * half, (s + 1) * half)
        pa = _dot(ya_ref[rs, :], wa_ref[...])
        pc = _dot(yc_ref[rs, :], wc_ref[...])
        pb = _dot(yb[rs, :], wb_ref[...])
        hs.append((sg_ref[rs, 0:D_MODEL] * pa + sg_ref[rs, D_MODEL:2 * D_MODEL] * pb
                   + sg_ref[rs, 2 * D_MODEL:3 * D_MODEL] * pc).astype(BF16))
    gate1 = mod_ref[2:3, :]
    shift2 = mod_ref[3:4, :]
    scale2 = mod_ref[4:5, :]
    for s in range(2):
        rs = slice(s * half, (s + 1) * half)
        x1 = x_ref[rs, :] + gate1 * _dot(hs[s], wout_ref[...])
        x1_ref[rs, :] = x1
        xn2_ref[rs, :] = (_rms(x1, n2_ref[...]) * (1.0 + scale2) + shift2).astype(BF16)


def _merge_call(x2d, ya, yc, bb, u, sg, layer, mod4, mod_row, pw, seq):
    n = x2d.shape[0]
    tm = MERGE_ROWS
    nbh = n // HALO
    per = tm // HALO

    def row(width):
        return pl.BlockSpec((tm, width), lambda i: (i, 0))

    def wspec(name):
        return _layer_spec(pw[name].shape[1:], layer, 1)

    return pl.pallas_call(
        functools.partial(_merge_kernel, seq=seq),
        grid=(n // tm,),
        in_specs=[
            row(D_MODEL), row(A_W), row(C_HEADS * C_V), row(B_W), row(B_W),
            pl.BlockSpec((HALO, B_W), lambda i: (jnp.maximum(i * per - 1, 0), 0)),
            pl.BlockSpec((HALO, B_W), lambda i: (jnp.minimum((i + 1) * per, nbh - 1), 0)),
            row(N_BRANCH * D_MODEL),
            _mod_spec(layer, mod_row, tm),
            wspec("w_o_hgrn"), wspec("w_o_conv"), wspec("w_o_mla"), wspec("w_out"), wspec("conv_w"),
            wspec("norm2"),
        ],
        out_specs=[row(D_MODEL), row(D_MODEL)],
        out_shape=[jax.ShapeDtypeStruct((n, D_MODEL), F32), jax.ShapeDtypeStruct((n, D_MODEL), BF16)],
        compiler_params=_cparams(("arbitrary",)),
        name="merge",
    )(x2d, ya, yc, bb, u, u, u, sg, mod4, pw["w_o_hgrn"], pw["w_o_conv"], pw["w_o_mla"], pw["w_out"],
      pw["conv_w"], pw["norm2"])


def _ffn_kernel(xn_ref, xp_ref, xx_ref, x1_ref, mod_ref, wup_ref, cw_ref, wd_ref,
                fn_ref, out_ref, h_scr, lhs_scr, *, seq, final):
    r = xn_ref.shape[0]
    tn = FFN_COLS
    at_start, at_end = _seq_edges(pl.program_id(0), r, seq)
    need_halo = seq > r
    if need_halo:
        lhs_scr[0:HALO, :] = xp_ref[...]
        lhs_scr[HALO:HALO + r, :] = xn_ref[...]
        lhs_scr[HALO + r:, :] = xx_ref[...]
        xn = lhs_scr[...]
    else:
        xn = xn_ref[...]

    def conv_part(off):
        um = _dot(xn, wup_ref[:, off:off + tn])
        prev_row = next_row = None
        if need_halo:
            prev_row = jnp.where(at_start, 0.0, um[HALO - 1:HALO, :])
            next_row = jnp.where(at_end, 0.0, um[HALO + r:HALO + r + 1, :])
            um = um[HALO:HALO + r, :]
        return _conv3_rows(um, prev_row, next_row, cw_ref.at[:, off:off + tn], seq)

    for j in range(D_FF // tn):
        a = conv_part(j * tn)
        bval = conv_part(D_FF + j * tn)
        h_scr[:, j * tn:(j + 1) * tn] = (_silu(a) * bval).astype(BF16)

    gate2 = mod_ref[5:6, :]
    y = x1_ref[...] + gate2 * _dot(h_scr[...], wd_ref[...])
    if final:
        y = _rms(y, fn_ref[...])
    out_ref[...] = y


def _ffn_call(xn2, x1, layer, mod4, mod_row, pw, fnorm, seq, final):
    n = xn2.shape[0]
    r = FFN_ROWS
    nbh = n // HALO
    per = r // HALO
    return pl.pallas_call(
        functools.partial(_ffn_kernel, seq=seq, final=final),
        grid=(n // r,),
        in_specs=[
            pl.BlockSpec((r, D_MODEL), lambda i: (i, 0)),
            pl.BlockSpec((HALO, D_MODEL), lambda i: (jnp.maximum(i * per - 1, 0), 0)),
            pl.BlockSpec((HALO, D_MODEL), lambda i: (jnp.minimum((i + 1) * per, nbh - 1), 0)),
            pl.BlockSpec((r, D_MODEL), lambda i: (i, 0)),
            _mod_spec(layer, mod_row, r),
            _layer_spec(pw["w_up"].shape[1:], layer, 1),
            _layer_spec(pw["ffn_conv_w"].shape[1:], layer, 1),
            _layer_spec(pw["w_down"].shape[1:], layer, 1),
            pl.BlockSpec((1, D_MODEL), lambda i: (0, 0)),
        ],
        out_specs=pl.BlockSpec((r, D_MODEL), lambda i: (i, 0)),
        out_shape=jax.ShapeDtypeStruct((n, D_MODEL), F32),
        scratch_shapes=[pltpu.VMEM((r, D_FF), BF16),
                        pltpu.VMEM((r + 2 * HALO if seq > r else HALO, D_MODEL), BF16)],
        compiler_params=_cparams(("arbitrary",)),
        name="conv_mlp",
    )(xn2, xn2, xn2, x1, mod4, pw["w_up"], pw["ffn_conv_w"], pw["w_down"], fnorm)


def _rope_swap_perm():
    idx = np.arange(C_ROPE).reshape(2, 2, C_ROPE // 4)
    return idx[:, ::-1, :].reshape(-1)


def _pack_w_in(w_in):
    w = w_in.astype(BF16)
    depth = w.shape[0]
    swap = _rope_swap_perm()
    kv = w[:, :, OFF_KV:OFF_KV + CACHE_W]
    kr = kv[:, :, C_KV_LORA:]
    pad = jnp.zeros((depth, D_MODEL, KV_EXT - CACHE_W - C_ROPE), BF16)
    return jnp.concatenate([w[:, :, :OFF_KV], kv, kr[:, :, swap], pad, w[:, :, OFF_KV + CACHE_W:]], axis=2)


def _pack_w_q_up(w):
    depth = w.shape[0]
    swap = _rope_swap_perm()
    w4 = w.astype(BF16).reshape(depth, C_Q_LORA, C_HEADS, C_NOPE + C_ROPE)
    rope = w4[..., C_NOPE:]
    return jnp.concatenate([w4, rope[..., swap]], axis=3).reshape(depth, C_Q_LORA, C_HEADS * LANES)


def _pack_w_kv_up(w):
    depth = w.shape[0]
    w4 = w.astype(BF16).reshape(depth, C_KV_LORA, C_HEADS, C_NOPE + C_V)
    kn = jnp.concatenate([w4[..., :C_NOPE], jnp.zeros((depth, C_KV_LORA, C_HEADS, LANES - C_NOPE), BF16)], axis=3)
    return jnp.concatenate(
        [kn.reshape(depth, C_KV_LORA, C_HEADS * LANES), w4[..., C_NOPE:].reshape(depth, C_KV_LORA, C_HEADS * C_V)],
        axis=2)


def _pad_axis(a, size, axis):
    shape = list(a.shape)
    shape[axis] = size - a.shape[axis]
    return jnp.concatenate([a, jnp.zeros(shape, a.dtype)], axis=axis)


def _rope_tables(seq, rotate):
    scale = (C_NOPE + C_ROPE) ** -0.5 * LOG2E
    n_freq = C_ROPE // 4
    if rotate:
        rows = seq // GRID_W
        r = jnp.repeat(jnp.arange(rows, dtype=F32), GRID_W)
        col = jnp.tile(jnp.arange(GRID_W, dtype=F32), rows)
        freq = ROPE_THETA ** (-jnp.arange(n_freq, dtype=F32) / n_freq)
        ang = jnp.stack([r[:, None] * freq, col[:, None] * freq], axis=1)
        cos, sin = jnp.cos(ang), jnp.sin(ang)
    else:
        cos = jnp.ones((seq, 2, n_freq), F32)
        sin = jnp.zeros((seq, 2, n_freq), F32)
    cos_full = jnp.stack([cos, cos], axis=2).reshape(seq, C_ROPE)
    sin_signed = jnp.stack([-sin, sin], axis=2).reshape(seq, C_ROPE)
    tabq = scale * jnp.concatenate([jnp.ones((seq, C_NOPE), F32), cos_full, sin_signed], axis=1)
    tabk = jnp.concatenate([cos_full, sin_signed, jnp.zeros((seq, LANES - 2 * C_ROPE), F32)], axis=1)
    return tabq, tabk


def _layer(x2d, batch, seq, layer, mod4, mod_row, pw, lb_all, s_init, cache_pad, tabs, want_ctx_outputs,
           final, fnorm):
    tabq, tabk = tabs
    res = _in_call(x2d, layer, mod4, mod_row, pw, lb_all, tabq, tabk, want_ctx_outputs)
    q, gf, gb, kf, kb, v, ga, bb, u, qfull, kfull, vv, sg = res[:13]
    own_cache = res[13].reshape(batch, seq, CACHE_W) if want_ctx_outputs else None
    ya, states = _hgrn_call(q, gf, gb, kf, kb, v, ga, layer, pw["gnorm"], s_init, batch, seq, want_ctx_outputs)
    kc = vc = None
    if cache_pad is not None:
        kc, vc = _cache_call(cache_pad, layer, pw["w_kv_up"])
    yc = _attn_call(qfull, kfull, vv, kc, vc, batch, seq)
    x1, xn2 = _merge_call(x2d, ya, yc, bb, u, sg, layer, mod4, mod_row, pw, seq)
    x2 = _ffn_call(xn2, x1, layer, mod4, mod_row, pw, fnorm, seq, final)
    return x2, states, own_cache


def kernel(x_prompt, x_sample, state_hgrn, cache_mla, c, c_ctx, w_ada, b_ada, norm1, w_in, hgrn_lb_logits, hgrn_gnorm, w_o_hgrn, conv_w, w_o_conv, mla_q_norm, w_q_up, mla_kv_norm, w_kv_up, w_o_mla, w_out, norm2, w_up, ffn_conv_w, w_down, final_norm):
    depth = w_in.shape[0]
    bp, sp, _ = x_prompt.shape
    bs, ss, _ = x_sample.shape
    assert 1 + bs <= MOD_ROWS
    for rows in (IN_ROWS, MERGE_ROWS, FFN_ROWS):
        assert (bp * sp) % rows == 0 and (bs * ss) % rows == 0
        assert (rows % sp == 0 or sp % rows == 0) and ss % rows == 0

    cvec = _pad_axis(jnp.concatenate([c_ctx[None, :], c], axis=0), MOD_ROWS, 0)
    mod4 = _ada_call(cvec, w_ada, b_ada).reshape(depth, MOD_ROWS, 6, D_MODEL)
    lb_all = _lb_call(hgrn_lb_logits)
    tabs_ctx = _rope_tables(IN_ROWS, rotate=False)
    tabs_lat = _rope_tables(ss, rotate=True)
    fnorm = final_norm.reshape(1, D_MODEL)
    cache_pad = _pad_axis(cache_mla, KV_EXT, 3)

    def vec(a):
        return a.reshape(depth, 1, -1)

    pw = {
        "norm1": vec(norm1), "w_in": _pack_w_in(w_in), "gnorm": vec(hgrn_gnorm),
        "w_o_hgrn": w_o_hgrn.astype(BF16), "conv_w": _pad_axis(conv_w, SUBLANES, 1),
        "w_o_conv": w_o_conv.astype(BF16), "q_norm": vec(mla_q_norm), "w_q_up": _pack_w_q_up(w_q_up),
        "kv_norm": vec(mla_kv_norm), "w_kv_up": _pack_w_kv_up(w_kv_up), "w_o_mla": w_o_mla.astype(BF16),
        "w_out": w_out.astype(BF16), "norm2": vec(norm2), "w_up": w_up.astype(BF16),
        "ffn_conv_w": _pad_axis(ffn_conv_w, SUBLANES, 1), "w_down": w_down.astype(BF16),
    }

    xp = x_prompt.reshape(bp * sp, D_MODEL)
    xs = x_sample.reshape(bs * ss, D_MODEL)
    new_states, new_caches = [], []
    for l in range(depth):
        final = l == depth - 1
        xp, st, kvc = _layer(xp, bp, sp, l, mod4, lambda r: 0, pw, lb_all, None, None, tabs_ctx,
                             True, final, fnorm)
        new_states.append(st)
        new_caches.append(kvc)
        xs, _, _ = _layer(xs, bs, ss, l, mod4, lambda r: 1 + r // ss, pw, lb_all, state_hgrn,
                          cache_pad, tabs_lat, False, final, fnorm)
    y_prompt = xp.reshape(bp, sp, D_MODEL)
    y_sample = xs.reshape(bs, ss, D_MODEL)
    return (y_prompt, y_sample, jnp.stack(new_states, axis=1), jnp.stack(new_caches, axis=1))
```

```python
import functools

import numpy as np
import jax
import jax.numpy as jnp
from jax import lax
from jax.experimental import pallas as pl
from jax.experimental.pallas import tpu as pltpu

F32 = jnp.float32
BF16 = jnp.bfloat16

D_MODEL = 1024
GRID_W = 64
EPS = 1e-6
A_HEADS = 4
A_DK = 128
A_DV = 128
A_W = A_HEADS * A_DK
LB_FLOOR = 1e-30
B_W = 512
C_HEADS = 8
C_NOPE = 64
C_ROPE = 32
C_V = 64
C_Q_LORA = 384
C_KV_LORA = 256
ROPE_THETA = 10000.0
D_FF = 2816
N_BRANCH = 3
CACHE_W = C_KV_LORA + C_ROPE
LOG2E = 1.4426950408889634

LANES = 128
SUBLANES = 8
HALO = 16
VMEM_LIMIT_BYTES = 56 * 1024 * 1024

IN_ROWS = 512
IN_LOOKAHEAD = 1
HGRN_CHUNK = 128
HGRN_VMEM_BUDGET = 32 * 1024 * 1024
ATTN_Q_ROWS = 512
MERGE_ROWS = 512
FFN_ROWS = 1024
FFN_COLS = 256
MOD_ROWS = 16
ADA_COLS = 1536

OFF_CQ = 8 * A_W
OFF_KV = OFF_CQ + C_Q_LORA
KV_EXT = 3 * LANES
OFF_GATE = OFF_KV + CACHE_W


def _cparams(sem):
    return pltpu.CompilerParams(dimension_semantics=sem, vmem_limit_bytes=VMEM_LIMIT_BYTES)


def _dot(a, b):
    return jnp.dot(a, b, preferred_element_type=F32)


def _dot_nt(a, b):
    return lax.dot_general(a, b, (((1,), (1,)), ((), ())), preferred_element_type=F32)


def _dot_tn(a, b):
    return lax.dot_general(a, b, (((0,), (0,)), ((), ())), preferred_element_type=F32)


def _rms(x, g):
    ms = jnp.mean(x * x, axis=-1, keepdims=True)
    return x * lax.rsqrt(ms + EPS) * g


def _sigmoid_pair(z):
    t = jnp.exp(-jnp.abs(z))
    r = 1.0 / (1.0 + t)
    big, small = r, t * r
    pos = z >= 0
    return jnp.where(pos, big, small), jnp.where(pos, small, big)


def _sigmoid(z):
    return 1.0 / (1.0 + jnp.exp(-z))


def _silu(z):
    return z * _sigmoid(z)


def _layer_spec(shape, layer, grid_rank, single=True):
    nd = len(shape)
    if grid_rank == 1:
        imap = lambda i: (layer,) + (0,) * nd
    else:
        imap = lambda i, j: (layer,) + (0,) * nd
    if single:
        return pl.BlockSpec((None,) + tuple(shape), imap, pipeline_mode=pl.Buffered(1))
    return pl.BlockSpec((None,) + tuple(shape), imap)


def _mod_spec(layer, mod_row, rows, grid_rank=1):
    if grid_rank == 1:
        return pl.BlockSpec((None, None, 6, D_MODEL), lambda i: (layer, mod_row(i * rows), 0, 0))
    return pl.BlockSpec((None, None, 6, D_MODEL), lambda i, j: (layer, mod_row(i * rows), 0, 0))


def _ada_kernel(c_ref, w_ref, b_ref, o_ref):
    s = _silu(c_ref[...])
    o_ref[...] = jnp.dot(s, w_ref[...], preferred_element_type=F32,
                         precision=lax.Precision.HIGHEST) + b_ref[...]


def _ada_call(cvec, w_ada, b_ada):
    depth = w_ada.shape[0]
    ncol = w_ada.shape[2] // ADA_COLS
    return pl.pallas_call(
        _ada_kernel,
        grid=(depth, ncol),
        in_specs=[
            pl.BlockSpec((MOD_ROWS, D_MODEL), lambda l, j: (0, 0)),
            pl.BlockSpec((None, D_MODEL, ADA_COLS), lambda l, j: (l, 0, j)),
            pl.BlockSpec((None, 1, ADA_COLS), lambda l, j: (l, 0, j)),
        ],
        out_specs=pl.BlockSpec((None, MOD_ROWS, ADA_COLS), lambda l, j: (l, 0, j)),
        out_shape=jax.ShapeDtypeStruct((depth, MOD_ROWS, w_ada.shape[2]), F32),
        compiler_params=_cparams(("arbitrary", "arbitrary")),
        name="ada_mod",
    )(cvec, w_ada, b_ada.reshape(depth, 1, -1))


def _lb_kernel(x_ref, o_ref):
    depth = x_ref.shape[0]
    xs = [x_ref[l] for l in range(depth)]
    m = xs[0]
    for l in range(1, depth):
        m = jnp.maximum(m, xs[l])
    es = [jnp.exp(x - m) for x in xs]
    tot = es[0]
    for l in range(1, depth):
        tot = tot + es[l]
    ps = [e / tot for e in es]
    run = jnp.zeros_like(ps[0])
    for l in range(depth):
        run = run + ps[l]
        o_ref[l] = run - ps[0]


def _lb_call(logits):
    return pl.pallas_call(
        _lb_kernel,
        out_shape=jax.ShapeDtypeStruct(logits.shape, F32),
        name="hgrn_lb",
    )(logits)


def _place_rope_key(z3, tabk):
    tmp = z3 * tabk
    rot = tmp + pltpu.roll(tmp, LANES - C_ROPE, axis=1)
    lane = lax.broadcasted_iota(jnp.int32, rot.shape, 1)
    rot = jnp.where(lane < C_ROPE, rot, 0.0)
    return pltpu.roll(rot, C_NOPE, axis=1) + pltpu.roll(rot, C_NOPE + C_ROPE, axis=1)


def _in_kernel(x_ref, mod_ref, n1_ref, w_ref, wkvd_ref, wg_ref, lb_ref, qn_ref, wq_ref, kvn_ref, wkv_ref,
               tabq_ref, tabk_ref,
               q_ref, gf_ref, gb_ref, kf_ref, kb_ref, v_ref, ga_ref, bb_ref, u_ref,
               qfull_ref, kfull_ref, vv_ref, sg_ref, *cache_ref):
    x = x_ref[...]
    shift1 = mod_ref[0:1, :]
    scale1 = mod_ref[1:2, :]
    xn = _rms(x, n1_ref[...]) * (1.0 + scale1) + shift1
    xb = xn.astype(BF16)

    def proj(off, width):
        return lambda: _dot(xb, w_ref[:, off:off + width])

    env = {}
    stages = []

    def ep_q(z):
        q_ref[...] = (_silu(z) * (A_DK ** -0.5)).astype(BF16)
    stages.append((proj(0, A_W), ep_q))

    def ep_forget(d, g_ref, k_ref):
        def ep(z):
            lb = lb_ref[d:d + 1, :]
            sp, sn = _sigmoid_pair(z)
            g_ref[...] = jnp.log(jnp.maximum(lb, LB_FLOOR) + (1.0 - lb) * sp) * LOG2E
            k_ref[...] = ((1.0 - lb) * sn - jnp.maximum(LB_FLOOR - lb, 0.0)).astype(BF16)
        return ep
    stages.append((proj(1 * A_W, A_W), ep_forget(0, gf_ref, kf_ref)))
    stages.append((proj(2 * A_W, A_W), ep_forget(1, gb_ref, kb_ref)))

    def ep_v(z):
        v_ref[...] = z.astype(BF16)
    stages.append((proj(3 * A_W, A_W), ep_v))

    def ep_ga(z):
        ga_ref[...] = _silu(z).astype(BF16)
    stages.append((proj(4 * A_W, A_W), ep_ga))

    def ep_bb(z):
        bb_ref[...] = z.astype(BF16)
    stages.append((proj(5 * A_W, B_W), ep_bb))

    def ep_cb(z):
        env["conv_c"] = z
    stages.append((proj(6 * A_W, B_W), ep_cb))

    def ep_hb(z):
        u_ref[...] = (env["conv_c"] * z).astype(BF16)
    stages.append((proj(7 * A_W, B_W), ep_hb))

    def ep_cq(z):
        env["cqn"] = _rms(z, qn_ref[...]).astype(BF16)
    stages.append((proj(OFF_CQ, C_Q_LORA), ep_cq))

    def ep_kv(z):
        ckvn = _rms(z[:, :C_KV_LORA], kvn_ref[...])
        z3 = z[:, C_KV_LORA:]
        if cache_ref:
            cache_ref[0][:, :C_KV_LORA] = ckvn
            cache_ref[0][:, C_KV_LORA:] = z3[:, :C_ROPE]
        krr = _place_rope_key(z3, tabk_ref[...])
        env["krr"] = jnp.concatenate([krr, krr], axis=1)
        env["cb"] = ckvn.astype(BF16)
    stages.append((lambda: _dot(xb, wkvd_ref[...]), ep_kv))

    def gate_stage(j):
        def ep(z):
            sg_ref[:, j * A_W:(j + 1) * A_W] = _sigmoid(z).astype(BF16)
        return (lambda: _dot(xb, wg_ref[:, j * A_W:(j + 1) * A_W]), ep)

    def q_up_stage(h):
        def ep(z):
            tabq = jnp.concatenate([tabq_ref[...]] * 2, axis=1)
            qfull_ref[:, h * LANES:(h + 2) * LANES] = (z * tabq).astype(BF16)
        return (lambda: _dot(env["cqn"], wq_ref[:, h * LANES:(h + 2) * LANES]), ep)

    def k_up_stage(h):
        def ep(z):
            kfull_ref[:, h * LANES:(h + 2) * LANES] = (z + env["krr"]).astype(BF16)
        return (lambda: _dot(env["cb"], wkv_ref[:, h * LANES:(h + 2) * LANES]), ep)

    def ep_vv(z):
        vv_ref[...] = z.astype(BF16)
    v_up_stage = (lambda: _dot(env["cb"], wkv_ref[:, C_HEADS * LANES:]), ep_vv)

    stages = stages[8:] + stages[:8]
    stages += [gate_stage(0), q_up_stage(0), q_up_stage(2), gate_stage(1), q_up_stage(4), q_up_stage(6),
               gate_stage(2), k_up_stage(0), k_up_stage(2), gate_stage(3), k_up_stage(4), k_up_stage(6),
               gate_stage(4), v_up_stage, gate_stage(5)]

    pending = [stages[j][0]() for j in range(IN_LOOKAHEAD)]
    for j, (_, epilogue) in enumerate(stages):
        if j + IN_LOOKAHEAD < len(stages):
            pending.append(stages[j + IN_LOOKAHEAD][0]())
        epilogue(pending.pop(0))


def _in_call(x2d, layer, mod4, mod_row, pw, lb_all, tabq, tabk, want_cache):
    n = x2d.shape[0]
    tm = IN_ROWS
    tab_blocks = tabq.shape[0] // tm

    def row(width):
        return pl.BlockSpec((tm, width), lambda i: (i, 0))

    outs = [
        (A_W, BF16),
        (A_W, F32), (A_W, F32),
        (A_W, BF16), (A_W, BF16),
        (A_W, BF16),
        (A_W, BF16),
        (B_W, BF16),
        (B_W, BF16),
        (C_HEADS * LANES, BF16),
        (C_HEADS * LANES, BF16),
        (C_HEADS * C_V, BF16),
        (N_BRANCH * D_MODEL, BF16),
    ]
    if want_cache:
        outs.append((CACHE_W, F32))
    return pl.pallas_call(
        _in_kernel,
        grid=(n // tm,),
        in_specs=[
            row(D_MODEL),
            _mod_spec(layer, mod_row, tm),
            _layer_spec((1, D_MODEL), layer, 1),
            _layer_spec((D_MODEL, OFF_KV), layer, 1),
            _layer_spec((D_MODEL, KV_EXT), layer, 1),
            _layer_spec((D_MODEL, N_BRANCH * D_MODEL), layer, 1),
            _layer_spec((2, A_W), layer, 1),
            _layer_spec((1, C_Q_LORA), layer, 1),
            _layer_spec((C_Q_LORA, C_HEADS * LANES), layer, 1),
            _layer_spec((1, C_KV_LORA), layer, 1),
            _layer_spec((C_KV_LORA, C_HEADS * LANES + C_HEADS * C_V), layer, 1),
            pl.BlockSpec((tm, LANES), lambda i: (i % tab_blocks, 0)),
            pl.BlockSpec((tm, LANES), lambda i: (i % tab_blocks, 0)),
        ],
        out_specs=[row(w) for w, _ in outs],
        out_shape=[jax.ShapeDtypeStruct((n, w), dt) for w, dt in outs],
        compiler_params=_cparams(("arbitrary",)),
        name="in_proj",
    )(x2d, mod4, pw["norm1"], pw["w_in_a"], pw["w_in_kv"], pw["w_in_g"], lb_all, pw["q_norm"], pw["w_q_up"],
      pw["kv_norm"], pw["w_kv_up"], tabq, tabk)


def _cache_kernel(c_ref, kr_ref, wkv_ref, kfull_ref, vv_ref):
    cb = c_ref[:, :C_KV_LORA].astype(BF16)
    kr = kr_ref[...]
    krr = pltpu.roll(kr, C_NOPE, axis=1) + pltpu.roll(kr, C_NOPE + C_ROPE, axis=1)
    krr = jnp.concatenate([krr, krr], axis=1)
    for h in range(0, C_HEADS, 2):
        kz = _dot(cb, wkv_ref[:, h * LANES:(h + 2) * LANES])
        kfull_ref[:, h * LANES:(h + 2) * LANES] = (kz + krr).astype(BF16)
    vv_ref[...] = _dot(cb, wkv_ref[:, C_HEADS * LANES:]).astype(BF16)


def _cache_call(cache, cache_kr, layer, wkv):
    b, _, p, w = cache.shape
    return pl.pallas_call(
        _cache_kernel,
        grid=(b,),
        in_specs=[
            pl.BlockSpec((None, None, p, w), lambda i: (i, layer, 0, 0)),
            pl.BlockSpec((None, None, p, LANES), lambda i: (i, layer, 0, 0)),
            _layer_spec(wkv.shape[1:], layer, 1, single=False),
        ],
        out_specs=[
            pl.BlockSpec((None, p, C_HEADS * LANES), lambda i: (i, 0, 0)),
            pl.BlockSpec((None, p, C_HEADS * C_V), lambda i: (i, 0, 0)),
        ],
        out_shape=[
            jax.ShapeDtypeStruct((b, p, C_HEADS * LANES), BF16),
            jax.ShapeDtypeStruct((b, p, C_HEADS * C_V), BF16),
        ],
        compiler_params=_cparams(("arbitrary",)),
        name="cache_expand",
    )(cache, cache_kr, wkv)


def _level_matrix(c, rev):
    t = np.arange(c)[:, None]
    s = np.arange(c)[None, :]
    lev = np.floor(np.log2(np.maximum(np.bitwise_xor(t, s), 1))).astype(np.int32)
    lev = np.where(s == t, -1, lev)
    lev = np.where((s < t) if rev else (s > t), -2, lev)
    return lev.astype(np.int32)


def _prefix_matrix(c, rev):
    t = np.arange(c)[:, None]
    s = np.arange(c)[None, :]
    incl = ((s >= t) if rev else (s <= t)).astype(np.float32)
    return np.concatenate([incl, incl], axis=1)


def _level_exponent(b, g, h, rev):
    c, dk = b.shape
    ref = h if rev else h - 1
    if 2 * h > SUBLANES:
        parts = []
        for gs in range(0, c, 2 * h):
            beta = jnp.broadcast_to(b[gs + ref:gs + ref + 1, :], (h, dk))
            lo = b[gs:gs + h, :]
            hi = b[gs + h:gs + 2 * h, :]
            parts += [lo - beta, beta - hi] if rev else [beta - lo, hi - beta]
        return jnp.concatenate(parts, axis=0)
    row = lax.broadcasted_iota(jnp.int32, (SUBLANES, dk), 0)
    upper = (row % (2 * h)) >= h
    keep = jnp.logical_not(upper) if rev else upper
    if h == 1:
        parts = [jnp.where(keep, g[j:j + SUBLANES, :], 0.0) for j in range(0, c, SUBLANES)]
        return jnp.concatenate(parts, axis=0)
    sign = jnp.where(keep, 1.0, -1.0)
    parts = []
    for j in range(0, c, SUBLANES):
        bj = b[j:j + SUBLANES, :]
        beta = jnp.broadcast_to(bj[ref:ref + 1, :], (SUBLANES, dk))
        for gs in range(2 * h, SUBLANES, 2 * h):
            beta = jnp.where(row >= gs, jnp.broadcast_to(bj[gs + ref:gs + ref + 1, :], (SUBLANES, dk)), beta)
        parts.append((bj - beta) * sign)
    return jnp.concatenate(parts, axis=0)


def _hgrn_chunks(streams, lev, pmat):
    c = streams[0][0].shape[0]
    bs = []
    for q, k, g, v, st, rev in streams:
        g_hi = g.astype(BF16)
        g_lo = (g - g_hi.astype(F32)).astype(BF16)
        bs.append(_dot(pmat[rev], jnp.concatenate([g_hi, g_lo], axis=0)))
    qfs = [s[0].astype(F32) for s in streams]
    kfs = [s[1].astype(F32) for s in streams]
    a_s = [jnp.zeros((c, c), F32) for _ in streams]
    for hl in range(int(np.log2(c))):
        ps = []
        for (q, k, g, v, st, rev), b, qf, kf in zip(streams, bs, qfs, kfs):
            x = jnp.exp2(_level_exponent(b, g, 1 << hl, rev))
            ps.append(_dot_nt((qf * x).astype(BF16), (kf * x).astype(BF16)))
        a_s = [jnp.where(lev[s[5]] == hl, p, a) for s, p, a in zip(streams, ps, a_s)]
    ps = [_dot_nt(s[0], s[1]) for s in streams]
    a_s = [jnp.where(lev[s[5]] == -1, p, a) for s, p, a in zip(streams, ps, a_s)]
    outs = []
    for (q, k, g, v, st, rev), b, a, qf in zip(streams, bs, a_s, qfs):
        outs.append(_dot(a.astype(BF16), v) + _dot_nt((qf * jnp.exp2(b)).astype(BF16), st.astype(BF16)))
    new_states = []
    for (q, k, g, v, st, rev), b, kf in zip(streams, bs, kfs):
        tot = b[0:1, :] if rev else b[c - 1:c, :]
        ks = (kf * jnp.exp2(tot - b)).astype(BF16)
        new_states.append(st * jnp.exp2(tot) + _dot_tn(v, ks))
    return outs, new_states


def _hgrn_kernel(*refs, has_init, want_state):
    refs = list(refs)
    (q_ref, gf_ref, gb_ref, kf_ref, kb_ref, v_ref, ga_ref, gn_ref,
     levf_ref, levb_ref, pmf_ref, pmb_ref) = refs[:12]
    pos = 12
    s0_ref = None
    if has_init:
        s0_ref = refs[pos]
        pos += 1
    y_ref = refs[pos]
    pos += 1
    sfin_ref = None
    if want_state:
        sfin_ref = refs[pos]
        pos += 1
    o_scr, st_scr = refs[pos], refs[pos + 1]

    t = q_ref.shape[0]
    c = HGRN_CHUNK
    n = t // c
    hp = q_ref.shape[1] // A_DK
    for d in range(2):
        for h in range(hp):
            st_scr[d, h] = s0_ref[d, h].T if has_init else jnp.zeros((A_DV, A_DK), F32)
    lev = (levf_ref, levb_ref)
    pmat = (pmf_ref, pmb_ref)
    g_refs = (gf_ref, gb_ref)
    k_refs = (kf_ref, kb_ref)

    def body(i, carry):
        streams = []
        rows = []
        for d in range(2):
            ci = i if d == 0 else n - 1 - i
            sl = pl.ds(pl.multiple_of(ci * c, c), c)
            rows.append(sl)
            for h in range(hp):
                cs = slice(h * A_DK, (h + 1) * A_DK)
                streams.append((q_ref[sl, cs], k_refs[d][sl, cs], g_refs[d][sl, cs], v_ref[sl, cs],
                                st_scr[d, h], d))
        outs, new_states = _hgrn_chunks(streams, (lev[0][...], lev[1][...]), (pmat[0][...], pmat[1][...]))
        for idx, (o, st_new) in enumerate(zip(outs, new_states)):
            d, h = divmod(idx, hp)
            st_scr[d, h] = st_new
            o_scr[d, rows[d], h * A_DV:(h + 1) * A_DV] = o
        return carry

    lax.fori_loop(0, n, body, 0)
    for h in range(hp):
        cs = slice(h * A_DV, (h + 1) * A_DV)
        if want_state:
            for d in range(2):
                sfin_ref[d, h] = st_scr[d, h].T
        o = o_scr[0, :, cs] + o_scr[1, :, cs]
        y_ref[:, cs] = (_rms(o, gn_ref[...]) * ga_ref[:, cs]).astype(BF16)


def _hgrn_heads_per_step(seq):
    per_head_row = 2 * (5 * 2 + 2 * 4) * A_DK + 2 * 2 * A_DV + 2 * 4 * A_DV
    hp = A_HEADS
    while hp > 1 and hp * seq * per_head_row > HGRN_VMEM_BUDGET:
        hp //= 2
    return hp


def _hgrn_call(q, gf, gb, kf, kb, v, ga, layer, gnorm, s_init, batch, seq, want_state):
    c = HGRN_CHUNK
    hp = _hgrn_heads_per_step(seq)
    consts =[jnp.asarray(_level_matrix(c, False)), jnp.asarray(_level_matrix(c, True)),
              jnp.asarray(_prefix_matrix(c, False), BF16), jnp.asarray(_prefix_matrix(c, True), BF16)]

    def r3(a):
        return a.reshape(batch, seq, A_W)

    col = pl.BlockSpec((None, seq, hp * A_DK), lambda b, h: (b, 0, h))
    in_specs = [col] * 7 + [_layer_spec((1, A_DV), layer, 2, single=False)]
    in_specs += [pl.BlockSpec(m.shape, lambda b, h: (0, 0)) for m in consts]
    args = [r3(q), r3(gf), r3(gb), r3(kf), r3(kb), r3(v), r3(ga), gnorm] + consts
    has_init = s_init is not None
    if has_init:
        in_specs.append(pl.BlockSpec((None, None, 2, hp, A_DK, A_DV), lambda b, h: (b, layer, 0, h, 0, 0)))
        args.append(s_init)
    out_specs = [col]
    out_shape = [jax.ShapeDtypeStruct((batch, seq, A_W), BF16)]
    if want_state:
        out_specs.append(pl.BlockSpec((None, 2, hp, A_DK, A_DV), lambda b, h: (b, 0, h, 0, 0)))
        out_shape.append(jax.ShapeDtypeStruct((batch, 2, A_HEADS, A_DK, A_DV), F32))
    res = pl.pallas_call(
        functools.partial(_hgrn_kernel, has_init=has_init, want_state=want_state),
        grid=(batch, A_HEADS // hp),
        in_specs=in_specs,
        out_specs=out_specs,
        out_shape=out_shape,
        scratch_shapes=[pltpu.VMEM((2, seq, hp * A_DV), F32), pltpu.VMEM((2, hp, A_DV, A_DK), F32)],
        compiler_params=_cparams(("arbitrary", "arbitrary")),
        name="hgrn",
    )(*args)
    y = res[0].reshape(batch * seq, A_W)
    return y, (res[1] if want_state else None)


def _attn_kernel(*refs, has_cache):
    if has_cache:
        q_ref, k_ref, v_ref, kc_ref, vc_ref, y_ref = refs
    else:
        q_ref, k_ref, v_ref, y_ref = refs
    lane = lax.broadcasted_iota(jnp.int32, (q_ref.shape[0], LANES), 1)

    def scores(h):
        hs = slice(h * LANES, (h + 1) * LANES)
        qh = q_ref[:, hs]
        s1 = _dot_nt(qh, k_ref[:, hs])
        s2 = _dot_nt(qh, kc_ref[:, hs]) if has_cache else None
        return s1, s2

    def attend(h, s1, s2):
        vs = slice((h // 2) * LANES, (h // 2 + 1) * LANES)
        m = jnp.max(s1, axis=-1, keepdims=True)
        if has_cache:
            m = jnp.maximum(m, jnp.max(s2, axis=-1, keepdims=True))
        p1 = jnp.exp2(s1 - m)
        l = jnp.sum(p1, axis=-1, keepdims=True)
        o = _dot(p1.astype(BF16), v_ref[:, vs])
        if has_cache:
            p2 = jnp.exp2(s2 - m)
            l = l + jnp.sum(p2, axis=-1, keepdims=True)
            o = o + _dot(p2.astype(BF16), vc_ref[:, vs])
        return o / l

    pending = scores(0)
    outs = []
    for h in range(C_HEADS):
        nxt = scores(h + 1) if h + 1 < C_HEADS else None
        outs.append(attend(h, *pending))
        pending = nxt
        if h % 2 == 1:
            vs = slice((h // 2) * LANES, (h // 2 + 1) * LANES)
            y_ref[:, vs] = jnp.where(lane < C_V, outs[h - 1], outs[h]).astype(BF16)


def _attn_call(qfull, kfull, vv, kc, vc, batch, seq):
    tq = min(ATTN_Q_ROWS, seq)
    has_cache = kc is not None
    kw = C_HEADS * LANES
    vw = C_HEADS * C_V
    in_specs = [
        pl.BlockSpec((None, tq, kw), lambda b, i: (b, i, 0)),
        pl.BlockSpec((None, seq, kw), lambda b, i: (b, 0, 0)),
        pl.BlockSpec((None, seq, vw), lambda b, i: (b, 0, 0)),
    ]
    args = [qfull.reshape(batch, seq, kw), kfull.reshape(batch, seq, kw), vv.reshape(batch, seq, vw)]
    if has_cache:
        p = kc.shape[1]
        in_specs += [
            pl.BlockSpec((None, p, kw), lambda b, i: (b, 0, 0)),
            pl.BlockSpec((None, p, vw), lambda b, i: (b, 0, 0)),
        ]
        args += [kc, vc]
    y = pl.pallas_call(
        functools.partial(_attn_kernel, has_cache=has_cache),
        grid=(batch, seq // tq),
        in_specs=in_specs,
        out_specs=pl.BlockSpec((None, tq, vw), lambda b, i: (b, i, 0)),
        out_shape=jax.ShapeDtypeStruct((batch, seq, vw), BF16),
        compiler_params=_cparams(("arbitrary", "arbitrary")),
        name="mla_attn",
    )(*args)
    return y.reshape(batch * seq, vw)


def _conv3_rows(u, prev_row, next_row, cw_ref, seq):
    r, w = u.shape
    sub = min(seq, r)
    whole = seq <= r
    i8 = lax.broadcasted_iota(jnp.int32, (SUBLANES, w), 0)
    w0, w1, w2 = cw_ref[0:1, :], cw_ref[1:2, :], cw_ref[2:3, :]
    outs = []
    for s in range(r // sub):
        blk = u[s * sub:(s + 1) * sub, :]
        p = 0.0 if whole else prev_row
        n = 0.0 if whole else next_row
        up = pltpu.roll(blk, 1, axis=0)
        dn = pltpu.roll(blk, sub - 1, axis=0)
        up = jnp.concatenate([jnp.where(i8 == 0, p, up[:SUBLANES, :]), up[SUBLANES:, :]], axis=0)
        dn = jnp.concatenate([dn[:sub - SUBLANES, :], jnp.where(i8 == SUBLANES - 1, n, dn[sub - SUBLANES:, :])],
                             axis=0)
        outs.append(up * w0 + blk * w1 + dn * w2)
    return outs[0] if len(outs) == 1 else jnp.concatenate(outs, axis=0)


def _seq_edges(i, rows, seq):
    first = i * rows
    return first % seq == 0, (first + rows) % seq == 0


def _merge_kernel(x_ref, ya_ref, yc_ref, bb_ref, u_ref, up_ref, un_ref, sg_ref, mod_ref,
                  wa_ref, wb_ref, wc_ref, wout_ref, cw_ref, n2_ref, x1_ref, xn2_ref, *, seq):
    tm = x_ref.shape[0]
    at_start, at_end = _seq_edges(pl.program_id(0), tm, seq)
    prev_row = jnp.where(at_start, 0.0, up_ref[HALO - 1:HALO, :].astype(F32))
    next_row = jnp.where(at_end, 0.0, un_ref[0:1, :].astype(F32))
    conv = _conv3_rows(u_ref[...].astype(F32), prev_row, next_row, cw_ref, seq)
    yb = (bb_ref[...] * conv).astype(BF16)
    half = tm // 2
    hs = []
    for s in range(2):
        rs = slice(s * half, (s + 1) * half)
        pa = _dot(ya_ref[rs, :], wa_ref[...])
        pc = _dot(yc_ref[rs, :], wc_ref[...])
        pb = _dot(yb[rs, :], wb_ref[...])
        hs.append((sg_ref[rs, 0:D_MODEL] * pa + sg_ref[rs, D_MODEL:2 * D_MODEL] * pb
                   + sg_ref[rs, 2 * D_MODEL:3 * D_MODEL] * pc).astype(BF16))
    gate1 = mod_ref[2:3, :]
    shift2 = mod_ref[3:4, :]
    scale2 = mod_ref[4:5, :]
    for s in range(2):
        rs = slice(s * half, (s + 1) * half)
        x1 = x_ref[rs, :] + gate1 * _dot(hs[s], wout_ref[...])
        x1_ref[rs, :] = x1
        xn2_ref[rs, :] = (_rms(x1, n2_ref[...]) * (1.0 + scale2) + shift2).astype(BF16)


def _merge_call(x2d, ya, yc, bb, u, sg, layer, mod4, mod_row, pw, seq):
    n = x2d.shape[0]
    tm = MERGE_ROWS
    nbh = n // HALO
    per = tm // HALO

    def row(width):
        return pl.BlockSpec((tm, width), lambda i: (i, 0))

    def wspec(name):
        return _layer_spec(pw[name].shape[1:], layer, 1)

    return pl.pallas_call(
        functools.partial(_merge_kernel, seq=seq),
        grid=(n // tm,),
        in_specs=[
            row(D_MODEL), row(A_W), row(C_HEADS * C_V), row(B_W), row(B_W),
            pl.BlockSpec((HALO, B_W), lambda i: (jnp.maximum(i * per - 1, 0), 0)),
            pl.BlockSpec((HALO, B_W), lambda i: (jnp.minimum((i + 1) * per, nbh - 1), 0)),
            row(N_BRANCH * D_MODEL),
            _mod_spec(layer, mod_row, tm),
            wspec("w_o_hgrn"), wspec("w_o_conv"), wspec("w_o_mla"), wspec("w_out"), wspec("conv_w"),
            wspec("norm2"),
        ],
        out_specs=[row(D_MODEL), row(D_MODEL)],
        out_shape=[jax.ShapeDtypeStruct((n, D_MODEL), F32), jax.ShapeDtypeStruct((n, D_MODEL), BF16)],
        compiler_params=_cparams(("arbitrary",)),
        name="merge",
    )(x2d, ya, yc, bb, u, u, u, sg, mod4, pw["w_o_hgrn"], pw["w_o_conv"], pw["w_o_mla"], pw["w_out"],
      pw["conv_w"], pw["norm2"])


def _ffn_kernel(xn_ref, xp_ref, xx_ref, x1_ref, mod_ref, wup_ref, cw_ref, wd_ref,
                fn_ref, out_ref, h_scr, lhs_scr, *, seq, final):
    r = xn_ref.shape[0]
    tn = FFN_COLS
    at_start, at_end = _seq_edges(pl.program_id(0), r, seq)
    need_halo = seq > r
    if need_halo:
        lhs_scr[0:HALO, :] = xp_ref[...]
        lhs_scr[HALO:HALO + r, :] = xn_ref[...]
        lhs_scr[HALO + r:, :] = xx_ref[...]
        xn = lhs_scr[...]
    else:
        xn = xn_ref[...]

    def up(off):
        return _dot(xn, wup_ref[:, off:off + tn])

    def conv_part(um, off):
        prev_row = next_row = None
        if need_halo:
            prev_row = jnp.where(at_start, 0.0, um[HALO - 1:HALO, :])
            next_row = jnp.where(at_end, 0.0, um[HALO + r:HALO + r + 1, :])
            um = um[HALO:HALO + r, :]
        return _conv3_rows(um, prev_row, next_row, cw_ref.at[:, off:off + tn], seq)

    offs = [o for j in range(D_FF // tn) for o in (j * tn, D_FF + j * tn)]
    pending = up(offs[0])
    a = None
    for idx, off in enumerate(offs):
        nxt = up(offs[idx + 1]) if idx + 1 < len(offs) else None
        conv = conv_part(pending, off)
        if off < D_FF:
            a = conv
        else:
            j = (off - D_FF) // tn
            h_scr[:, j * tn:(j + 1) * tn] = (_silu(a) * conv).astype(BF16)
        pending = nxt

    gate2 = mod_ref[5:6, :]
    y = x1_ref[...] + gate2 * _dot(h_scr[...], wd_ref[...])
    if final:
        y = _rms(y, fn_ref[...])
    out_ref[...] = y


def _ffn_call(xn2, x1, layer, mod4, mod_row, pw, fnorm, seq, final):
    n = xn2.shape[0]
    r = FFN_ROWS
    nbh = n // HALO
    per = r // HALO
    return pl.pallas_call(
        functools.partial(_ffn_kernel, seq=seq, final=final),
        grid=(n // r,),
        in_specs=[
            pl.BlockSpec((r, D_MODEL), lambda i: (i, 0)),
            pl.BlockSpec((HALO, D_MODEL), lambda i: (jnp.maximum(i * per - 1, 0), 0)),
            pl.BlockSpec((HALO, D_MODEL), lambda i: (jnp.minimum((i + 1) * per, nbh - 1), 0)),
            pl.BlockSpec((r, D_MODEL), lambda i: (i, 0)),
            _mod_spec(layer, mod_row, r),
            _layer_spec(pw["w_up"].shape[1:], layer, 1),
            _layer_spec(pw["ffn_conv_w"].shape[1:], layer, 1),
            _layer_spec(pw["w_down"].shape[1:], layer, 1),
            pl.BlockSpec((1, D_MODEL), lambda i: (0, 0)),
        ],
        out_specs=pl.BlockSpec((r, D_MODEL), lambda i: (i, 0)),
        out_shape=jax.ShapeDtypeStruct((n, D_MODEL), F32),
        scratch_shapes=[pltpu.VMEM((r, D_FF), BF16),
                        pltpu.VMEM((r + 2 * HALO if seq > r else HALO, D_MODEL), BF16)],
        compiler_params=_cparams(("arbitrary",)),
        name="conv_mlp",
    )(xn2, xn2, xn2, x1, mod4, pw["w_up"], pw["ffn_conv_w"], pw["w_down"], fnorm)


def _rope_swap_perm():
    idx = np.arange(C_ROPE).reshape(2, 2, C_ROPE // 4)
    return idx[:, ::-1, :].reshape(-1)


def _split_w_in(w_in):
    depth = w_in.shape[0]
    swap = _rope_swap_perm()
    kv = w_in[:, :, OFF_KV:OFF_GATE].astype(BF16)
    kr = kv[:, :, C_KV_LORA:]
    pad = jnp.zeros((depth, D_MODEL, KV_EXT - CACHE_W - C_ROPE), BF16)
    return (w_in[:, :, :OFF_KV].astype(BF16), jnp.concatenate([kv, kr[:, :, swap], pad], axis=2),
            w_in[:, :, OFF_GATE:].astype(BF16))


def _pack_w_q_up(w):
    depth = w.shape[0]
    swap = _rope_swap_perm()
    w4 = w.astype(BF16).reshape(depth, C_Q_LORA, C_HEADS, C_NOPE + C_ROPE)
    rope = w4[..., C_NOPE:]
    return jnp.concatenate([w4, rope[..., swap]], axis=3).reshape(depth, C_Q_LORA, C_HEADS * LANES)


def _pack_w_kv_up(w):
    depth = w.shape[0]
    w4 = w.astype(BF16).reshape(depth, C_KV_LORA, C_HEADS, C_NOPE + C_V)
    kn = jnp.concatenate([w4[..., :C_NOPE], jnp.zeros((depth, C_KV_LORA, C_HEADS, LANES - C_NOPE), BF16)], axis=3)
    return jnp.concatenate(
        [kn.reshape(depth, C_KV_LORA, C_HEADS * LANES), w4[..., C_NOPE:].reshape(depth, C_KV_LORA, C_HEADS * C_V)],
        axis=2)


def _pad_axis(a, size, axis):
    shape = list(a.shape)
    shape[axis] = size - a.shape[axis]
    return jnp.concatenate([a, jnp.zeros(shape, a.dtype)], axis=axis)


def _rope_tables(seq, rotate):
    scale = (C_NOPE + C_ROPE) ** -0.5 * LOG2E
    n_freq = C_ROPE // 4
    if rotate:
        rows = seq // GRID_W
        r = jnp.repeat(jnp.arange(rows, dtype=F32), GRID_W)
        col = jnp.tile(jnp.arange(GRID_W, dtype=F32), rows)
        freq = ROPE_THETA ** (-jnp.arange(n_freq, dtype=F32) / n_freq)
        ang = jnp.stack([r[:, None] * freq, col[:, None] * freq], axis=1)
        cos, sin = jnp.cos(ang), jnp.sin(ang)
    else:
        cos = jnp.ones((seq, 2, n_freq), F32)
        sin = jnp.zeros((seq, 2, n_freq), F32)
    cos_full = jnp.stack([cos, cos], axis=2).reshape(seq, C_ROPE)
    sin_signed = jnp.stack([-sin, sin], axis=2).reshape(seq, C_ROPE)
    tabq = scale * jnp.concatenate([jnp.ones((seq, C_NOPE), F32), cos_full, sin_signed], axis=1)
    tabk = jnp.concatenate([cos_full, sin_signed, jnp.zeros((seq, LANES - 2 * C_ROPE), F32)], axis=1)
    return tabq, tabk


def _layer(x2d, batch, seq, layer, mod4, mod_row, pw, lb_all, s_init, cache, tabs, want_ctx_outputs,
           final, fnorm):
    tabq, tabk = tabs
    res = _in_call(x2d, layer, mod4, mod_row, pw, lb_all, tabq, tabk, want_ctx_outputs)
    q, gf, gb, kf, kb, v, ga, bb, u, qfull, kfull, vv, sg = res[:13]
    own_cache = res[13].reshape(batch, seq, CACHE_W) if want_ctx_outputs else None
    ya, states = _hgrn_call(q, gf, gb, kf, kb, v, ga, layer, pw["gnorm"], s_init, batch, seq, want_ctx_outputs)
    kc = vc = None
    if cache is not None:
        kc, vc = _cache_call(cache[0], cache[1], layer, pw["w_kv_up"])
    yc = _attn_call(qfull, kfull, vv, kc, vc, batch, seq)
    x1, xn2 = _merge_call(x2d, ya, yc, bb, u, sg, layer, mod4, mod_row, pw, seq)
    x2 = _ffn_call(xn2, x1, layer, mod4, mod_row, pw, fnorm, seq, final)
    return x2, states, own_cache


def kernel(x_prompt, x_sample, state_hgrn, cache_mla, c, c_ctx, w_ada, b_ada, norm1, w_in, hgrn_lb_logits, hgrn_gnorm, w_o_hgrn, conv_w, w_o_conv, mla_q_norm, w_q_up, mla_kv_norm, w_kv_up, w_o_mla, w_out, norm2, w_up, ffn_conv_w, w_down, final_norm):
    depth = w_in.shape[0]
    bp, sp, _ = x_prompt.shape
    bs, ss, _ = x_sample.shape
    assert 1 + bs <= MOD_ROWS
    for rows in (IN_ROWS, MERGE_ROWS, FFN_ROWS):
        assert (bp * sp) % rows == 0 and (bs * ss) % rows == 0
        assert (rows % sp == 0 or sp % rows == 0) and ss % rows == 0

    cvec = _pad_axis(jnp.concatenate([c_ctx[None, :], c], axis=0), MOD_ROWS, 0)
    mod4 = _ada_call(cvec, w_ada, b_ada).reshape(depth, MOD_ROWS, 6, D_MODEL)
    lb_all = _lb_call(hgrn_lb_logits)
    tabs_ctx = _rope_tables(IN_ROWS, rotate=False)
    tabs_lat = _rope_tables(ss, rotate=True)
    fnorm = final_norm.reshape(1, D_MODEL)
    cache = (cache_mla, _pad_axis(cache_mla[..., C_KV_LORA:], LANES, 3))
    w_in_a, w_in_kv, w_in_g = _split_w_in(w_in)

    def vec(a):
        return a.reshape(depth, 1, -1)

    pw = {
        "norm1": vec(norm1), "w_in_a": w_in_a, "w_in_kv": w_in_kv, "w_in_g": w_in_g, "gnorm": vec(hgrn_gnorm),
        "w_o_hgrn": w_o_hgrn.astype(BF16), "conv_w": _pad_axis(conv_w, SUBLANES, 1),
        "w_o_conv": w_o_conv.astype(BF16), "q_norm": vec(mla_q_norm), "w_q_up": _pack_w_q_up(w_q_up),
        "kv_norm": vec(mla_kv_norm), "w_kv_up": _pack_w_kv_up(w_kv_up), "w_o_mla": w_o_mla.astype(BF16),
        "w_out": w_out.astype(BF16), "norm2": vec(norm2), "w_up": w_up.astype(BF16),
        "ffn_conv_w": _pad_axis(ffn_conv_w, SUBLANES, 1), "w_down": w_down.astype(BF16),
    }

    xp = x_prompt.reshape(bp * sp, D_MODEL)
    xs = x_sample.reshape(bs * ss, D_MODEL)
    new_states, new_caches = [], []
    for l in range(depth):
        final = l == depth - 1
        xp, st, kvc = _layer(xp, bp, sp, l, mod4, lambda r: 0, pw, lb_all, None, None, tabs_ctx,
                             True, final, fnorm)
        new_states.append(st)
        new_caches.append(kvc)
        xs, _, _ = _layer(xs, bs, ss, l, mod4, lambda r: 1 + r // ss, pw, lb_all, state_hgrn,
                          cache, tabs_lat, False, final, fnorm)
    y_prompt = xp.reshape(bp, sp, D_MODEL)
    y_sample = xs.reshape(bs, ss, D_MODEL)
    return (y_prompt, y_sample, jnp.stack(new_states, axis=1), jnp.stack(new_caches, axis=1))
```

```python
import functools

import numpy as np
import jax
import jax.numpy as jnp
from jax import lax
from jax.experimental import pallas as pl
from jax.experimental.pallas import tpu as pltpu

F32 = jnp.float32
BF16 = jnp.bfloat16

D_MODEL = 1024
GRID_W = 64
EPS = 1e-6
A_HEADS = 4
A_DK = 128
A_DV = 128
A_W = A_HEADS * A_DK
LB_FLOOR = 1e-30
B_W = 512
C_HEADS = 8
C_NOPE = 64
C_ROPE = 32
C_V = 64
C_Q_LORA = 384
C_KV_LORA = 256
ROPE_THETA = 10000.0
D_FF = 2816
N_BRANCH = 3
CACHE_W = C_KV_LORA + C_ROPE
LOG2E = 1.4426950408889634

LANES = 128
SUBLANES = 8
HALO = 16
VMEM_LIMIT_BYTES = 56 * 1024 * 1024

IN_ROWS = 512
IN_LOOKAHEAD = 1
HGRN_CHUNK = 128
HGRN_CHUNK_GROUP = 2
HGRN_VMEM_BUDGET = 32 * 1024 * 1024
ATTN_Q_ROWS = 512
MERGE_ROWS = 512
FFN_ROWS = 1024
FFN_COLS = 256
MOD_ROWS = 16
ADA_COLS = 1536

OFF_CQ = 8 * A_W
OFF_KV = OFF_CQ + C_Q_LORA
KV_EXT = 3 * LANES
OFF_GATE = OFF_KV + CACHE_W


def _cparams(sem):
    return pltpu.CompilerParams(dimension_semantics=sem, vmem_limit_bytes=VMEM_LIMIT_BYTES)


def _dot(a, b):
    return jnp.dot(a, b, preferred_element_type=F32)


def _dot_nt(a, b):
    return lax.dot_general(a, b, (((1,), (1,)), ((), ())), preferred_element_type=F32)


def _dot_tn(a, b):
    return lax.dot_general(a, b, (((0,), (0,)), ((), ())), preferred_element_type=F32)


def _rms(x, g):
    ms = jnp.mean(x * x, axis=-1, keepdims=True)
    return x * lax.rsqrt(ms + EPS) * g


def _sigmoid_pair(z):
    t = jnp.exp(-jnp.abs(z))
    r = 1.0 / (1.0 + t)
    big, small = r, t * r
    pos = z >= 0
    return jnp.where(pos, big, small), jnp.where(pos, small, big)


def _sigmoid(z):
    return 1.0 / (1.0 + jnp.exp(-z))


def _silu(z):
    return z * _sigmoid(z)


def _layer_spec(shape, layer, grid_rank, single=True):
    nd = len(shape)
    if grid_rank == 1:
        imap = lambda i: (layer,) + (0,) * nd
    else:
        imap = lambda i, j: (layer,) + (0,) * nd
    if single:
        return pl.BlockSpec((None,) + tuple(shape), imap, pipeline_mode=pl.Buffered(1))
    return pl.BlockSpec((None,) + tuple(shape), imap)


def _mod_spec(layer, mod_row, rows, grid_rank=1):
    if grid_rank == 1:
        return pl.BlockSpec((None, None, 6, D_MODEL), lambda i: (layer, mod_row(i * rows), 0, 0))
    return pl.BlockSpec((None, None, 6, D_MODEL), lambda i, j: (layer, mod_row(i * rows), 0, 0))


def _ada_kernel(c_ref, w_ref, b_ref, o_ref):
    s = _silu(c_ref[...])
    o_ref[...] = jnp.dot(s, w_ref[...], preferred_element_type=F32,
                         precision=lax.Precision.HIGHEST) + b_ref[...]


def _ada_call(cvec, w_ada, b_ada):
    depth = w_ada.shape[0]
    ncol = w_ada.shape[2] // ADA_COLS
    return pl.pallas_call(
        _ada_kernel,
        grid=(depth, ncol),
        in_specs=[
            pl.BlockSpec((MOD_ROWS, D_MODEL), lambda l, j: (0, 0)),
            pl.BlockSpec((None, D_MODEL, ADA_COLS), lambda l, j: (l, 0, j)),
            pl.BlockSpec((None, 1, ADA_COLS), lambda l, j: (l, 0, j)),
        ],
        out_specs=pl.BlockSpec((None, MOD_ROWS, ADA_COLS), lambda l, j: (l, 0, j)),
        out_shape=jax.ShapeDtypeStruct((depth, MOD_ROWS, w_ada.shape[2]), F32),
        compiler_params=_cparams(("arbitrary", "arbitrary")),
        name="ada_mod",
    )(cvec, w_ada, b_ada.reshape(depth, 1, -1))


def _lb_kernel(x_ref, o_ref):
    depth = x_ref.shape[0]
    xs = [x_ref[l] for l in range(depth)]
    m = xs[0]
    for l in range(1, depth):
        m = jnp.maximum(m, xs[l])
    es = [jnp.exp(x - m) for x in xs]
    tot = es[0]
    for l in range(1, depth):
        tot = tot + es[l]
    ps = [e / tot for e in es]
    run = jnp.zeros_like(ps[0])
    for l in range(depth):
        run = run + ps[l]
        o_ref[l] = run - ps[0]


def _lb_call(logits):
    return pl.pallas_call(
        _lb_kernel,
        out_shape=jax.ShapeDtypeStruct(logits.shape, F32),
        name="hgrn_lb",
    )(logits)


def _place_rope_key(z3, tabk):
    tmp = z3 * tabk
    rot = tmp + pltpu.roll(tmp, LANES - C_ROPE, axis=1)
    lane = lax.broadcasted_iota(jnp.int32, rot.shape, 1)
    rot = jnp.where(lane < C_ROPE, rot, 0.0)
    return pltpu.roll(rot, C_NOPE, axis=1) + pltpu.roll(rot, C_NOPE + C_ROPE, axis=1)


def _in_kernel(*refs, n_prev, seq):
    (x_ref, mod_ref, n1_ref, w_ref, wkvd_ref, wg_ref, lb_ref, qn_ref, wq_ref, kvn_ref, wkv_ref,
     tabq_ref, tabk_ref) = refs[:13]
    prev_cache_refs = refs[13:13 + n_prev]
    (q_ref, gf_ref, gb_ref, kf_ref, kb_ref, v_ref, ga_ref, bb_ref, u_ref,
     qfull_ref, kfull_ref, vv_ref, sg_ref) = refs[13 + n_prev:26 + n_prev]
    cache_ref = refs[26 + n_prev:]
    x = x_ref[...]
    shift1 = mod_ref[0:1, :]
    scale1 = mod_ref[1:2, :]
    xn = _rms(x, n1_ref[...]) * (1.0 + scale1) + shift1
    xb = xn.astype(BF16)

    def proj(off, width):
        return lambda: _dot(xb, w_ref[:, off:off + width])

    env = {}
    stages = []

    def ep_q(z):
        q_ref[...] = (_silu(z) * (A_DK ** -0.5)).astype(BF16)
    stages.append((proj(0, A_W), ep_q))

    def ep_forget(d, g_ref, k_ref):
        def ep(z):
            lb = lb_ref[d:d + 1, :]
            sp, sn = _sigmoid_pair(z)
            g_ref[...] = jnp.log(jnp.maximum(lb, LB_FLOOR) + (1.0 - lb) * sp) * LOG2E
            k_ref[...] = ((1.0 - lb) * sn - jnp.maximum(LB_FLOOR - lb, 0.0)).astype(BF16)
        return ep
    stages.append((proj(1 * A_W, A_W), ep_forget(0, gf_ref, kf_ref)))
    stages.append((proj(2 * A_W, A_W), ep_forget(1, gb_ref, kb_ref)))

    def ep_v(z):
        v_ref[...] = z.astype(BF16)
    stages.append((proj(3 * A_W, A_W), ep_v))

    def ep_ga(z):
        ga_ref[...] = _silu(z).astype(BF16)
    stages.append((proj(4 * A_W, A_W), ep_ga))

    def ep_bb(z):
        bb_ref[...] = z.astype(BF16)
    stages.append((proj(5 * A_W, B_W), ep_bb))

    def ep_cb(z):
        env["conv_c"] = z
    stages.append((proj(6 * A_W, B_W), ep_cb))

    def ep_hb(z):
        u_ref[...] = (env["conv_c"] * z).astype(BF16)
    stages.append((proj(7 * A_W, B_W), ep_hb))

    def ep_cq(z):
        env["cqn"] = _rms(z, qn_ref[...]).astype(BF16)
    stages.append((proj(OFF_CQ, C_Q_LORA), ep_cq))

    def ep_kv(z):
        ckvn = _rms(z[:, :C_KV_LORA], kvn_ref[...])
        z3 = z[:, C_KV_LORA:]
        if cache_ref and n_prev == 0:
            cache_ref[0][:, :C_KV_LORA] = ckvn
            cache_ref[0][:, C_KV_LORA:] = z3[:, :C_ROPE]
        elif cache_ref:
            for s in range(x_ref.shape[0] // seq):
                rows = slice(s * seq, (s + 1) * seq)
                for l, prev in enumerate(prev_cache_refs):
                    cache_ref[0][s, l] = prev[s]
                cache_ref[0][s, n_prev, :, :C_KV_LORA] = ckvn[rows, :]
                cache_ref[0][s, n_prev, :, C_KV_LORA:] = z3[rows, :C_ROPE]
        krr = _place_rope_key(z3, tabk_ref[...])
        env["krr"] = jnp.concatenate([krr, krr], axis=1)
        env["cb"] = ckvn.astype(BF16)
    stages.append((lambda: _dot(xb, wkvd_ref[...]), ep_kv))

    def gate_stage(j):
        def ep(z):
            sg_ref[:, j * A_W:(j + 1) * A_W] = _sigmoid(z).astype(BF16)
        return (lambda: _dot(xb, wg_ref[:, j * A_W:(j + 1) * A_W]), ep)

    def q_up_stage(h):
        def ep(z):
            tabq = jnp.concatenate([tabq_ref[...]] * 2, axis=1)
            qfull_ref[:, h * LANES:(h + 2) * LANES] = (z * tabq).astype(BF16)
        return (lambda: _dot(env["cqn"], wq_ref[:, h * LANES:(h + 2) * LANES]), ep)

    def k_up_stage(h):
        def ep(z):
            kfull_ref[:, h * LANES:(h + 2) * LANES] = (z + env["krr"]).astype(BF16)
        return (lambda: _dot(env["cb"], wkv_ref[:, h * LANES:(h + 2) * LANES]), ep)

    def ep_vv(z):
        vv_ref[...] = z.astype(BF16)
    v_up_stage = (lambda: _dot(env["cb"], wkv_ref[:, C_HEADS * LANES:]), ep_vv)

    stages = stages[8:] + stages[:8]
    stages += [gate_stage(0), q_up_stage(0), q_up_stage(2), gate_stage(1), q_up_stage(4), q_up_stage(6),
               gate_stage(2), k_up_stage(0), k_up_stage(2), gate_stage(3), k_up_stage(4), k_up_stage(6),
               gate_stage(4), v_up_stage, gate_stage(5)]

    pending = [stages[j][0]() for j in range(IN_LOOKAHEAD)]
    for j, (_, epilogue) in enumerate(stages):
        if j + IN_LOOKAHEAD < len(stages):
            pending.append(stages[j + IN_LOOKAHEAD][0]())
        epilogue(pending.pop(0))


def _in_call(x2d, layer, mod4, mod_row, pw, lb_all, tabq, tabk, want_cache, seq, prev_caches=()):
    n = x2d.shape[0]
    tm = IN_ROWS
    tab_blocks = tabq.shape[0] // tm
    n_prev = len(prev_caches) if want_cache else 0

    def row(width):
        return pl.BlockSpec((tm, width), lambda i: (i, 0))

    outs = [
        (A_W, BF16),
        (A_W, F32), (A_W, F32),
        (A_W, BF16), (A_W, BF16),
        (A_W, BF16),
        (A_W, BF16),
        (B_W, BF16),
        (B_W, BF16),
        (C_HEADS * LANES, BF16),
        (C_HEADS * LANES, BF16),
        (C_HEADS * C_V, BF16),
        (N_BRANCH * D_MODEL, BF16),
    ]
    out_specs = [row(w) for w, _ in outs]
    out_shape = [jax.ShapeDtypeStruct((n, w), dt) for w, dt in outs]
    prev_specs = []
    if want_cache and n_prev == 0:
        out_specs.append(row(CACHE_W))
        out_shape.append(jax.ShapeDtypeStruct((n, CACHE_W), F32))
    elif want_cache:
        per = tm // seq
        prev_specs = [pl.BlockSpec((per, seq, CACHE_W), lambda i: (i, 0, 0)) for _ in prev_caches]
        out_specs.append(pl.BlockSpec((per, n_prev + 1, seq, CACHE_W), lambda i: (i, 0, 0, 0)))
        out_shape.append(jax.ShapeDtypeStruct((n // seq, n_prev + 1, seq, CACHE_W), F32))
    return pl.pallas_call(
        functools.partial(_in_kernel, n_prev=n_prev, seq=seq),
        grid=(n // tm,),
        in_specs=[
            row(D_MODEL),
            _mod_spec(layer, mod_row, tm),
            _layer_spec((1, D_MODEL), layer, 1),
            _layer_spec((D_MODEL, OFF_KV), layer, 1),
            _layer_spec((D_MODEL, KV_EXT), layer, 1),
            _layer_spec((D_MODEL, N_BRANCH * D_MODEL), layer, 1),
            _layer_spec((2, A_W), layer, 1),
            _layer_spec((1, C_Q_LORA), layer, 1),
            _layer_spec((C_Q_LORA, C_HEADS * LANES), layer, 1),
            _layer_spec((1, C_KV_LORA), layer, 1),
            _layer_spec((C_KV_LORA, C_HEADS * LANES + C_HEADS * C_V), layer, 1),
            pl.BlockSpec((tm, LANES), lambda i: (i % tab_blocks, 0)),
            pl.BlockSpec((tm, LANES), lambda i: (i % tab_blocks, 0)),
        ] + prev_specs,
        out_specs=out_specs,
        out_shape=out_shape,
        compiler_params=_cparams(("arbitrary",)),
        name="in_proj",
    )(x2d, mod4, pw["norm1"], pw["w_in_a"], pw["w_in_kv"], pw["w_in_g"], lb_all, pw["q_norm"], pw["w_q_up"],
      pw["kv_norm"], pw["w_kv_up"], tabq, tabk, *(prev_caches if n_prev else ()))


def _cache_kernel(c_ref, kr_ref, wkv_ref, kfull_ref, vv_ref):
    cb = c_ref[:, :C_KV_LORA].astype(BF16)
    kr = kr_ref[...]
    krr = pltpu.roll(kr, C_NOPE, axis=1) + pltpu.roll(kr, C_NOPE + C_ROPE, axis=1)
    krr = jnp.concatenate([krr, krr], axis=1)
    for h in range(0, C_HEADS, 2):
        kz = _dot(cb, wkv_ref[:, h * LANES:(h + 2) * LANES])
        kfull_ref[:, h * LANES:(h + 2) * LANES] = (kz + krr).astype(BF16)
    vv_ref[...] = _dot(cb, wkv_ref[:, C_HEADS * LANES:]).astype(BF16)


def _cache_call(cache, cache_kr, layer, wkv):
    b, _, p, w = cache.shape
    return pl.pallas_call(
        _cache_kernel,
        grid=(b,),
        in_specs=[
            pl.BlockSpec((None, None, p, w), lambda i: (i, layer, 0, 0)),
            pl.BlockSpec((None, None, p, LANES), lambda i: (i, layer, 0, 0)),
            _layer_spec(wkv.shape[1:], layer, 1, single=False),
        ],
        out_specs=[
            pl.BlockSpec((None, p, C_HEADS * LANES), lambda i: (i, 0, 0)),
            pl.BlockSpec((None, p, C_HEADS * C_V), lambda i: (i, 0, 0)),
        ],
        out_shape=[
            jax.ShapeDtypeStruct((b, p, C_HEADS * LANES), BF16),
            jax.ShapeDtypeStruct((b, p, C_HEADS * C_V), BF16),
        ],
        compiler_params=_cparams(("arbitrary",)),
        name="cache_expand",
    )(cache, cache_kr, wkv)


def _level_matrix(c, rev):
    t = np.arange(c)[:, None]
    s = np.arange(c)[None, :]
    lev = np.floor(np.log2(np.maximum(np.bitwise_xor(t, s), 1))).astype(np.int32)
    lev = np.where(s == t, -1, lev)
    lev = np.where((s < t) if rev else (s > t), -2, lev)
    return lev.astype(np.int32)


def _prefix_matrix(c, rev):
    t = np.arange(c)[:, None]
    s = np.arange(c)[None, :]
    incl = ((s >= t) if rev else (s <= t)).astype(np.float32)
    return np.concatenate([incl, incl], axis=1)


def _level_exponent(b, g, h, rev):
    c, dk = b.shape
    ref = h if rev else h - 1
    if 2 * h > SUBLANES:
        parts = []
        for gs in range(0, c, 2 * h):
            beta = jnp.broadcast_to(b[gs + ref:gs + ref + 1, :], (h, dk))
            lo = b[gs:gs + h, :]
            hi = b[gs + h:gs + 2 * h, :]
            parts += [lo - beta, beta - hi] if rev else [beta - lo, hi - beta]
        return jnp.concatenate(parts, axis=0)
    row = lax.broadcasted_iota(jnp.int32, (SUBLANES, dk), 0)
    upper = (row % (2 * h)) >= h
    keep = jnp.logical_not(upper) if rev else upper
    if h == 1:
        parts = [jnp.where(keep, g[j:j + SUBLANES, :], 0.0) for j in range(0, c, SUBLANES)]
        return jnp.concatenate(parts, axis=0)
    sign = jnp.where(keep, 1.0, -1.0)
    parts = []
    for j in range(0, c, SUBLANES):
        bj = b[j:j + SUBLANES, :]
        beta = jnp.broadcast_to(bj[ref:ref + 1, :], (SUBLANES, dk))
        for gs in range(2 * h, SUBLANES, 2 * h):
            beta = jnp.where(row >= gs, jnp.broadcast_to(bj[gs + ref:gs + ref + 1, :], (SUBLANES, dk)), beta)
        parts.append((bj - beta) * sign)
    return jnp.concatenate(parts, axis=0)


def _level_halves(c, h):
    return [(slice(gs, gs + h), slice(gs + h, gs + 2 * h)) for gs in range(0, c, 2 * h)]


def _level_operands(q, k, qf, kf, x, h, rev):
    c = q.shape[0]
    if h < SUBLANES:
        return (qf * x).astype(BF16), (kf * x).astype(BF16)
    packed = h >= HALO
    qparts, kparts = [], []
    for lo, hi in _level_halves(c, h):
        qs, ks = (lo, hi) if rev else (hi, lo)
        q_sc, k_sc = qf[qs, :] * x[qs, :], kf[ks, :] * x[ks, :]
        if packed:
            q_sc, k_sc = q_sc.astype(BF16), k_sc.astype(BF16)
            q_raw, k_raw = q[ks, :], k[qs, :]
        else:
            q_raw, k_raw = qf[ks, :], kf[qs, :]
        qparts += [q_sc, q_raw] if rev else [q_raw, q_sc]
        kparts += [k_raw, k_sc] if rev else [k_sc, k_raw]
    qh, kh = jnp.concatenate(qparts, axis=0), jnp.concatenate(kparts, axis=0)
    return (qh, kh) if packed else (qh.astype(BF16), kh.astype(BF16))


def _level_select(a, p, lev, hl, rev):
    c = a.shape[0]
    h = 1 << hl
    if h < SUBLANES:
        return jnp.where(lev == hl, p, a)
    parts = []
    for lo, hi in _level_halves(c, h):
        qs, other = (lo, hi) if rev else (hi, lo)
        sel = jnp.where(lev[qs, :] == hl, p[qs, :], a[qs, :])
        parts += [sel, a[other, :]] if rev else [a[other, :], sel]
    return jnp.concatenate(parts, axis=0)


def _hgrn_intra(streams, lev, pmat):
    c = streams[0][0].shape[0]
    bs = []
    for q, k, g, v, rev in streams:
        g_hi = g.astype(BF16)
        g_lo = (g - g_hi.astype(F32)).astype(BF16)
        bs.append(_dot(pmat[rev], jnp.concatenate([g_hi, g_lo], axis=0)))
    qfs = [s[0].astype(F32) for s in streams]
    kfs = [s[1].astype(F32) for s in streams]
    a_s = [jnp.zeros((c, c), F32) for _ in streams]
    for hl in range(int(np.log2(c))):
        ps = []
        for (q, k, g, v, rev), b, qf, kf in zip(streams, bs, qfs, kfs):
            x = jnp.exp2(_level_exponent(b, g, 1 << hl, rev))
            ps.append(_dot_nt(*_level_operands(q, k, qf, kf, x, 1 << hl, rev)))
        a_s = [_level_select(a, p, lev[s[4]], hl, s[4]) for s, p, a in zip(streams, ps, a_s)]
    ps = [_dot_nt(s[0], s[1]) for s in streams]
    a_s = [jnp.where(lev[s[4]] == -1, p, a).astype(BF16) for s, p, a in zip(streams, ps, a_s)]
    return bs, a_s, qfs, kfs


def _hgrn_inter(streams, intra, states):
    c = streams[0][0].shape[0]
    bs, a_s, qfs, kfs = intra
    outs = []
    for (q, k, g, v, rev), b, a, qf, st in zip(streams, bs, a_s, qfs, states):
        outs.append(_dot(a, v) + _dot_nt((qf * jnp.exp2(b)).astype(BF16), st.astype(BF16)))
    new_states = []
    for (q, k, g, v, rev), b, kf, st in zip(streams, bs, kfs, states):
        tot = b[0:1, :] if rev else b[c - 1:c, :]
        ks = (kf * jnp.exp2(tot - b)).astype(BF16)
        new_states.append(st * jnp.exp2(tot) + _dot_tn(v, ks))
    return outs, new_states


def _hgrn_kernel(*refs, has_init, want_state, n_prev):
    refs = list(refs)
    (q_ref, gf_ref, gb_ref, kf_ref, kb_ref, v_ref, ga_ref, gn_ref,
     levf_ref, levb_ref, pmf_ref, pmb_ref) = refs[:12]
    pos = 12
    s0_ref = None
    if has_init:
        s0_ref = refs[pos]
        pos += 1
    prev_state_refs = refs[pos:pos + n_prev]
    pos += n_prev
    y_ref = refs[pos]
    pos += 1
    sfin_ref = None
    if want_state:
        sfin_ref = refs[pos]
        pos += 1
    o_scr, st_scr = refs[pos], refs[pos + 1]

    t = q_ref.shape[0]
    c = HGRN_CHUNK
    n = t // c
    hp = q_ref.shape[1] // A_DK
    for d in range(2):
        for h in range(hp):
            st_scr[d, h] = s0_ref[d, h].T if has_init else jnp.zeros((A_DV, A_DK), F32)
    lev = (levf_ref, levb_ref)
    pmat = (pmf_ref, pmb_ref)
    g_refs = (gf_ref, gb_ref)
    k_refs = (kf_ref, kb_ref)

    group = HGRN_CHUNK_GROUP if n % HGRN_CHUNK_GROUP == 0 else 1

    def body(i, carry):
        levs = (lev[0][...], lev[1][...])
        pmats = (pmat[0][...], pmat[1][...])
        steps = []
        for u in range(group):
            streams, rows = [], []
            for d in range(2):
                ci = i * group + u if d == 0 else n - 1 - (i * group + u)
                sl = pl.ds(pl.multiple_of(ci * c, c), c)
                rows.append(sl)
                for h in range(hp):
                    cs = slice(h * A_DK, (h + 1) * A_DK)
                    streams.append((q_ref[sl, cs], k_refs[d][sl, cs], g_refs[d][sl, cs], v_ref[sl, cs], d))
            steps.append((streams, rows))
        all_streams = [s for streams, _ in steps for s in streams]
        bs, a_s, qfs, kfs = _hgrn_intra(all_streams, levs, pmats)
        per = 2 * hp
        states = [st_scr[d, h] for d in range(2) for h in range(hp)]
        for u, (streams, rows) in enumerate(steps):
            part = slice(u * per, (u + 1) * per)
            outs, states = _hgrn_inter(streams, (bs[part], a_s[part], qfs[part], kfs[part]), states)
            for idx, o in enumerate(outs):
                d, h = divmod(idx, hp)
                o_scr[d, rows[d], h * A_DV:(h + 1) * A_DV] = o
        for idx, st_new in enumerate(states):
            d, h = divmod(idx, hp)
            st_scr[d, h] = st_new
        return carry

    lax.fori_loop(0, n // group, body, 0)
    for l, prev in enumerate(prev_state_refs):
        sfin_ref[l] = prev[...]
    for h in range(hp):
        cs = slice(h * A_DV, (h + 1) * A_DV)
        if want_state:
            for d in range(2):
                if n_prev:
                    sfin_ref[n_prev, d, h] = st_scr[d, h].T
                else:
                    sfin_ref[d, h] = st_scr[d, h].T
        o = o_scr[0, :, cs] + o_scr[1, :, cs]
        y_ref[:, cs] = (_rms(o, gn_ref[...]) * ga_ref[:, cs]).astype(BF16)


def _hgrn_heads_per_step(seq):
    per_head_row = 2 * (5 * 2 + 2 * 4) * A_DK + 2 * 2 * A_DV + 2 * 4 * A_DV
    hp = A_HEADS
    while hp > 1 and hp * seq * per_head_row > HGRN_VMEM_BUDGET:
        hp //= 2
    return hp


def _hgrn_call(q, gf, gb, kf, kb, v, ga, layer, gnorm, s_init, batch, seq, want_state, prev_states=()):
    c = HGRN_CHUNK
    n_prev = len(prev_states) if want_state else 0
    hp = _hgrn_heads_per_step(seq)
    consts =[jnp.asarray(_level_matrix(c, False)), jnp.asarray(_level_matrix(c, True)),
              jnp.asarray(_prefix_matrix(c, False), BF16), jnp.asarray(_prefix_matrix(c, True), BF16)]

    def r3(a):
        return a.reshape(batch, seq, A_W)

    col = pl.BlockSpec((None, seq, hp * A_DK), lambda b, h: (b, 0, h))
    in_specs = [col] * 7 + [_layer_spec((1, A_DV), layer, 2, single=False)]
    in_specs += [pl.BlockSpec(m.shape, lambda b, h: (0, 0)) for m in consts]
    args = [r3(q), r3(gf), r3(gb), r3(kf), r3(kb), r3(v), r3(ga), gnorm] + consts
    has_init = s_init is not None
    if has_init:
        in_specs.append(pl.BlockSpec((None, None, 2, hp, A_DK, A_DV), lambda b, h: (b, layer, 0, h, 0, 0)))
        args.append(s_init)
    st_spec = pl.BlockSpec((None, 2, hp, A_DK, A_DV), lambda b, h: (b, 0, h, 0, 0))
    for prev in (prev_states if n_prev else ()):
        in_specs.append(st_spec)
        args.append(prev)
    out_specs = [col]
    out_shape = [jax.ShapeDtypeStruct((batch, seq, A_W), BF16)]
    if want_state and n_prev:
        out_specs.append(pl.BlockSpec((None, n_prev + 1, 2, hp, A_DK, A_DV), lambda b, h: (b, 0, 0, h, 0, 0)))
        out_shape.append(jax.ShapeDtypeStruct((batch, n_prev + 1, 2, A_HEADS, A_DK, A_DV), F32))
    elif want_state:
        out_specs.append(st_spec)
        out_shape.append(jax.ShapeDtypeStruct((batch, 2, A_HEADS, A_DK, A_DV), F32))
    res = pl.pallas_call(
        functools.partial(_hgrn_kernel, has_init=has_init, want_state=want_state, n_prev=n_prev),
        grid=(batch, A_HEADS // hp),
        in_specs=in_specs,
        out_specs=out_specs,
        out_shape=out_shape,
        scratch_shapes=[pltpu.VMEM((2, seq, hp * A_DV), F32), pltpu.VMEM((2, hp, A_DV, A_DK), F32)],
        compiler_params=_cparams(("arbitrary", "arbitrary")),
        name="hgrn",
    )(*args)
    y = res[0].reshape(batch * seq, A_W)
    return y, (res[1] if want_state else None)


def _attn_kernel(*refs, has_cache):
    if has_cache:
        q_ref, k_ref, v_ref, kc_ref, vc_ref, y_ref = refs
    else:
        q_ref, k_ref, v_ref, y_ref = refs
    lane = lax.broadcasted_iota(jnp.int32, (q_ref.shape[0], LANES), 1)

    def scores(h):
        hs = slice(h * LANES, (h + 1) * LANES)
        qh = q_ref[:, hs]
        s1 = _dot_nt(qh, k_ref[:, hs])
        s2 = _dot_nt(qh, kc_ref[:, hs]) if has_cache else None
        return s1, s2

    def attend(h, s1, s2):
        vs = slice((h // 2) * LANES, (h // 2 + 1) * LANES)
        m = jnp.max(s1, axis=-1, keepdims=True)
        if has_cache:
            m = jnp.maximum(m, jnp.max(s2, axis=-1, keepdims=True))
        p1 = jnp.exp2(s1 - m)
        l = jnp.sum(p1, axis=-1, keepdims=True)
        o = _dot(p1.astype(BF16), v_ref[:, vs])
        if has_cache:
            p2 = jnp.exp2(s2 - m)
            l = l + jnp.sum(p2, axis=-1, keepdims=True)
            o = o + _dot(p2.astype(BF16), vc_ref[:, vs])
        return o / l

    pending = scores(0)
    outs = []
    for h in range(C_HEADS):
        nxt = scores(h + 1) if h + 1 < C_HEADS else None
        outs.append(attend(h, *pending))
        pending = nxt
        if h % 2 == 1:
            vs = slice((h // 2) * LANES, (h // 2 + 1) * LANES)
            y_ref[:, vs] = jnp.where(lane < C_V, outs[h - 1], outs[h]).astype(BF16)


def _attn_call(qfull, kfull, vv, kc, vc, batch, seq):
    tq = min(ATTN_Q_ROWS, seq)
    has_cache = kc is not None
    kw = C_HEADS * LANES
    vw = C_HEADS * C_V
    in_specs = [
        pl.BlockSpec((None, tq, kw), lambda b, i: (b, i, 0)),
        pl.BlockSpec((None, seq, kw), lambda b, i: (b, 0, 0)),
        pl.BlockSpec((None, seq, vw), lambda b, i: (b, 0, 0)),
    ]
    args = [qfull.reshape(batch, seq, kw), kfull.reshape(batch, seq, kw), vv.reshape(batch, seq, vw)]
    if has_cache:
        p = kc.shape[1]
        in_specs += [
            pl.BlockSpec((None, p, kw), lambda b, i: (b, 0, 0)),
            pl.BlockSpec((None, p, vw), lambda b, i: (b, 0, 0)),
        ]
        args += [kc, vc]
    y = pl.pallas_call(
        functools.partial(_attn_kernel, has_cache=has_cache),
        grid=(batch, seq // tq),
        in_specs=in_specs,
        out_specs=pl.BlockSpec((None, tq, vw), lambda b, i: (b, i, 0)),
        out_shape=jax.ShapeDtypeStruct((batch, seq, vw), BF16),
        compiler_params=_cparams(("arbitrary", "arbitrary")),
        name="mla_attn",
    )(*args)
    return y.reshape(batch * seq, vw)


def _conv3_rows(u, prev_row, next_row, cw_ref, seq):
    r, w = u.shape
    sub = min(seq, r)
    whole = seq <= r
    i8 = lax.broadcasted_iota(jnp.int32, (SUBLANES, w), 0)
    w0, w1, w2 = cw_ref[0:1, :], cw_ref[1:2, :], cw_ref[2:3, :]
    outs = []
    for s in range(r // sub):
        blk = u[s * sub:(s + 1) * sub, :]
        p = 0.0 if whole else prev_row
        n = 0.0 if whole else next_row
        up = pltpu.roll(blk, 1, axis=0)
        dn = pltpu.roll(blk, sub - 1, axis=0)
        up = jnp.concatenate([jnp.where(i8 == 0, p, up[:SUBLANES, :]), up[SUBLANES:, :]], axis=0)
        dn = jnp.concatenate([dn[:sub - SUBLANES, :], jnp.where(i8 == SUBLANES - 1, n, dn[sub - SUBLANES:, :])],
                             axis=0)
        outs.append(up * w0 + blk * w1 + dn * w2)
    return outs[0] if len(outs) == 1 else jnp.concatenate(outs, axis=0)


def _seq_edges(i, rows, seq):
    first = i * rows
    return first % seq == 0, (first + rows) % seq == 0


def _merge_kernel(x_ref, ya_ref, yc_ref, bb_ref, u_ref, up_ref, un_ref, sg_ref, mod_ref,
                  wa_ref, wb_ref, wc_ref, wout_ref, cw_ref, n2_ref, x1_ref, xn2_ref, *, seq):
    tm = x_ref.shape[0]
    at_start, at_end = _seq_edges(pl.program_id(0), tm, seq)
    prev_row = jnp.where(at_start, 0.0, up_ref[HALO - 1:HALO, :].astype(F32))
    next_row = jnp.where(at_end, 0.0, un_ref[0:1, :].astype(F32))
    conv = _conv3_rows(u_ref[...].astype(F32), prev_row, next_row, cw_ref, seq)
    yb = (bb_ref[...] * conv).astype(BF16)
    half = tm // 2
    hs = []
    for s in range(2):
        rs = slice(s * half, (s + 1) * half)
        pa = _dot(ya_ref[rs, :], wa_ref[...])
        pc = _dot(yc_ref[rs, :], wc_ref[...])
        pb = _dot(yb[rs, :], wb_ref[...])
        hs.append((sg_ref[rs, 0:D_MODEL] * pa + sg_ref[rs, D_MODEL:2 * D_MODEL] * pb
                   + sg_ref[rs, 2 * D_MODEL:3 * D_MODEL] * pc).astype(BF16))
    gate1 = mod_ref[2:3, :]
    shift2 = mod_ref[3:4, :]
    scale2 = mod_ref[4:5, :]
    for s in range(2):
        rs = slice(s * half, (s + 1) * half)
        x1 = x_ref[rs, :] + gate1 * _dot(hs[s], wout_ref[...])
        x1_ref[rs, :] = x1
        xn2_ref[rs, :] = (_rms(x1, n2_ref[...]) * (1.0 + scale2) + shift2).astype(BF16)


def _merge_call(x2d, ya, yc, bb, u, sg, layer, mod4, mod_row, pw, seq):
    n = x2d.shape[0]
    tm = MERGE_ROWS
    nbh = n // HALO
    per = tm // HALO

    def row(width):
        return pl.BlockSpec((tm, width), lambda i: (i, 0))

    def wspec(name):
        return _layer_spec(pw[name].shape[1:], layer, 1)

    return pl.pallas_call(
        functools.partial(_merge_kernel, seq=seq),
        grid=(n // tm,),
        in_specs=[
            row(D_MODEL), row(A_W), row(C_HEADS * C_V), row(B_W), row(B_W),
            pl.BlockSpec((HALO, B_W), lambda i: (jnp.maximum(i * per - 1, 0), 0)),
            pl.BlockSpec((HALO, B_W), lambda i: (jnp.minimum((i + 1) * per, nbh - 1), 0)),
            row(N_BRANCH * D_MODEL),
            _mod_spec(layer, mod_row, tm),
            wspec("w_o_hgrn"), wspec("w_o_conv"), wspec("w_o_mla"), wspec("w_out"), wspec("conv_w"),
            wspec("norm2"),
        ],
        out_specs=[row(D_MODEL), row(D_MODEL)],
        out_shape=[jax.ShapeDtypeStruct((n, D_MODEL), F32), jax.ShapeDtypeStruct((n, D_MODEL), BF16)],
        compiler_params=_cparams(("arbitrary",)),
        name="merge",
    )(x2d, ya, yc, bb, u, u, u, sg, mod4, pw["w_o_hgrn"], pw["w_o_conv"], pw["w_o_mla"], pw["w_out"],
      pw["conv_w"], pw["norm2"])


def _ffn_kernel(xn_ref, xp_ref, xx_ref, x1_ref, mod_ref, wup_ref, cw_ref, wd_ref,
                fn_ref, out_ref, h_scr, lhs_scr, *, seq, final):
    r = xn_ref.shape[0]
    tn = FFN_COLS
    at_start, at_end = _seq_edges(pl.program_id(0), r, seq)
    need_halo = seq > r
    if need_halo:
        lhs_scr[0:HALO, :] = xp_ref[...]
        lhs_scr[HALO:HALO + r, :] = xn_ref[...]
        lhs_scr[HALO + r:, :] = xx_ref[...]
        xn = lhs_scr[...]
    else:
        xn = xn_ref[...]

    def up(off):
        return _dot(xn, wup_ref[:, off:off + tn])

    def conv_part(um, off):
        prev_row = next_row = None
        if need_halo:
            prev_row = jnp.where(at_start, 0.0, um[HALO - 1:HALO, :])
            next_row = jnp.where(at_end, 0.0, um[HALO + r:HALO + r + 1, :])
            um = um[HALO:HALO + r, :]
        return _conv3_rows(um, prev_row, next_row, cw_ref.at[:, off:off + tn], seq)

    offs = [o for j in range(D_FF // tn) for o in (j * tn, D_FF + j * tn)]
    pending = up(offs[0])
    a = None
    for idx, off in enumerate(offs):
        nxt = up(offs[idx + 1]) if idx + 1 < len(offs) else None
        conv = conv_part(pending, off)
        if off < D_FF:
            a = conv
        else:
            j = (off - D_FF) // tn
            h_scr[:, j * tn:(j + 1) * tn] = (_silu(a) * conv).astype(BF16)
        pending = nxt

    gate2 = mod_ref[5:6, :]
    y = x1_ref[...] + gate2 * _dot(h_scr[...], wd_ref[...])
    if final:
        y = _rms(y, fn_ref[...])
    out_ref[...] = y


def _ffn_call(xn2, x1, layer, mod4, mod_row, pw, fnorm, seq, final):
    n = xn2.shape[0]
    r = FFN_ROWS
    nbh = n // HALO
    per = r // HALO
    return pl.pallas_call(
        functools.partial(_ffn_kernel, seq=seq, final=final),
        grid=(n // r,),
        in_specs=[
            pl.BlockSpec((r, D_MODEL), lambda i: (i, 0)),
            pl.BlockSpec((HALO, D_MODEL), lambda i: (jnp.maximum(i * per - 1, 0), 0)),
            pl.BlockSpec((HALO, D_MODEL), lambda i: (jnp.minimum((i + 1) * per, nbh - 1), 0)),
            pl.BlockSpec((r, D_MODEL), lambda i: (i, 0)),
            _mod_spec(layer, mod_row, r),
            _layer_spec(pw["w_up"].shape[1:], layer, 1),
            _layer_spec(pw["ffn_conv_w"].shape[1:], layer, 1),
            _layer_spec(pw["w_down"].shape[1:], layer, 1),
            pl.BlockSpec((1, D_MODEL), lambda i: (0, 0)),
        ],
        out_specs=pl.BlockSpec((r, D_MODEL), lambda i: (i, 0)),
        out_shape=jax.ShapeDtypeStruct((n, D_MODEL), F32),
        scratch_shapes=[pltpu.VMEM((r, D_FF), BF16),
                        pltpu.VMEM((r + 2 * HALO if seq > r else HALO, D_MODEL), BF16)],
        compiler_params=_cparams(("arbitrary",)),
        name="conv_mlp",
    )(xn2, xn2, xn2, x1, mod4, pw["w_up"], pw["ffn_conv_w"], pw["w_down"], fnorm)


def _rope_swap_perm():
    idx = np.arange(C_ROPE).reshape(2, 2, C_ROPE // 4)
    return idx[:, ::-1, :].reshape(-1)


def _split_w_in(w_in):
    depth = w_in.shape[0]
    swap = _rope_swap_perm()
    kv = w_in[:, :, OFF_KV:OFF_GATE].astype(BF16)
    kr = kv[:, :, C_KV_LORA:]
    pad = jnp.zeros((depth, D_MODEL, KV_EXT - CACHE_W - C_ROPE), BF16)
    return (w_in[:, :, :OFF_KV].astype(BF16), jnp.concatenate([kv, kr[:, :, swap], pad], axis=2),
            w_in[:, :, OFF_GATE:].astype(BF16))


def _pack_w_q_up(w):
    depth = w.shape[0]
    swap = _rope_swap_perm()
    w4 = w.astype(BF16).reshape(depth, C_Q_LORA, C_HEADS, C_NOPE + C_ROPE)
    rope = w4[..., C_NOPE:]
    return jnp.concatenate([w4, rope[..., swap]], axis=3).reshape(depth, C_Q_LORA, C_HEADS * LANES)


def _pack_w_kv_up(w):
    depth = w.shape[0]
    w4 = w.astype(BF16).reshape(depth, C_KV_LORA, C_HEADS, C_NOPE + C_V)
    kn = jnp.concatenate([w4[..., :C_NOPE], jnp.zeros((depth, C_KV_LORA, C_HEADS, LANES - C_NOPE), BF16)], axis=3)
    return jnp.concatenate(
        [kn.reshape(depth, C_KV_LORA, C_HEADS * LANES), w4[..., C_NOPE:].reshape(depth, C_KV_LORA, C_HEADS * C_V)],
        axis=2)


def _pad_axis(a, size, axis):
    shape = list(a.shape)
    shape[axis] = size - a.shape[axis]
    return jnp.concatenate([a, jnp.zeros(shape, a.dtype)], axis=axis)


def _rope_tables(seq, rotate):
    scale = (C_NOPE + C_ROPE) ** -0.5 * LOG2E
    n_freq = C_ROPE // 4
    if rotate:
        rows = seq // GRID_W
        r = jnp.repeat(jnp.arange(rows, dtype=F32), GRID_W)
        col = jnp.tile(jnp.arange(GRID_W, dtype=F32), rows)
        freq = ROPE_THETA ** (-jnp.arange(n_freq, dtype=F32) / n_freq)
        ang = jnp.stack([r[:, None] * freq, col[:, None] * freq], axis=1)
        cos, sin = jnp.cos(ang), jnp.sin(ang)
    else:
        cos = jnp.ones((seq, 2, n_freq), F32)
        sin = jnp.zeros((seq, 2, n_freq), F32)
    cos_full = jnp.stack([cos, cos], axis=2).reshape(seq, C_ROPE)
    sin_signed = jnp.stack([-sin, sin], axis=2).reshape(seq, C_ROPE)
    tabq = scale * jnp.concatenate([jnp.ones((seq, C_NOPE), F32), cos_full, sin_signed], axis=1)
    tabk = jnp.concatenate([cos_full, sin_signed, jnp.zeros((seq, LANES - 2 * C_ROPE), F32)], axis=1)
    return tabq, tabk


def _layer(x2d, batch, seq, layer, mod4, mod_row, pw, lb_all, s_init, cache, tabs, want_ctx_outputs,
           final, fnorm, prev_states=(), prev_caches=()):
    tabq, tabk = tabs
    res = _in_call(x2d, layer, mod4, mod_row, pw, lb_all, tabq, tabk, want_ctx_outputs, seq, prev_caches)
    q, gf, gb, kf, kb, v, ga, bb, u, qfull, kfull, vv, sg = res[:13]
    own_cache = None
    if want_ctx_outputs:
        own_cache = res[13] if prev_caches else res[13].reshape(batch, seq, CACHE_W)
    ya, states = _hgrn_call(q, gf, gb, kf, kb, v, ga, layer, pw["gnorm"], s_init, batch, seq, want_ctx_outputs,
                            prev_states)
    kc = vc = None
    if cache is not None:
        kc, vc = _cache_call(cache[0], cache[1], layer, pw["w_kv_up"])
    yc = _attn_call(qfull, kfull, vv, kc, vc, batch, seq)
    x1, xn2 = _merge_call(x2d, ya, yc, bb, u, sg, layer, mod4, mod_row, pw, seq)
    x2 = _ffn_call(xn2, x1, layer, mod4, mod_row, pw, fnorm, seq, final)
    return x2, states, own_cache


def kernel(x_prompt, x_sample, state_hgrn, cache_mla, c, c_ctx, w_ada, b_ada, norm1, w_in, hgrn_lb_logits, hgrn_gnorm, w_o_hgrn, conv_w, w_o_conv, mla_q_norm, w_q_up, mla_kv_norm, w_kv_up, w_o_mla, w_out, norm2, w_up, ffn_conv_w, w_down, final_norm):
    depth = w_in.shape[0]
    bp, sp, _ = x_prompt.shape
    bs, ss, _ = x_sample.shape
    assert 1 + bs <= MOD_ROWS
    for rows in (IN_ROWS, MERGE_ROWS, FFN_ROWS):
        assert (bp * sp) % rows == 0 and (bs * ss) % rows == 0
        assert (rows % sp == 0 or sp % rows == 0) and ss % rows == 0

    cvec = _pad_axis(jnp.concatenate([c_ctx[None, :], c], axis=0), MOD_ROWS, 0)
    mod4 = _ada_call(cvec, w_ada, b_ada).reshape(depth, MOD_ROWS, 6, D_MODEL)
    lb_all = _lb_call(hgrn_lb_logits)
    tabs_ctx = _rope_tables(IN_ROWS, rotate=False)
    tabs_lat = _rope_tables(ss, rotate=True)
    fnorm = final_norm.reshape(1, D_MODEL)
    cache = (cache_mla, _pad_axis(cache_mla[..., C_KV_LORA:], LANES, 3))
    w_in_a, w_in_kv, w_in_g = _split_w_in(w_in)

    def vec(a):
        return a.reshape(depth, 1, -1)

    pw = {
        "norm1": vec(norm1), "w_in_a": w_in_a, "w_in_kv": w_in_kv, "w_in_g": w_in_g, "gnorm": vec(hgrn_gnorm),
        "w_o_hgrn": w_o_hgrn.astype(BF16), "conv_w": _pad_axis(conv_w, SUBLANES, 1),
        "w_o_conv": w_o_conv.astype(BF16), "q_norm": vec(mla_q_norm), "w_q_up": _pack_w_q_up(w_q_up),
        "kv_norm": vec(mla_kv_norm), "w_kv_up": _pack_w_kv_up(w_kv_up), "w_o_mla": w_o_mla.astype(BF16),
        "w_out": w_out.astype(BF16), "norm2": vec(norm2), "w_up": w_up.astype(BF16),
        "ffn_conv_w": _pad_axis(ffn_conv_w, SUBLANES, 1), "w_down": w_down.astype(BF16),
    }

    xp = x_prompt.reshape(bp * sp, D_MODEL)
    xs = x_sample.reshape(bs * ss, D_MODEL)
    stack_in_kernel = depth > 1 and IN_ROWS % sp == 0
    new_states, new_caches = [], []
    for l in range(depth):
        final = l == depth - 1
        prev = (tuple(new_states), tuple(new_caches)) if (final and stack_in_kernel) else ((), ())
        xp, st, kvc = _layer(xp, bp, sp, l, mod4, lambda r: 0, pw, lb_all, None, None, tabs_ctx,
                             True, final, fnorm, *prev)
        new_states.append(st)
        new_caches.append(kvc)
        xs, _, _ = _layer(xs, bs, ss, l, mod4, lambda r: 1 + r // ss, pw, lb_all, state_hgrn,
                          cache, tabs_lat, False, final, fnorm)
    y_prompt = xp.reshape(bp, sp, D_MODEL)
    y_sample = xs.reshape(bs, ss, D_MODEL)
    if stack_in_kernel:
        return (y_prompt, y_sample, new_states[-1], new_caches[-1])
    return (y_prompt, y_sample, jnp.stack(new_states, axis=1), jnp.stack(new_caches, axis=1))
```

```python
import functools
import math

import numpy as np
import jax
import jax.numpy as jnp
from jax import lax
from jax.experimental import pallas as pl
from jax.experimental.pallas import tpu as pltpu

F32 = jnp.float32
BF16 = jnp.bfloat16

D_MODEL = 1024
GRID_W = 64
EPS = 1e-6
A_HEADS = 4
A_DK = 128
A_DV = 128
A_W = A_HEADS * A_DK
LB_FLOOR = 1e-30
B_W = 512
C_HEADS = 8
C_NOPE = 64
C_ROPE = 32
C_V = 64
C_Q_LORA = 384
C_KV_LORA = 256
ROPE_THETA = 10000.0
D_FF = 2816
N_BRANCH = 3
CACHE_W = C_KV_LORA + C_ROPE
LOG2E = 1.4426950408889634

LANES = 128
SUBLANES = 8
HALO = 16
VMEM_LIMIT_BYTES = 56 * 1024 * 1024

IN_ROWS = 512
IN_LOOKAHEAD = 1
HGRN_CHUNK = 128
HGRN_CHUNK_GROUP = 2
HGRN_VMEM_BUDGET = 32 * 1024 * 1024
ATTN_Q_ROWS = 512
MERGE_ROWS = 512
FFN_ROWS = 1024
FFN_COLS = 256
MOD_ROWS = 16
ADA_COLS = 1536

OFF_CQ = 8 * A_W
OFF_KV = OFF_CQ + C_Q_LORA
KV_EXT = 3 * LANES
OFF_GATE = OFF_KV + CACHE_W


def _cparams(sem):
    return pltpu.CompilerParams(dimension_semantics=sem, vmem_limit_bytes=VMEM_LIMIT_BYTES)


def _dot(a, b):
    return jnp.dot(a, b, preferred_element_type=F32)


def _dot_nt(a, b):
    return lax.dot_general(a, b, (((1,), (1,)), ((), ())), preferred_element_type=F32)


def _dot_tn(a, b):
    return lax.dot_general(a, b, (((0,), (0,)), ((), ())), preferred_element_type=F32)


def _rms(x, g):
    ms = jnp.mean(x * x, axis=-1, keepdims=True)
    return x * lax.rsqrt(ms + EPS) * g


def _sigmoid_pair(z):
    t = jnp.exp(-jnp.abs(z))
    r = 1.0 / (1.0 + t)
    big, small = r, t * r
    pos = z >= 0
    return jnp.where(pos, big, small), jnp.where(pos, small, big)


def _sigmoid(z):
    return 1.0 / (1.0 + jnp.exp(-z))


def _silu(z):
    return z * _sigmoid(z)


def _layer_spec(shape, layer, grid_rank, single=True):
    nd = len(shape)
    if grid_rank == 1:
        imap = lambda i: (layer,) + (0,) * nd
    else:
        imap = lambda i, j: (layer,) + (0,) * nd
    if single:
        return pl.BlockSpec((None,) + tuple(shape), imap, pipeline_mode=pl.Buffered(1))
    return pl.BlockSpec((None,) + tuple(shape), imap)


def _mod_spec(layer, mod_row, rows, grid_rank=1):
    if grid_rank == 1:
        return pl.BlockSpec((None, None, 6, D_MODEL), lambda i: (layer, mod_row(i * rows), 0, 0))
    return pl.BlockSpec((None, None, 6, D_MODEL), lambda i, j: (layer, mod_row(i * rows), 0, 0))


def _ada_kernel(c_ref, w_ref, b_ref, o_ref):
    s = _silu(c_ref[...])
    o_ref[...] = jnp.dot(s, w_ref[...], preferred_element_type=F32,
                         precision=lax.Precision.HIGHEST) + b_ref[...]


def _ada_call(cvec, w_ada, b_ada):
    depth = w_ada.shape[0]
    ncol = w_ada.shape[2] // ADA_COLS
    return pl.pallas_call(
        _ada_kernel,
        grid=(depth, ncol),
        in_specs=[
            pl.BlockSpec((MOD_ROWS, D_MODEL), lambda l, j: (0, 0)),
            pl.BlockSpec((None, D_MODEL, ADA_COLS), lambda l, j: (l, 0, j)),
            pl.BlockSpec((None, 1, ADA_COLS), lambda l, j: (l, 0, j)),
        ],
        out_specs=pl.BlockSpec((None, MOD_ROWS, ADA_COLS), lambda l, j: (l, 0, j)),
        out_shape=jax.ShapeDtypeStruct((depth, MOD_ROWS, w_ada.shape[2]), F32),
        compiler_params=_cparams(("arbitrary", "arbitrary")),
        name="ada_mod",
    )(cvec, w_ada, b_ada.reshape(depth, 1, -1))


def _lb_kernel(x_ref, o_ref):
    depth = x_ref.shape[0]
    xs = [x_ref[l] for l in range(depth)]
    m = xs[0]
    for l in range(1, depth):
        m = jnp.maximum(m, xs[l])
    es = [jnp.exp(x - m) for x in xs]
    tot = es[0]
    for l in range(1, depth):
        tot = tot + es[l]
    ps = [e / tot for e in es]
    run = jnp.zeros_like(ps[0])
    for l in range(depth):
        run = run + ps[l]
        o_ref[l] = run - ps[0]


def _lb_call(logits):
    return pl.pallas_call(
        _lb_kernel,
        out_shape=jax.ShapeDtypeStruct(logits.shape, F32),
        name="hgrn_lb",
    )(logits)


def _place_rope_key(z3, tabk):
    tmp = z3 * tabk
    rot = tmp + pltpu.roll(tmp, LANES - C_ROPE, axis=1)
    lane = lax.broadcasted_iota(jnp.int32, rot.shape, 1)
    rot = jnp.where(lane < C_ROPE, rot, 0.0)
    return pltpu.roll(rot, C_NOPE, axis=1) + pltpu.roll(rot, C_NOPE + C_ROPE, axis=1)


def _in_kernel(*refs, n_prev, seq):
    (x_ref, mod_ref, n1_ref, w_ref, wkvd_ref, wg_ref, lb_ref, qn_ref, wq_ref, kvn_ref, wkv_ref,
     tabq_ref, tabk_ref) = refs[:13]
    prev_cache_refs = refs[13:13 + n_prev]
    (q_ref, gf_ref, gb_ref, kf_ref, kb_ref, v_ref, ga_ref, bb_ref, u_ref,
     qfull_ref, kfull_ref, vv_ref, sg_ref) = refs[13 + n_prev:26 + n_prev]
    cache_ref = refs[26 + n_prev:]
    x = x_ref[...]
    shift1 = mod_ref[0:1, :]
    scale1 = mod_ref[1:2, :]
    xn = _rms(x, n1_ref[...]) * (1.0 + scale1) + shift1
    xb = xn.astype(BF16)

    def proj(off, width):
        return lambda: _dot(xb, w_ref[:, off:off + width])

    env = {}
    stages = []

    def ep_q(z):
        q_ref[...] = (_silu(z) * (A_DK ** -0.5)).astype(BF16)
    stages.append((proj(0, A_W), ep_q))

    def ep_forget(d, g_ref, k_ref):
        def ep(z):
            lb = lb_ref[d:d + 1, :]
            sp, sn = _sigmoid_pair(z)
            g_ref[...] = jnp.log(jnp.maximum(lb, LB_FLOOR) + (1.0 - lb) * sp) * LOG2E
            k_ref[...] = ((1.0 - lb) * sn - jnp.maximum(LB_FLOOR - lb, 0.0)).astype(BF16)
        return ep
    stages.append((proj(1 * A_W, A_W), ep_forget(0, gf_ref, kf_ref)))
    stages.append((proj(2 * A_W, A_W), ep_forget(1, gb_ref, kb_ref)))

    def ep_v(z):
        v_ref[...] = z.astype(BF16)
    stages.append((proj(3 * A_W, A_W), ep_v))

    def ep_ga(z):
        ga_ref[...] = _silu(z).astype(BF16)
    stages.append((proj(4 * A_W, A_W), ep_ga))

    def ep_bb(z):
        bb_ref[...] = z.astype(BF16)
    stages.append((proj(5 * A_W, B_W), ep_bb))

    def ep_cb(z):
        env["conv_c"] = z
    stages.append((proj(6 * A_W, B_W), ep_cb))

    def ep_hb(z):
        u_ref[...] = (env["conv_c"] * z).astype(BF16)
    stages.append((proj(7 * A_W, B_W), ep_hb))

    def ep_cq(z):
        env["cqn"] = _rms(z, qn_ref[...]).astype(BF16)
    stages.append((proj(OFF_CQ, C_Q_LORA), ep_cq))

    def ep_kv(z):
        ckvn = _rms(z[:, :C_KV_LORA], kvn_ref[...])
        z3 = z[:, C_KV_LORA:]
        if cache_ref and n_prev == 0:
            cache_ref[0][:, :C_KV_LORA] = ckvn
            cache_ref[0][:, C_KV_LORA:] = z3[:, :C_ROPE]
        elif cache_ref:
            for s in range(x_ref.shape[0] // seq):
                rows = slice(s * seq, (s + 1) * seq)
                for l, prev in enumerate(prev_cache_refs):
                    cache_ref[0][s, l] = prev[s]
                cache_ref[0][s, n_prev, :, :C_KV_LORA] = ckvn[rows, :]
                cache_ref[0][s, n_prev, :, C_KV_LORA:] = z3[rows, :C_ROPE]
        krr = _place_rope_key(z3, tabk_ref[...])
        env["krr"] = jnp.concatenate([krr, krr], axis=1)
        env["cb"] = ckvn.astype(BF16)
    stages.append((lambda: _dot(xb, wkvd_ref[...]), ep_kv))

    def gate_stage(j):
        def ep(z):
            sg_ref[:, j * A_W:(j + 1) * A_W] = _sigmoid(z).astype(BF16)
        return (lambda: _dot(xb, wg_ref[:, j * A_W:(j + 1) * A_W]), ep)

    def q_up_stage(h):
        def ep(z):
            tabq = jnp.concatenate([tabq_ref[...]] * 2, axis=1)
            qfull_ref[:, h * LANES:(h + 2) * LANES] = (z * tabq).astype(BF16)
        return (lambda: _dot(env["cqn"], wq_ref[:, h * LANES:(h + 2) * LANES]), ep)

    def k_up_stage(h):
        def ep(z):
            kfull_ref[:, h * LANES:(h + 2) * LANES] = (z + env["krr"]).astype(BF16)
        return (lambda: _dot(env["cb"], wkv_ref[:, h * LANES:(h + 2) * LANES]), ep)

    def ep_vv(z):
        vv_ref[...] = z.astype(BF16)
    v_up_stage = (lambda: _dot(env["cb"], wkv_ref[:, C_HEADS * LANES:]), ep_vv)

    stages = stages[8:] + stages[:8]
    stages += [gate_stage(0), q_up_stage(0), q_up_stage(2), gate_stage(1), q_up_stage(4), q_up_stage(6),
               gate_stage(2), k_up_stage(0), k_up_stage(2), gate_stage(3), k_up_stage(4), k_up_stage(6),
               gate_stage(4), v_up_stage, gate_stage(5)]

    pending = [stages[j][0]() for j in range(IN_LOOKAHEAD)]
    for j, (_, epilogue) in enumerate(stages):
        if j + IN_LOOKAHEAD < len(stages):
            pending.append(stages[j + IN_LOOKAHEAD][0]())
        epilogue(pending.pop(0))


def _in_call(x2d, layer, mod4, mod_row, pw, lb_all, tabq, tabk, want_cache, seq, prev_caches=()):
    n = x2d.shape[0]
    tm = IN_ROWS
    tab_blocks = tabq.shape[0] // tm
    n_prev = len(prev_caches) if want_cache else 0

    def row(width):
        return pl.BlockSpec((tm, width), lambda i: (i, 0))

    outs = [
        (A_W, BF16),
        (A_W, F32), (A_W, F32),
        (A_W, BF16), (A_W, BF16),
        (A_W, BF16),
        (A_W, BF16),
        (B_W, BF16),
        (B_W, BF16),
        (C_HEADS * LANES, BF16),
        (C_HEADS * LANES, BF16),
        (C_HEADS * C_V, BF16),
        (N_BRANCH * D_MODEL, BF16),
    ]
    out_specs = [row(w) for w, _ in outs]
    out_shape = [jax.ShapeDtypeStruct((n, w), dt) for w, dt in outs]
    prev_specs = []
    if want_cache and n_prev == 0:
        out_specs.append(row(CACHE_W))
        out_shape.append(jax.ShapeDtypeStruct((n, CACHE_W), F32))
    elif want_cache:
        per = tm // seq
        prev_specs = [pl.BlockSpec((per, seq, CACHE_W), lambda i: (i, 0, 0)) for _ in prev_caches]
        out_specs.append(pl.BlockSpec((per, n_prev + 1, seq, CACHE_W), lambda i: (i, 0, 0, 0)))
        out_shape.append(jax.ShapeDtypeStruct((n // seq, n_prev + 1, seq, CACHE_W), F32))
    return pl.pallas_call(
        functools.partial(_in_kernel, n_prev=n_prev, seq=seq),
        grid=(n // tm,),
        in_specs=[
            row(D_MODEL),
            _mod_spec(layer, mod_row, tm),
            _layer_spec((1, D_MODEL), layer, 1),
            _layer_spec((D_MODEL, OFF_KV), layer, 1),
            _layer_spec((D_MODEL, KV_EXT), layer, 1),
            _layer_spec((D_MODEL, N_BRANCH * D_MODEL), layer, 1),
            _layer_spec((2, A_W), layer, 1),
            _layer_spec((1, C_Q_LORA), layer, 1),
            _layer_spec((C_Q_LORA, C_HEADS * LANES), layer, 1),
            _layer_spec((1, C_KV_LORA), layer, 1),
            _layer_spec((C_KV_LORA, C_HEADS * LANES + C_HEADS * C_V), layer, 1),
            pl.BlockSpec((tm, LANES), lambda i: (i % tab_blocks, 0)),
            pl.BlockSpec((tm, LANES), lambda i: (i % tab_blocks, 0)),
        ] + prev_specs,
        out_specs=out_specs,
        out_shape=out_shape,
        compiler_params=_cparams(("arbitrary",)),
        name="in_proj",
    )(x2d, mod4, pw["norm1"], pw["w_in_a"], pw["w_in_kv"], pw["w_in_g"], lb_all, pw["q_norm"], pw["w_q_up"],
      pw["kv_norm"], pw["w_kv_up"], tabq, tabk, *(prev_caches if n_prev else ()))


def _cache_kernel(c_ref, kr_ref, wkv_ref, kfull_ref, vv_ref):
    cb = c_ref[:, :C_KV_LORA].astype(BF16)
    kr = kr_ref[...]
    krr = pltpu.roll(kr, C_NOPE, axis=1) + pltpu.roll(kr, C_NOPE + C_ROPE, axis=1)
    krr = jnp.concatenate([krr, krr], axis=1)
    for h in range(0, C_HEADS, 2):
        kz = _dot(cb, wkv_ref[:, h * LANES:(h + 2) * LANES])
        kfull_ref[:, h * LANES:(h + 2) * LANES] = (kz + krr).astype(BF16)
    vv_ref[...] = _dot(cb, wkv_ref[:, C_HEADS * LANES:]).astype(BF16)


def _cache_call(cache, cache_kr, layer, wkv):
    b, _, p, w = cache.shape
    return pl.pallas_call(
        _cache_kernel,
        grid=(b,),
        in_specs=[
            pl.BlockSpec((None, None, p, w), lambda i: (i, layer, 0, 0)),
            pl.BlockSpec((None, None, p, LANES), lambda i: (i, layer, 0, 0)),
            _layer_spec(wkv.shape[1:], layer, 1, single=False),
        ],
        out_specs=[
            pl.BlockSpec((None, p, C_HEADS * LANES), lambda i: (i, 0, 0)),
            pl.BlockSpec((None, p, C_HEADS * C_V), lambda i: (i, 0, 0)),
        ],
        out_shape=[
            jax.ShapeDtypeStruct((b, p, C_HEADS * LANES), BF16),
            jax.ShapeDtypeStruct((b, p, C_HEADS * C_V), BF16),
        ],
        compiler_params=_cparams(("arbitrary",)),
        name="cache_expand",
    )(cache, cache_kr, wkv)


def _level_matrix(c, rev):
    t = np.arange(c)[:, None]
    s = np.arange(c)[None, :]
    lev = np.floor(np.log2(np.maximum(np.bitwise_xor(t, s), 1))).astype(np.int32)
    lev = np.where(s == t, -1, lev)
    lev = np.where((s < t) if rev else (s > t), -2, lev)
    return lev.astype(np.int32)


def _prefix_matrix(c, rev):
    t = np.arange(c)[:, None]
    s = np.arange(c)[None, :]
    incl = ((s >= t) if rev else (s <= t)).astype(np.float32)
    return np.concatenate([incl, incl], axis=1)


def _level_exponent(b, g, h, rev):
    c, dk = b.shape
    ref = h if rev else h - 1
    if 2 * h > SUBLANES:
        parts = []
        for gs in range(0, c, 2 * h):
            beta = jnp.broadcast_to(b[gs + ref:gs + ref + 1, :], (h, dk))
            lo = b[gs:gs + h, :]
            hi = b[gs + h:gs + 2 * h, :]
            parts += [lo - beta, beta - hi] if rev else [beta - lo, hi - beta]
        return jnp.concatenate(parts, axis=0)
    row = lax.broadcasted_iota(jnp.int32, (SUBLANES, dk), 0)
    upper = (row % (2 * h)) >= h
    keep = jnp.logical_not(upper) if rev else upper
    if h == 1:
        parts = [jnp.where(keep, g[j:j + SUBLANES, :], 0.0) for j in range(0, c, SUBLANES)]
        return jnp.concatenate(parts, axis=0)
    sign = jnp.where(keep, 1.0, -1.0)
    parts = []
    for j in range(0, c, SUBLANES):
        bj = b[j:j + SUBLANES, :]
        beta = jnp.broadcast_to(bj[ref:ref + 1, :], (SUBLANES, dk))
        for gs in range(2 * h, SUBLANES, 2 * h):
            beta = jnp.where(row >= gs, jnp.broadcast_to(bj[gs + ref:gs + ref + 1, :], (SUBLANES, dk)), beta)
        parts.append((bj - beta) * sign)
    return jnp.concatenate(parts, axis=0)


def _level_halves(c, h):
    return [(slice(gs, gs + h), slice(gs + h, gs + 2 * h)) for gs in range(0, c, 2 * h)]


def _level_operands(q, k, qf, kf, x, h, rev):
    c = q.shape[0]
    if h < SUBLANES:
        return (qf * x).astype(BF16), (kf * x).astype(BF16)
    packed = h >= HALO
    qparts, kparts = [], []
    for lo, hi in _level_halves(c, h):
        qs, ks = (lo, hi) if rev else (hi, lo)
        q_sc, k_sc = qf[qs, :] * x[qs, :], kf[ks, :] * x[ks, :]
        if packed:
            q_sc, k_sc = q_sc.astype(BF16), k_sc.astype(BF16)
            q_raw, k_raw = q[ks, :], k[qs, :]
        else:
            q_raw, k_raw = qf[ks, :], kf[qs, :]
        qparts += [q_sc, q_raw] if rev else [q_raw, q_sc]
        kparts += [k_raw, k_sc] if rev else [k_sc, k_raw]
    qh, kh = jnp.concatenate(qparts, axis=0), jnp.concatenate(kparts, axis=0)
    return (qh, kh) if packed else (qh.astype(BF16), kh.astype(BF16))


def _level_select(a, p, lev, hl, rev):
    c = a.shape[0]
    h = 1 << hl
    if h < SUBLANES:
        return jnp.where(lev == hl, p, a)
    parts = []
    for lo, hi in _level_halves(c, h):
        qs, other = (lo, hi) if rev else (hi, lo)
        sel = jnp.where(lev[qs, :] == hl, p[qs, :], a[qs, :])
        parts += [sel, a[other, :]] if rev else [a[other, :], sel]
    return jnp.concatenate(parts, axis=0)


def _hgrn_intra(streams, lev, pmat):
    c = streams[0][0].shape[0]
    bs = []
    for q, k, g, v, rev in streams:
        g_hi = g.astype(BF16)
        g_lo = (g - g_hi.astype(F32)).astype(BF16)
        bs.append(_dot(pmat[rev], jnp.concatenate([g_hi, g_lo], axis=0)))
    qfs = [s[0].astype(F32) for s in streams]
    kfs = [s[1].astype(F32) for s in streams]
    a_s = [jnp.zeros((c, c), F32) for _ in streams]
    for hl in range(int(np.log2(c))):
        ps = []
        for (q, k, g, v, rev), b, qf, kf in zip(streams, bs, qfs, kfs):
            x = jnp.exp2(_level_exponent(b, g, 1 << hl, rev))
            ps.append(_dot_nt(*_level_operands(q, k, qf, kf, x, 1 << hl, rev)))
        a_s = [_level_select(a, p, lev[s[4]], hl, s[4]) for s, p, a in zip(streams, ps, a_s)]
    ps = [_dot_nt(s[0], s[1]) for s in streams]
    a_s = [jnp.where(lev[s[4]] == -1, p, a).astype(BF16) for s, p, a in zip(streams, ps, a_s)]
    return bs, a_s, qfs, kfs


def _hgrn_inter(streams, intra, states):
    c = streams[0][0].shape[0]
    bs, a_s, qfs, kfs = intra
    outs = []
    for (q, k, g, v, rev), b, a, qf, st in zip(streams, bs, a_s, qfs, states):
        outs.append(_dot(a, v) + _dot_nt((qf * jnp.exp2(b)).astype(BF16), st.astype(BF16)))
    new_states = []
    for (q, k, g, v, rev), b, kf, st in zip(streams, bs, kfs, states):
        tot = b[0:1, :] if rev else b[c - 1:c, :]
        ks = (kf * jnp.exp2(tot - b)).astype(BF16)
        new_states.append(st * jnp.exp2(tot) + _dot_tn(v, ks))
    return outs, new_states


def _hgrn_kernel(*refs, has_init, want_state, n_prev):
    refs = list(refs)
    (q_ref, gf_ref, gb_ref, kf_ref, kb_ref, v_ref, ga_ref, gn_ref,
     levf_ref, levb_ref, pmf_ref, pmb_ref) = refs[:12]
    pos = 12
    s0_ref = None
    if has_init:
        s0_ref = refs[pos]
        pos += 1
    prev_state_refs = refs[pos:pos + n_prev]
    pos += n_prev
    y_ref = refs[pos]
    pos += 1
    sfin_ref = None
    if want_state:
        sfin_ref = refs[pos]
        pos += 1
    o_scr, st_scr = refs[pos], refs[pos + 1]

    t = q_ref.shape[0]
    c = HGRN_CHUNK
    n = t // c
    hp = q_ref.shape[1] // A_DK
    for d in range(2):
        for h in range(hp):
            st_scr[d, h] = s0_ref[d, h].T if has_init else jnp.zeros((A_DV, A_DK), F32)
    lev = (levf_ref, levb_ref)
    pmat = (pmf_ref, pmb_ref)
    g_refs = (gf_ref, gb_ref)
    k_refs = (kf_ref, kb_ref)

    group = math.gcd(n, HGRN_CHUNK_GROUP)

    def body(i, carry):
        levs = (lev[0][...], lev[1][...])
        pmats = (pmat[0][...], pmat[1][...])
        steps = []
        for u in range(group):
            streams, rows = [], []
            for d in range(2):
                ci = i * group + u if d == 0 else n - 1 - (i * group + u)
                sl = pl.ds(pl.multiple_of(ci * c, c), c)
                rows.append(sl)
                for h in range(hp):
                    cs = slice(h * A_DK, (h + 1) * A_DK)
                    streams.append((q_ref[sl, cs], k_refs[d][sl, cs], g_refs[d][sl, cs], v_ref[sl, cs], d))
            steps.append((streams, rows))
        all_streams = [s for streams, _ in steps for s in streams]
        bs, a_s, qfs, kfs = _hgrn_intra(all_streams, levs, pmats)
        per = 2 * hp
        states = [st_scr[d, h] for d in range(2) for h in range(hp)]
        for u, (streams, rows) in enumerate(steps):
            part = slice(u * per, (u + 1) * per)
            outs, states = _hgrn_inter(streams, (bs[part], a_s[part], qfs[part], kfs[part]), states)
            for idx, o in enumerate(outs):
                d, h = divmod(idx, hp)
                o_scr[d, rows[d], h * A_DV:(h + 1) * A_DV] = o
        for idx, st_new in enumerate(states):
            d, h = divmod(idx, hp)
            st_scr[d, h] = st_new
        return carry

    lax.fori_loop(0, n // group, body, 0)
    for l, prev in enumerate(prev_state_refs):
        sfin_ref[l] = prev[...]
    for h in range(hp):
        cs = slice(h * A_DV, (h + 1) * A_DV)
        if want_state:
            for d in range(2):
                if n_prev:
                    sfin_ref[n_prev, d, h] = st_scr[d, h].T
                else:
                    sfin_ref[d, h] = st_scr[d, h].T
        o = o_scr[0, :, cs] + o_scr[1, :, cs]
        y_ref[:, cs] = (_rms(o, gn_ref[...]) * ga_ref[:, cs]).astype(BF16)


def _hgrn_heads_per_step(seq):
    per_head_row = 2 * (5 * 2 + 2 * 4) * A_DK + 2 * 2 * A_DV + 2 * 4 * A_DV
    hp = A_HEADS
    while hp > 1 and hp * seq * per_head_row > HGRN_VMEM_BUDGET:
        hp //= 2
    return hp


def _hgrn_call(q, gf, gb, kf, kb, v, ga, layer, gnorm, s_init, batch, seq, want_state, prev_states=()):
    c = HGRN_CHUNK
    n_prev = len(prev_states) if want_state else 0
    hp = _hgrn_heads_per_step(seq)
    consts =[jnp.asarray(_level_matrix(c, False)), jnp.asarray(_level_matrix(c, True)),
              jnp.asarray(_prefix_matrix(c, False), BF16), jnp.asarray(_prefix_matrix(c, True), BF16)]

    def r3(a):
        return a.reshape(batch, seq, A_W)

    col = pl.BlockSpec((None, seq, hp * A_DK), lambda b, h: (b, 0, h))
    in_specs = [col] * 7 + [_layer_spec((1, A_DV), layer, 2, single=False)]
    in_specs += [pl.BlockSpec(m.shape, lambda b, h: (0, 0)) for m in consts]
    args = [r3(q), r3(gf), r3(gb), r3(kf), r3(kb), r3(v), r3(ga), gnorm] + consts
    has_init = s_init is not None
    if has_init:
        in_specs.append(pl.BlockSpec((None, None, 2, hp, A_DK, A_DV), lambda b, h: (b, layer, 0, h, 0, 0)))
        args.append(s_init)
    st_spec = pl.BlockSpec((None, 2, hp, A_DK, A_DV), lambda b, h: (b, 0, h, 0, 0))
    for prev in (prev_states if n_prev else ()):
        in_specs.append(st_spec)
        args.append(prev)
    out_specs = [col]
    out_shape = [jax.ShapeDtypeStruct((batch, seq, A_W), BF16)]
    if want_state and n_prev:
        out_specs.append(pl.BlockSpec((None, n_prev + 1, 2, hp, A_DK, A_DV), lambda b, h: (b, 0, 0, h, 0, 0)))
        out_shape.append(jax.ShapeDtypeStruct((batch, n_prev + 1, 2, A_HEADS, A_DK, A_DV), F32))
    elif want_state:
        out_specs.append(st_spec)
        out_shape.append(jax.ShapeDtypeStruct((batch, 2, A_HEADS, A_DK, A_DV), F32))
    res = pl.pallas_call(
        functools.partial(_hgrn_kernel, has_init=has_init, want_state=want_state, n_prev=n_prev),
        grid=(batch, A_HEADS // hp),
        in_specs=in_specs,
        out_specs=out_specs,
        out_shape=out_shape,
        scratch_shapes=[pltpu.VMEM((2, seq, hp * A_DV), F32), pltpu.VMEM((2, hp, A_DV, A_DK), F32)],
        compiler_params=_cparams(("arbitrary", "arbitrary")),
        name="hgrn",
    )(*args)
    y = res[0].reshape(batch * seq, A_W)
    return y, (res[1] if want_state else None)


def _attn_kernel(*refs, has_cache):
    if has_cache:
        q_ref, k_ref, v_ref, kc_ref, vc_ref, y_ref = refs
    else:
        q_ref, k_ref, v_ref, y_ref = refs
    lane = lax.broadcasted_iota(jnp.int32, (q_ref.shape[0], LANES), 1)

    def scores(h):
        hs = slice(h * LANES, (h + 1) * LANES)
        qh = q_ref[:, hs]
        s1 = _dot_nt(qh, k_ref[:, hs])
        s2 = _dot_nt(qh, kc_ref[:, hs]) if has_cache else None
        return s1, s2

    def attend(h, s1, s2):
        vs = slice((h // 2) * LANES, (h // 2 + 1) * LANES)
        m = jnp.max(s1, axis=-1, keepdims=True)
        if has_cache:
            m = jnp.maximum(m, jnp.max(s2, axis=-1, keepdims=True))
        p1 = jnp.exp2(s1 - m)
        l = jnp.sum(p1, axis=-1, keepdims=True)
        o = _dot(p1.astype(BF16), v_ref[:, vs])
        if has_cache:
            p2 = jnp.exp2(s2 - m)
            l = l + jnp.sum(p2, axis=-1, keepdims=True)
            o = o + _dot(p2.astype(BF16), vc_ref[:, vs])
        return o / l

    pending = scores(0)
    outs = []
    for h in range(C_HEADS):
        nxt = scores(h + 1) if h + 1 < C_HEADS else None
        outs.append(attend(h, *pending))
        pending = nxt
        if h % 2 == 1:
            vs = slice((h // 2) * LANES, (h // 2 + 1) * LANES)
            y_ref[:, vs] = jnp.where(lane < C_V, outs[h - 1], outs[h]).astype(BF16)


def _attn_call(qfull, kfull, vv, kc, vc, batch, seq):
    tq = min(ATTN_Q_ROWS, seq)
    has_cache = kc is not None
    kw = C_HEADS * LANES
    vw = C_HEADS * C_V
    in_specs = [
        pl.BlockSpec((None, tq, kw), lambda b, i: (b, i, 0)),
        pl.BlockSpec((None, seq, kw), lambda b, i: (b, 0, 0)),
        pl.BlockSpec((None, seq, vw), lambda b, i: (b, 0, 0)),
    ]
    args = [qfull.reshape(batch, seq, kw), kfull.reshape(batch, seq, kw), vv.reshape(batch, seq, vw)]
    if has_cache:
        p = kc.shape[1]
        in_specs += [
            pl.BlockSpec((None, p, kw), lambda b, i: (b, 0, 0)),
            pl.BlockSpec((None, p, vw), lambda b, i: (b, 0, 0)),
        ]
        args += [kc, vc]
    y = pl.pallas_call(
        functools.partial(_attn_kernel, has_cache=has_cache),
        grid=(batch, seq // tq),
        in_specs=in_specs,
        out_specs=pl.BlockSpec((None, tq, vw), lambda b, i: (b, i, 0)),
        out_shape=jax.ShapeDtypeStruct((batch, seq, vw), BF16),
        compiler_params=_cparams(("arbitrary", "arbitrary")),
        name="mla_attn",
    )(*args)
    return y.reshape(batch * seq, vw)


def _conv3_rows(u, prev_row, next_row, cw_ref, seq):
    r, w = u.shape
    sub = min(seq, r)
    whole = seq <= r
    i8 = lax.broadcasted_iota(jnp.int32, (SUBLANES, w), 0)
    w0, w1, w2 = cw_ref[0:1, :], cw_ref[1:2, :], cw_ref[2:3, :]
    outs = []
    for s in range(r // sub):
        blk = u[s * sub:(s + 1) * sub, :]
        p = 0.0 if whole else prev_row
        n = 0.0 if whole else next_row
        up = pltpu.roll(blk, 1, axis=0)
        dn = pltpu.roll(blk, sub - 1, axis=0)
        up = jnp.concatenate([jnp.where(i8 == 0, p, up[:SUBLANES, :]), up[SUBLANES:, :]], axis=0)
        dn = jnp.concatenate([dn[:sub - SUBLANES, :], jnp.where(i8 == SUBLANES - 1, n, dn[sub - SUBLANES:, :])],
                             axis=0)
        outs.append(up * w0 + blk * w1 + dn * w2)
    return outs[0] if len(outs) == 1 else jnp.concatenate(outs, axis=0)


def _seq_edges(i, rows, seq):
    first = i * rows
    return first % seq == 0, (first + rows) % seq == 0


def _merge_kernel(x_ref, ya_ref, yc_ref, bb_ref, u_ref, up_ref, un_ref, sg_ref, mod_ref,
                  wa_ref, wb_ref, wc_ref, wout_ref, cw_ref, n2_ref, x1_ref, xn2_ref, *, seq):
    tm = x_ref.shape[0]
    at_start, at_end = _seq_edges(pl.program_id(0), tm, seq)
    prev_row = jnp.where(at_start, 0.0, up_ref[HALO - 1:HALO, :].astype(F32))
    next_row = jnp.where(at_end, 0.0, un_ref[0:1, :].astype(F32))
    conv = _conv3_rows(u_ref[...].astype(F32), prev_row, next_row, cw_ref, seq)
    yb = (bb_ref[...] * conv).astype(BF16)
    half = tm // 2
    hs = []
    for s in range(2):
        rs = slice(s * half, (s + 1) * half)
        pa = _dot(ya_ref[rs, :], wa_ref[...])
        pc = _dot(yc_ref[rs, :], wc_ref[...])
        pb = _dot(yb[rs, :], wb_ref[...])
        hs.append((sg_ref[rs, 0:D_MODEL] * pa + sg_ref[rs, D_MODEL:2 * D_MODEL] * pb
                   + sg_ref[rs, 2 * D_MODEL:3 * D_MODEL] * pc).astype(BF16))
    gate1 = mod_ref[2:3, :]
    shift2 = mod_ref[3:4, :]
    scale2 = mod_ref[4:5, :]
    for s in range(2):
        rs = slice(s * half, (s + 1) * half)
        x1 = x_ref[rs, :] + gate1 * _dot(hs[s], wout_ref[...])
        x1_ref[rs, :] = x1
        xn2_ref[rs, :] = (_rms(x1, n2_ref[...]) * (1.0 + scale2) + shift2).astype(BF16)


def _merge_call(x2d, ya, yc, bb, u, sg, layer, mod4, mod_row, pw, seq):
    n = x2d.shape[0]
    tm = MERGE_ROWS
    nbh = n // HALO
    per = tm // HALO

    def row(width):
        return pl.BlockSpec((tm, width), lambda i: (i, 0))

    def wspec(name):
        return _layer_spec(pw[name].shape[1:], layer, 1)

    return pl.pallas_call(
        functools.partial(_merge_kernel, seq=seq),
        grid=(n // tm,),
        in_specs=[
            row(D_MODEL), row(A_W), row(C_HEADS * C_V), row(B_W), row(B_W),
            pl.BlockSpec((HALO, B_W), lambda i: (jnp.maximum(i * per - 1, 0), 0)),
            pl.BlockSpec((HALO, B_W), lambda i: (jnp.minimum((i + 1) * per, nbh - 1), 0)),
            row(N_BRANCH * D_MODEL),
            _mod_spec(layer, mod_row, tm),
            wspec("w_o_hgrn"), wspec("w_o_conv"), wspec("w_o_mla"), wspec("w_out"), wspec("conv_w"),
            wspec("norm2"),
        ],
        out_specs=[row(D_MODEL), row(D_MODEL)],
        out_shape=[jax.ShapeDtypeStruct((n, D_MODEL), F32), jax.ShapeDtypeStruct((n, D_MODEL), BF16)],
        compiler_params=_cparams(("arbitrary",)),
        name="merge",
    )(x2d, ya, yc, bb, u, u, u, sg, mod4, pw["w_o_hgrn"], pw["w_o_conv"], pw["w_o_mla"], pw["w_out"],
      pw["conv_w"], pw["norm2"])


def _ffn_kernel(xn_ref, xp_ref, xx_ref, x1_ref, mod_ref, wup_ref, cw_ref, wd_ref,
                fn_ref, out_ref, h_scr, lhs_scr, *, seq, final):
    r = xn_ref.shape[0]
    tn = FFN_COLS
    at_start, at_end = _seq_edges(pl.program_id(0), r, seq)
    need_halo = seq > r
    if need_halo:
        lhs_scr[0:HALO, :] = xp_ref[...]
        lhs_scr[HALO:HALO + r, :] = xn_ref[...]
        lhs_scr[HALO + r:, :] = xx_ref[...]
        xn = lhs_scr[...]
    else:
        xn = xn_ref[...]

    def up(off):
        return _dot(xn, wup_ref[:, off:off + tn])

    def conv_part(um, off):
        prev_row = next_row = None
        if need_halo:
            prev_row = jnp.where(at_start, 0.0, um[HALO - 1:HALO, :])
            next_row = jnp.where(at_end, 0.0, um[HALO + r:HALO + r + 1, :])
            um = um[HALO:HALO + r, :]
        return _conv3_rows(um, prev_row, next_row, cw_ref.at[:, off:off + tn], seq)

    offs = [o for j in range(D_FF // tn) for o in (j * tn, D_FF + j * tn)]
    pending = up(offs[0])
    a = None
    for idx, off in enumerate(offs):
        nxt = up(offs[idx + 1]) if idx + 1 < len(offs) else None
        conv = conv_part(pending, off)
        if off < D_FF:
            a = conv
        else:
            j = (off - D_FF) // tn
            h_scr[:, j * tn:(j + 1) * tn] = (_silu(a) * conv).astype(BF16)
        pending = nxt

    gate2 = mod_ref[5:6, :]
    y = x1_ref[...] + gate2 * _dot(h_scr[...], wd_ref[...])
    if final:
        y = _rms(y, fn_ref[...])
    out_ref[...] = y


def _ffn_call(xn2, x1, layer, mod4, mod_row, pw, fnorm, seq, final):
    n = xn2.shape[0]
    r = FFN_ROWS
    nbh = n // HALO
    per = r // HALO
    return pl.pallas_call(
        functools.partial(_ffn_kernel, seq=seq, final=final),
        grid=(n // r,),
        in_specs=[
            pl.BlockSpec((r, D_MODEL), lambda i: (i, 0)),
            pl.BlockSpec((HALO, D_MODEL), lambda i: (jnp.maximum(i * per - 1, 0), 0)),
            pl.BlockSpec((HALO, D_MODEL), lambda i: (jnp.minimum((i + 1) * per, nbh - 1), 0)),
            pl.BlockSpec((r, D_MODEL), lambda i: (i, 0)),
            _mod_spec(layer, mod_row, r),
            _layer_spec(pw["w_up"].shape[1:], layer, 1),
            _layer_spec(pw["ffn_conv_w"].shape[1:], layer, 1),
            _layer_spec(pw["w_down"].shape[1:], layer, 1),
            pl.BlockSpec((1, D_MODEL), lambda i: (0, 0)),
        ],
        out_specs=pl.BlockSpec((r, D_MODEL), lambda i: (i, 0)),
        out_shape=jax.ShapeDtypeStruct((n, D_MODEL), F32),
        scratch_shapes=[pltpu.VMEM((r, D_FF), BF16),
                        pltpu.VMEM((r + 2 * HALO if seq > r else HALO, D_MODEL), BF16)],
        compiler_params=_cparams(("arbitrary",)),
        name="conv_mlp",
    )(xn2, xn2, xn2, x1, mod4, pw["w_up"], pw["ffn_conv_w"], pw["w_down"], fnorm)


def _rope_swap_perm():
    idx = np.arange(C_ROPE).reshape(2, 2, C_ROPE // 4)
    return idx[:, ::-1, :].reshape(-1)


def _split_w_in(w_in):
    depth = w_in.shape[0]
    swap = _rope_swap_perm()
    wb = lax.optimization_barrier(w_in.astype(BF16))
    kv = wb[:, :, OFF_KV:OFF_GATE]
    kr = kv[:, :, C_KV_LORA:]
    pad = jnp.zeros((depth, D_MODEL, KV_EXT - CACHE_W - C_ROPE), BF16)
    return wb, jnp.concatenate([kv, kr[:, :, swap], pad], axis=2), wb[:, :, OFF_GATE:]


def _pack_w_q_up(w):
    depth = w.shape[0]
    swap = _rope_swap_perm()
    w4 = w.astype(BF16).reshape(depth, C_Q_LORA, C_HEADS, C_NOPE + C_ROPE)
    rope = w4[..., C_NOPE:]
    return jnp.concatenate([w4, rope[..., swap]], axis=3).reshape(depth, C_Q_LORA, C_HEADS * LANES)


def _pack_w_kv_up(w):
    depth = w.shape[0]
    w4 = w.astype(BF16).reshape(depth, C_KV_LORA, C_HEADS, C_NOPE + C_V)
    kn = jnp.concatenate([w4[..., :C_NOPE], jnp.zeros((depth, C_KV_LORA, C_HEADS, LANES - C_NOPE), BF16)], axis=3)
    return jnp.concatenate(
        [kn.reshape(depth, C_KV_LORA, C_HEADS * LANES), w4[..., C_NOPE:].reshape(depth, C_KV_LORA, C_HEADS * C_V)],
        axis=2)


def _pad_axis(a, size, axis):
    shape = list(a.shape)
    shape[axis] = size - a.shape[axis]
    return jnp.concatenate([a, jnp.zeros(shape, a.dtype)], axis=axis)


def _rope_tables(seq, rotate):
    scale = (C_NOPE + C_ROPE) ** -0.5 * LOG2E
    n_freq = C_ROPE // 4
    if rotate:
        rows = seq // GRID_W
        r = jnp.repeat(jnp.arange(rows, dtype=F32), GRID_W)
        col = jnp.tile(jnp.arange(GRID_W, dtype=F32), rows)
        freq = ROPE_THETA ** (-jnp.arange(n_freq, dtype=F32) / n_freq)
        ang = jnp.stack([r[:, None] * freq, col[:, None] * freq], axis=1)
        cos, sin = jnp.cos(ang), jnp.sin(ang)
    else:
        cos = jnp.ones((seq, 2, n_freq), F32)
        sin = jnp.zeros((seq, 2, n_freq), F32)
    cos_full = jnp.stack([cos, cos], axis=2).reshape(seq, C_ROPE)
    sin_signed = jnp.stack([-sin, sin], axis=2).reshape(seq, C_ROPE)
    tabq = scale * jnp.concatenate([jnp.ones((seq, C_NOPE), F32), cos_full, sin_signed], axis=1)
    tabk = jnp.concatenate([cos_full, sin_signed, jnp.zeros((seq, LANES - 2 * C_ROPE), F32)], axis=1)
    return tabq, tabk


def _layer(x2d, batch, seq, layer, mod4, mod_row, pw, lb_all, s_init, cache, tabs, want_ctx_outputs,
           final, fnorm, prev_states=(), prev_caches=()):
    tabq, tabk = tabs
    res = _in_call(x2d, layer, mod4, mod_row, pw, lb_all, tabq, tabk, want_ctx_outputs, seq, prev_caches)
    q, gf, gb, kf, kb, v, ga, bb, u, qfull, kfull, vv, sg = res[:13]
    own_cache = None
    if want_ctx_outputs:
        own_cache = res[13] if prev_caches else res[13].reshape(batch, seq, CACHE_W)
    ya, states = _hgrn_call(q, gf, gb, kf, kb, v, ga, layer, pw["gnorm"], s_init, batch, seq, want_ctx_outputs,
                            prev_states)
    kc = vc = None
    if cache is not None:
        kc, vc = _cache_call(cache[0], cache[1], layer, pw["w_kv_up"])
    yc = _attn_call(qfull, kfull, vv, kc, vc, batch, seq)
    x1, xn2 = _merge_call(x2d, ya, yc, bb, u, sg, layer, mod4, mod_row, pw, seq)
    x2 = _ffn_call(xn2, x1, layer, mod4, mod_row, pw, fnorm, seq, final)
    return x2, states, own_cache


def kernel(x_prompt, x_sample, state_hgrn, cache_mla, c, c_ctx, w_ada, b_ada, norm1, w_in, hgrn_lb_logits, hgrn_gnorm, w_o_hgrn, conv_w, w_o_conv, mla_q_norm, w_q_up, mla_kv_norm, w_kv_up, w_o_mla, w_out, norm2, w_up, ffn_conv_w, w_down, final_norm):
    depth = w_in.shape[0]
    bp, sp, _ = x_prompt.shape
    bs, ss, _ = x_sample.shape
    assert 1 + bs <= MOD_ROWS
    for rows in (IN_ROWS, MERGE_ROWS, FFN_ROWS):
        assert (bp * sp) % rows == 0 and (bs * ss) % rows == 0
        assert (rows % sp == 0 or sp % rows == 0) and ss % rows == 0

    cvec = _pad_axis(jnp.concatenate([c_ctx[None, :], c], axis=0), MOD_ROWS, 0)
    mod4 = _ada_call(cvec, w_ada, b_ada).reshape(depth, MOD_ROWS, 6, D_MODEL)
    lb_all = _lb_call(hgrn_lb_logits)
    tabs_ctx = _rope_tables(IN_ROWS, rotate=False)
    tabs_lat = _rope_tables(ss, rotate=True)
    fnorm = final_norm.reshape(1, D_MODEL)
    cache = (cache_mla, _pad_axis(cache_mla[..., C_KV_LORA:], LANES, 3))
    w_in_a, w_in_kv, w_in_g = _split_w_in(w_in)

    def vec(a):
        return a.reshape(depth, 1, -1)

    pw = {
        "norm1": vec(norm1), "w_in_a": w_in_a, "w_in_kv": w_in_kv, "w_in_g": w_in_g, "gnorm": vec(hgrn_gnorm),
        "w_o_hgrn": w_o_hgrn.astype(BF16), "conv_w": _pad_axis(conv_w, SUBLANES, 1),
        "w_o_conv": w_o_conv.astype(BF16), "q_norm": vec(mla_q_norm), "w_q_up": _pack_w_q_up(w_q_up),
        "kv_norm": vec(mla_kv_norm), "w_kv_up": _pack_w_kv_up(w_kv_up), "w_o_mla": w_o_mla.astype(BF16),
        "w_out": w_out.astype(BF16), "norm2": vec(norm2), "w_up": w_up.astype(BF16),
        "ffn_conv_w": _pad_axis(ffn_conv_w, SUBLANES, 1), "w_down": w_down.astype(BF16),
    }

    xp = x_prompt.reshape(bp * sp, D_MODEL)
    xs = x_sample.reshape(bs * ss, D_MODEL)
    stack_in_kernel = depth > 1 and IN_ROWS % sp == 0
    new_states, new_caches = [], []
    for l in range(depth):
        final = l == depth - 1
        prev = (tuple(new_states), tuple(new_caches)) if (final and stack_in_kernel) else ((), ())
        xp, st, kvc = _layer(xp, bp, sp, l, mod4, lambda r: 0, pw, lb_all, None, None, tabs_ctx,
                             True, final, fnorm, *prev)
        new_states.append(st)
        new_caches.append(kvc)
        xs, _, _ = _layer(xs, bs, ss, l, mod4, lambda r: 1 + r // ss, pw, lb_all, state_hgrn,
                          cache, tabs_lat, False, final, fnorm)
    y_prompt = xp.reshape(bp, sp, D_MODEL)
    y_sample = xs.reshape(bs, ss, D_MODEL)
    if stack_in_kernel:
        return (y_prompt, y_sample, new_states[-1], new_caches[-1])
    return (y_prompt, y_sample, jnp.stack(new_states, axis=1), jnp.stack(new_caches, axis=1))
```

```python
import functools
import math

import numpy as np
import jax
import jax.numpy as jnp
from jax import lax
from jax.experimental import pallas as pl
from jax.experimental.pallas import tpu as pltpu

F32 = jnp.float32
BF16 = jnp.bfloat16

D_MODEL = 1024
GRID_W = 64
EPS = 1e-6
A_HEADS = 4
A_DK = 128
A_DV = 128
A_W = A_HEADS * A_DK
LB_FLOOR = 1e-30
B_W = 512
C_HEADS = 8
C_NOPE = 64
C_ROPE = 32
C_V = 64
C_Q_LORA = 384
C_KV_LORA = 256
ROPE_THETA = 10000.0
D_FF = 2816
N_BRANCH = 3
CACHE_W = C_KV_LORA + C_ROPE
LOG2E = 1.4426950408889634

LANES = 128
SUBLANES = 8
HALO = 16
VMEM_LIMIT_BYTES = 56 * 1024 * 1024

IN_ROWS = 512
IN_LOOKAHEAD = 1
HGRN_CHUNK = 128
HGRN_CHUNK_GROUP = 2
HGRN_VMEM_BUDGET = 32 * 1024 * 1024
ATTN_Q_ROWS = 512
ATTN_SEQ_ROWS = 1024
MERGE_ROWS = 512
FFN_ROWS = 1024
FFN_COLS = 256
MOD_ROWS = 16
ADA_COLS = 1536

OFF_CQ = 8 * A_W
OFF_KV = OFF_CQ + C_Q_LORA
KV_EXT = 3 * LANES
OFF_GATE = OFF_KV + CACHE_W


def _cparams(sem):
    return pltpu.CompilerParams(dimension_semantics=sem, vmem_limit_bytes=VMEM_LIMIT_BYTES)


def _dot(a, b):
    return jnp.dot(a, b, preferred_element_type=F32)


def _dot_nt(a, b):
    return lax.dot_general(a, b, (((1,), (1,)), ((), ())), preferred_element_type=F32)


def _dot_tn(a, b):
    return lax.dot_general(a, b, (((0,), (0,)), ((), ())), preferred_element_type=F32)


def _rms(x, g):
    ms = jnp.mean(x * x, axis=-1, keepdims=True)
    return x * lax.rsqrt(ms + EPS) * g


def _sigmoid_pair(z):
    t = jnp.exp(-jnp.abs(z))
    r = 1.0 / (1.0 + t)
    big, small = r, t * r
    pos = z >= 0
    return jnp.where(pos, big, small), jnp.where(pos, small, big)


def _sigmoid(z):
    return 1.0 / (1.0 + jnp.exp(-z))


def _silu(z):
    return z * _sigmoid(z)


def _layer_spec(shape, layer, grid_rank, single=True):
    nd = len(shape)
    if grid_rank == 1:
        imap = lambda i: (layer,) + (0,) * nd
    else:
        imap = lambda i, j: (layer,) + (0,) * nd
    if single:
        return pl.BlockSpec((None,) + tuple(shape), imap, pipeline_mode=pl.Buffered(1))
    return pl.BlockSpec((None,) + tuple(shape), imap)


def _mod_spec(layer, mod_row, rows, grid_rank=1):
    if grid_rank == 1:
        return pl.BlockSpec((None, None, 6, D_MODEL), lambda i: (layer, mod_row(i * rows), 0, 0))
    return pl.BlockSpec((None, None, 6, D_MODEL), lambda i, j: (layer, mod_row(i * rows), 0, 0))


def _ada_kernel(c_ref, w_ref, b_ref, o_ref):
    s = _silu(c_ref[...])
    o_ref[...] = jnp.dot(s, w_ref[...], preferred_element_type=F32,
                         precision=lax.Precision.HIGHEST) + b_ref[...]


def _ada_call(cvec, w_ada, b_ada):
    depth = w_ada.shape[0]
    ncol = w_ada.shape[2] // ADA_COLS
    return pl.pallas_call(
        _ada_kernel,
        grid=(depth, ncol),
        in_specs=[
            pl.BlockSpec((MOD_ROWS, D_MODEL), lambda l, j: (0, 0)),
            pl.BlockSpec((None, D_MODEL, ADA_COLS), lambda l, j: (l, 0, j)),
            pl.BlockSpec((None, 1, ADA_COLS), lambda l, j: (l, 0, j)),
        ],
        out_specs=pl.BlockSpec((None, MOD_ROWS, ADA_COLS), lambda l, j: (l, 0, j)),
        out_shape=jax.ShapeDtypeStruct((depth, MOD_ROWS, w_ada.shape[2]), F32),
        compiler_params=_cparams(("arbitrary", "arbitrary")),
        name="ada_mod",
    )(cvec, w_ada, b_ada.reshape(depth, 1, -1))


def _lb_kernel(x_ref, o_ref):
    depth = x_ref.shape[0]
    xs = [x_ref[l] for l in range(depth)]
    m = xs[0]
    for l in range(1, depth):
        m = jnp.maximum(m, xs[l])
    es = [jnp.exp(x - m) for x in xs]
    tot = es[0]
    for l in range(1, depth):
        tot = tot + es[l]
    ps = [e / tot for e in es]
    run = jnp.zeros_like(ps[0])
    for l in range(depth):
        run = run + ps[l]
        o_ref[l] = run - ps[0]


def _lb_call(logits):
    return pl.pallas_call(
        _lb_kernel,
        out_shape=jax.ShapeDtypeStruct(logits.shape, F32),
        name="hgrn_lb",
    )(logits)


def _place_rope_key(z3, tabk):
    tmp = z3 * tabk
    rot = tmp + pltpu.roll(tmp, LANES - C_ROPE, axis=1)
    lane = lax.broadcasted_iota(jnp.int32, rot.shape, 1)
    rot = jnp.where(lane < C_ROPE, rot, 0.0)
    return pltpu.roll(rot, C_NOPE, axis=1) + pltpu.roll(rot, C_NOPE + C_ROPE, axis=1)


def _in_kernel(*refs, n_prev, seq):
    (x_ref, mod_ref, n1_ref, w_ref, wkvd_ref, wg_ref, lb_ref, qn_ref, wq_ref, kvn_ref, wkv_ref,
     tabq_ref, tabk_ref) = refs[:13]
    prev_cache_refs = refs[13:13 + n_prev]
    (q_ref, gf_ref, gb_ref, kf_ref, kb_ref, v_ref, ga_ref, bb_ref, u_ref,
     qfull_ref, kfull_ref, vv_ref, sg_ref) = refs[13 + n_prev:26 + n_prev]
    cache_ref = refs[26 + n_prev:]
    x = x_ref[...]
    shift1 = mod_ref[0:1, :]
    scale1 = mod_ref[1:2, :]
    xn = _rms(x, n1_ref[...]) * (1.0 + scale1) + shift1
    xb = xn.astype(BF16)

    def proj(off, width):
        return lambda: _dot(xb, w_ref[:, off:off + width])

    env = {}
    stages = []

    def ep_q(z):
        q_ref[...] = (_silu(z) * (A_DK ** -0.5)).astype(BF16)
    stages.append((proj(0, A_W), ep_q))

    def ep_forget(d, g_ref, k_ref):
        def ep(z):
            lb = lb_ref[d:d + 1, :]
            sp, sn = _sigmoid_pair(z)
            g_ref[...] = jnp.log(jnp.maximum(lb, LB_FLOOR) + (1.0 - lb) * sp) * LOG2E
            k_ref[...] = ((1.0 - lb) * sn - jnp.maximum(LB_FLOOR - lb, 0.0)).astype(BF16)
        return ep
    stages.append((proj(1 * A_W, A_W), ep_forget(0, gf_ref, kf_ref)))
    stages.append((proj(2 * A_W, A_W), ep_forget(1, gb_ref, kb_ref)))

    def ep_v(z):
        v_ref[...] = z.astype(BF16)
    stages.append((proj(3 * A_W, A_W), ep_v))

    def ep_ga(z):
        ga_ref[...] = _silu(z).astype(BF16)
    stages.append((proj(4 * A_W, A_W), ep_ga))

    def ep_bb(z):
        bb_ref[...] = z.astype(BF16)
    stages.append((proj(5 * A_W, B_W), ep_bb))

    def ep_cb(z):
        env["conv_c"] = z
    stages.append((proj(6 * A_W, B_W), ep_cb))

    def ep_hb(z):
        u_ref[...] = (env["conv_c"] * z).astype(BF16)
    stages.append((proj(7 * A_W, B_W), ep_hb))

    def ep_cq(z):
        env["cqn"] = _rms(z, qn_ref[...]).astype(BF16)
    stages.append((proj(OFF_CQ, C_Q_LORA), ep_cq))

    def ep_kv(z):
        ckvn = _rms(z[:, :C_KV_LORA], kvn_ref[...])
        z3 = z[:, C_KV_LORA:]
        if cache_ref and n_prev == 0:
            cache_ref[0][:, :C_KV_LORA] = ckvn
            cache_ref[0][:, C_KV_LORA:] = z3[:, :C_ROPE]
        elif cache_ref:
            for s in range(x_ref.shape[0] // seq):
                rows = slice(s * seq, (s + 1) * seq)
                for l, prev in enumerate(prev_cache_refs):
                    cache_ref[0][s, l] = prev[s]
                cache_ref[0][s, n_prev, :, :C_KV_LORA] = ckvn[rows, :]
                cache_ref[0][s, n_prev, :, C_KV_LORA:] = z3[rows, :C_ROPE]
        krr = _place_rope_key(z3, tabk_ref[...])
        env["krr"] = jnp.concatenate([krr, krr], axis=1)
        env["cb"] = ckvn.astype(BF16)
    stages.append((lambda: _dot(xb, wkvd_ref[...]), ep_kv))

    def gate_stage(j):
        def ep(z):
            sg_ref[:, j * A_W:(j + 1) * A_W] = _sigmoid(z).astype(BF16)
        return (lambda: _dot(xb, wg_ref[:, j * A_W:(j + 1) * A_W]), ep)

    def q_up_stage(h):
        def ep(z):
            tabq = jnp.concatenate([tabq_ref[...]] * 2, axis=1)
            qfull_ref[:, h * LANES:(h + 2) * LANES] = (z * tabq).astype(BF16)
        return (lambda: _dot(env["cqn"], wq_ref[:, h * LANES:(h + 2) * LANES]), ep)

    def k_up_stage(h):
        def ep(z):
            kfull_ref[:, h * LANES:(h + 2) * LANES] = (z + env["krr"]).astype(BF16)
        return (lambda: _dot(env["cb"], wkv_ref[:, h * LANES:(h + 2) * LANES]), ep)

    def ep_vv(z):
        vv_ref[...] = z.astype(BF16)
    v_up_stage = (lambda: _dot(env["cb"], wkv_ref[:, C_HEADS * LANES:]), ep_vv)

    stages = stages[8:] + stages[:8]
    stages += [gate_stage(0), q_up_stage(0), q_up_stage(2), gate_stage(1), q_up_stage(4), q_up_stage(6),
               gate_stage(2), k_up_stage(0), k_up_stage(2), gate_stage(3), k_up_stage(4), k_up_stage(6),
               gate_stage(4), v_up_stage, gate_stage(5)]

    pending = [stages[j][0]() for j in range(IN_LOOKAHEAD)]
    for j, (_, epilogue) in enumerate(stages):
        if j + IN_LOOKAHEAD < len(stages):
            pending.append(stages[j + IN_LOOKAHEAD][0]())
        epilogue(pending.pop(0))


def _in_call(x2d, layer, mod4, mod_row, pw, lb_all, tabq, tabk, want_cache, seq, prev_caches=()):
    n = x2d.shape[0]
    tm = IN_ROWS
    tab_blocks = tabq.shape[0] // tm
    n_prev = len(prev_caches) if want_cache else 0

    def row(width):
        return pl.BlockSpec((tm, width), lambda i: (i, 0))

    outs = [
        (A_W, BF16),
        (A_W, F32), (A_W, F32),
        (A_W, BF16), (A_W, BF16),
        (A_W, BF16),
        (A_W, BF16),
        (B_W, BF16),
        (B_W, BF16),
        (C_HEADS * LANES, BF16),
        (C_HEADS * LANES, BF16),
        (C_HEADS * C_V, BF16),
        (N_BRANCH * D_MODEL, BF16),
    ]
    out_specs = [row(w) for w, _ in outs]
    out_shape = [jax.ShapeDtypeStruct((n, w), dt) for w, dt in outs]
    prev_specs = []
    if want_cache and n_prev == 0:
        out_specs.append(row(CACHE_W))
        out_shape.append(jax.ShapeDtypeStruct((n, CACHE_W), F32))
    elif want_cache:
        per = tm // seq
        prev_specs = [pl.BlockSpec((per, seq, CACHE_W), lambda i: (i, 0, 0)) for _ in prev_caches]
        out_specs.append(pl.BlockSpec((per, n_prev + 1, seq, CACHE_W), lambda i: (i, 0, 0, 0)))
        out_shape.append(jax.ShapeDtypeStruct((n // seq, n_prev + 1, seq, CACHE_W), F32))
    return pl.pallas_call(
        functools.partial(_in_kernel, n_prev=n_prev, seq=seq),
        grid=(n // tm,),
        in_specs=[
            row(D_MODEL),
            _mod_spec(layer, mod_row, tm),
            _layer_spec((1, D_MODEL), layer, 1),
            _layer_spec((D_MODEL, OFF_KV), layer, 1),
            _layer_spec((D_MODEL, KV_EXT), layer, 1),
            _layer_spec((D_MODEL, N_BRANCH * D_MODEL), layer, 1),
            _layer_spec((2, A_W), layer, 1),
            _layer_spec((1, C_Q_LORA), layer, 1),
            _layer_spec((C_Q_LORA, C_HEADS * LANES), layer, 1),
            _layer_spec((1, C_KV_LORA), layer, 1),
            _layer_spec((C_KV_LORA, C_HEADS * LANES + C_HEADS * C_V), layer, 1),
            pl.BlockSpec((tm, LANES), lambda i: (i % tab_blocks, 0)),
            pl.BlockSpec((tm, LANES), lambda i: (i % tab_blocks, 0)),
        ] + prev_specs,
        out_specs=out_specs,
        out_shape=out_shape,
        compiler_params=_cparams(("arbitrary",)),
        name="in_proj",
    )(x2d, mod4, pw["norm1"], pw["w_in_a"], pw["w_in_kv"], pw["w_in_g"], lb_all, pw["q_norm"], pw["w_q_up"],
      pw["kv_norm"], pw["w_kv_up"], tabq, tabk, *(prev_caches if n_prev else ()))


def _cache_kernel(c_ref, kr_ref, wkv_ref, kfull_ref, vv_ref):
    cb = c_ref[:, :C_KV_LORA].astype(BF16)
    kr = kr_ref[...]
    krr = pltpu.roll(kr, C_NOPE, axis=1) + pltpu.roll(kr, C_NOPE + C_ROPE, axis=1)
    krr = jnp.concatenate([krr, krr], axis=1)
    for h in range(0, C_HEADS, 2):
        kz = _dot(cb, wkv_ref[:, h * LANES:(h + 2) * LANES])
        kfull_ref[:, h * LANES:(h + 2) * LANES] = (kz + krr).astype(BF16)
    vv_ref[...] = _dot(cb, wkv_ref[:, C_HEADS * LANES:]).astype(BF16)


def _cache_call(cache, cache_kr, layer, wkv):
    b, _, p, w = cache.shape
    return pl.pallas_call(
        _cache_kernel,
        grid=(b,),
        in_specs=[
            pl.BlockSpec((None, None, p, w), lambda i: (i, layer, 0, 0)),
            pl.BlockSpec((None, None, p, LANES), lambda i: (i, layer, 0, 0)),
            _layer_spec(wkv.shape[1:], layer, 1, single=False),
        ],
        out_specs=[
            pl.BlockSpec((None, p, C_HEADS * LANES), lambda i: (i, 0, 0)),
            pl.BlockSpec((None, p, C_HEADS * C_V), lambda i: (i, 0, 0)),
        ],
        out_shape=[
            jax.ShapeDtypeStruct((b, p, C_HEADS * LANES), BF16),
            jax.ShapeDtypeStruct((b, p, C_HEADS * C_V), BF16),
        ],
        compiler_params=_cparams(("arbitrary",)),
        name="cache_expand",
    )(cache, cache_kr, wkv)


def _level_matrix(c, rev):
    t = np.arange(c)[:, None]
    s = np.arange(c)[None, :]
    lev = np.floor(np.log2(np.maximum(np.bitwise_xor(t, s), 1))).astype(np.int32)
    lev = np.where(s == t, -1, lev)
    lev = np.where((s < t) if rev else (s > t), -2, lev)
    return lev.astype(np.int32)


def _prefix_matrix(c, rev):
    t = np.arange(c)[:, None]
    s = np.arange(c)[None, :]
    incl = ((s >= t) if rev else (s <= t)).astype(np.float32)
    return np.concatenate([incl, incl], axis=1)


def _level_exponent(b, g, h, rev):
    c, dk = b.shape
    ref = h if rev else h - 1
    if 2 * h > SUBLANES:
        parts = []
        for gs in range(0, c, 2 * h):
            beta = jnp.broadcast_to(b[gs + ref:gs + ref + 1, :], (h, dk))
            lo = b[gs:gs + h, :]
            hi = b[gs + h:gs + 2 * h, :]
            parts += [lo - beta, beta - hi] if rev else [beta - lo, hi - beta]
        return jnp.concatenate(parts, axis=0)
    row = lax.broadcasted_iota(jnp.int32, (SUBLANES, dk), 0)
    upper = (row % (2 * h)) >= h
    keep = jnp.logical_not(upper) if rev else upper
    if h == 1:
        parts = [jnp.where(keep, g[j:j + SUBLANES, :], 0.0) for j in range(0, c, SUBLANES)]
        return jnp.concatenate(parts, axis=0)
    sign = jnp.where(keep, 1.0, -1.0)
    parts = []
    for j in range(0, c, SUBLANES):
        bj = b[j:j + SUBLANES, :]
        beta = jnp.broadcast_to(bj[ref:ref + 1, :], (SUBLANES, dk))
        for gs in range(2 * h, SUBLANES, 2 * h):
            beta = jnp.where(row >= gs, jnp.broadcast_to(bj[gs + ref:gs + ref + 1, :], (SUBLANES, dk)), beta)
        parts.append((bj - beta) * sign)
    return jnp.concatenate(parts, axis=0)


def _level_halves(c, h):
    return [(slice(gs, gs + h), slice(gs + h, gs + 2 * h)) for gs in range(0, c, 2 * h)]


def _level_operands(q, k, qf, kf, x, h, rev):
    c = q.shape[0]
    if h < SUBLANES:
        return (qf * x).astype(BF16), (kf * x).astype(BF16)
    packed = h >= HALO
    qparts, kparts = [], []
    for lo, hi in _level_halves(c, h):
        qs, ks = (lo, hi) if rev else (hi, lo)
        q_sc, k_sc = qf[qs, :] * x[qs, :], kf[ks, :] * x[ks, :]
        if packed:
            q_sc, k_sc = q_sc.astype(BF16), k_sc.astype(BF16)
            q_raw, k_raw = q[ks, :], k[qs, :]
        else:
            q_raw, k_raw = qf[ks, :], kf[qs, :]
        qparts += [q_sc, q_raw] if rev else [q_raw, q_sc]
        kparts += [k_raw, k_sc] if rev else [k_sc, k_raw]
    qh, kh = jnp.concatenate(qparts, axis=0), jnp.concatenate(kparts, axis=0)
    return (qh, kh) if packed else (qh.astype(BF16), kh.astype(BF16))


def _level_select(a, p, lev, hl, rev):
    c = a.shape[0]
    h = 1 << hl
    if h < SUBLANES:
        return jnp.where(lev == hl, p, a)
    parts = []
    for lo, hi in _level_halves(c, h):
        qs, other = (lo, hi) if rev else (hi, lo)
        sel = jnp.where(lev[qs, :] == hl, p[qs, :], a[qs, :])
        parts += [sel, a[other, :]] if rev else [a[other, :], sel]
    return jnp.concatenate(parts, axis=0)


def _hgrn_intra(streams, lev, pmat):
    c = streams[0][0].shape[0]
    bs = []
    for q, k, g, v, rev in streams:
        g_hi = g.astype(BF16)
        g_lo = (g - g_hi.astype(F32)).astype(BF16)
        bs.append(_dot(pmat[rev], jnp.concatenate([g_hi, g_lo], axis=0)))
    qfs = [s[0].astype(F32) for s in streams]
    kfs = [s[1].astype(F32) for s in streams]
    a_s = [jnp.zeros((c, c), F32) for _ in streams]
    for hl in range(int(np.log2(c))):
        ps = []
        for (q, k, g, v, rev), b, qf, kf in zip(streams, bs, qfs, kfs):
            x = jnp.exp2(_level_exponent(b, g, 1 << hl, rev))
            ps.append(_dot_nt(*_level_operands(q, k, qf, kf, x, 1 << hl, rev)))
        a_s = [_level_select(a, p, lev[s[4]], hl, s[4]) for s, p, a in zip(streams, ps, a_s)]
    ps = [_dot_nt(s[0], s[1]) for s in streams]
    a_s = [jnp.where(lev[s[4]] == -1, p, a).astype(BF16) for s, p, a in zip(streams, ps, a_s)]
    return bs, a_s, qfs, kfs


def _hgrn_inter(streams, intra, states):
    c = streams[0][0].shape[0]
    bs, a_s, qfs, kfs = intra
    outs = []
    for (q, k, g, v, rev), b, a, qf, st in zip(streams, bs, a_s, qfs, states):
        outs.append(_dot(a, v) + _dot_nt((qf * jnp.exp2(b)).astype(BF16), st.astype(BF16)))
    new_states = []
    for (q, k, g, v, rev), b, kf, st in zip(streams, bs, kfs, states):
        tot = b[0:1, :] if rev else b[c - 1:c, :]
        ks = (kf * jnp.exp2(tot - b)).astype(BF16)
        new_states.append(st * jnp.exp2(tot) + _dot_tn(v, ks))
    return outs, new_states


def _hgrn_kernel(*refs, has_init, want_state, n_prev):
    refs = list(refs)
    (q_ref, gf_ref, gb_ref, kf_ref, kb_ref, v_ref, ga_ref, gn_ref,
     levf_ref, levb_ref, pmf_ref, pmb_ref) = refs[:12]
    pos = 12
    s0_ref = None
    if has_init:
        s0_ref = refs[pos]
        pos += 1
    prev_state_refs = refs[pos:pos + n_prev]
    pos += n_prev
    y_ref = refs[pos]
    pos += 1
    sfin_ref = None
    if want_state:
        sfin_ref = refs[pos]
        pos += 1
    o_scr, st_scr = refs[pos], refs[pos + 1]

    t = q_ref.shape[0]
    c = HGRN_CHUNK
    n = t // c
    hp = q_ref.shape[1] // A_DK
    for d in range(2):
        for h in range(hp):
            st_scr[d, h] = s0_ref[d, h].T if has_init else jnp.zeros((A_DV, A_DK), F32)
    lev = (levf_ref, levb_ref)
    pmat = (pmf_ref, pmb_ref)
    g_refs = (gf_ref, gb_ref)
    k_refs = (kf_ref, kb_ref)

    group = math.gcd(n, HGRN_CHUNK_GROUP)

    def body(i, carry):
        levs = (lev[0][...], lev[1][...])
        pmats = (pmat[0][...], pmat[1][...])
        steps = []
        for u in range(group):
            streams, rows = [], []
            for d in range(2):
                ci = i * group + u if d == 0 else n - 1 - (i * group + u)
                sl = pl.ds(pl.multiple_of(ci * c, c), c)
                rows.append(sl)
                for h in range(hp):
                    cs = slice(h * A_DK, (h + 1) * A_DK)
                    streams.append((q_ref[sl, cs], k_refs[d][sl, cs], g_refs[d][sl, cs], v_ref[sl, cs], d))
            steps.append((streams, rows))
        all_streams = [s for streams, _ in steps for s in streams]
        bs, a_s, qfs, kfs = _hgrn_intra(all_streams, levs, pmats)
        per = 2 * hp
        states = [st_scr[d, h] for d in range(2) for h in range(hp)]
        for u, (streams, rows) in enumerate(steps):
            part = slice(u * per, (u + 1) * per)
            outs, states = _hgrn_inter(streams, (bs[part], a_s[part], qfs[part], kfs[part]), states)
            for idx, o in enumerate(outs):
                d, h = divmod(idx, hp)
                o_scr[d, rows[d], h * A_DV:(h + 1) * A_DV] = o
        for idx, st_new in enumerate(states):
            d, h = divmod(idx, hp)
            st_scr[d, h] = st_new
        return carry

    lax.fori_loop(0, n // group, body, 0)
    for l, prev in enumerate(prev_state_refs):
        sfin_ref[l] = prev[...]
    for h in range(hp):
        cs = slice(h * A_DV, (h + 1) * A_DV)
        if want_state:
            for d in range(2):
                if n_prev:
                    sfin_ref[n_prev, d, h] = st_scr[d, h].T
                else:
                    sfin_ref[d, h] = st_scr[d, h].T
        o = o_scr[0, :, cs] + o_scr[1, :, cs]
        y_ref[:, cs] = (_rms(o, gn_ref[...]) * ga_ref[:, cs]).astype(BF16)


def _hgrn_heads_per_step(seq):
    per_head_row = 2 * (5 * 2 + 2 * 4) * A_DK + 2 * 2 * A_DV + 2 * 4 * A_DV
    hp = A_HEADS
    while hp > 1 and hp * seq * per_head_row > HGRN_VMEM_BUDGET:
        hp //= 2
    return hp


def _hgrn_call(q, gf, gb, kf, kb, v, ga, layer, gnorm, s_init, batch, seq, want_state, prev_states=()):
    c = HGRN_CHUNK
    n_prev = len(prev_states) if want_state else 0
    hp = _hgrn_heads_per_step(seq)
    consts =[jnp.asarray(_level_matrix(c, False)), jnp.asarray(_level_matrix(c, True)),
              jnp.asarray(_prefix_matrix(c, False), BF16), jnp.asarray(_prefix_matrix(c, True), BF16)]

    def r3(a):
        return a.reshape(batch, seq, A_W)

    col = pl.BlockSpec((None, seq, hp * A_DK), lambda b, h: (b, 0, h))
    in_specs = [col] * 7 + [_layer_spec((1, A_DV), layer, 2, single=False)]
    in_specs += [pl.BlockSpec(m.shape, lambda b, h: (0, 0)) for m in consts]
    args = [r3(q), r3(gf), r3(gb), r3(kf), r3(kb), r3(v), r3(ga), gnorm] + consts
    has_init = s_init is not None
    if has_init:
        in_specs.append(pl.BlockSpec((None, None, 2, hp, A_DK, A_DV), lambda b, h: (b, layer, 0, h, 0, 0)))
        args.append(s_init)
    st_spec = pl.BlockSpec((None, 2, hp, A_DK, A_DV), lambda b, h: (b, 0, h, 0, 0))
    for prev in (prev_states if n_prev else ()):
        in_specs.append(st_spec)
        args.append(prev)
    out_specs = [col]
    out_shape = [jax.ShapeDtypeStruct((batch, seq, A_W), BF16)]
    if want_state and n_prev:
        out_specs.append(pl.BlockSpec((None, n_prev + 1, 2, hp, A_DK, A_DV), lambda b, h: (b, 0, 0, h, 0, 0)))
        out_shape.append(jax.ShapeDtypeStruct((batch, n_prev + 1, 2, A_HEADS, A_DK, A_DV), F32))
    elif want_state:
        out_specs.append(st_spec)
        out_shape.append(jax.ShapeDtypeStruct((batch, 2, A_HEADS, A_DK, A_DV), F32))
    res = pl.pallas_call(
        functools.partial(_hgrn_kernel, has_init=has_init, want_state=want_state, n_prev=n_prev),
        grid=(batch, A_HEADS // hp),
        in_specs=in_specs,
        out_specs=out_specs,
        out_shape=out_shape,
        scratch_shapes=[pltpu.VMEM((2, seq, hp * A_DV), F32), pltpu.VMEM((2, hp, A_DV, A_DK), F32)],
        compiler_params=_cparams(("arbitrary", "arbitrary")),
        name="hgrn",
    )(*args)
    y = res[0].reshape(batch * seq, A_W)
    return y, (res[1] if want_state else None)


def _attn_kernel(*refs, has_cache):
    if has_cache:
        q_ref, k_ref, v_ref, kc_ref, vc_ref, y_ref = refs
    else:
        q_ref, k_ref, v_ref, y_ref = refs
    nseq = q_ref.shape[0]
    lane = lax.broadcasted_iota(jnp.int32, (q_ref.shape[1], LANES), 1)

    def scores(s, h):
        hs = slice(h * LANES, (h + 1) * LANES)
        qh = q_ref[s, :, hs]
        s1 = _dot_nt(qh, k_ref[s, :, hs])
        s2 = _dot_nt(qh, kc_ref[s, :, hs]) if has_cache else None
        return s1, s2

    def attend(s, h, s1, s2):
        vs = slice((h // 2) * LANES, (h // 2 + 1) * LANES)
        m = jnp.max(s1, axis=-1, keepdims=True)
        if has_cache:
            m = jnp.maximum(m, jnp.max(s2, axis=-1, keepdims=True))
        p1 = jnp.exp2(s1 - m)
        l = jnp.sum(p1, axis=-1, keepdims=True)
        o = _dot(p1.astype(BF16), v_ref[s, :, vs])
        if has_cache:
            p2 = jnp.exp2(s2 - m)
            l = l + jnp.sum(p2, axis=-1, keepdims=True)
            o = o + _dot(p2.astype(BF16), vc_ref[s, :, vs])
        return o / l

    items = [(s, h) for s in range(nseq) for h in range(C_HEADS)]
    pending = scores(*items[0])
    prev_out = None
    for idx, (s, h) in enumerate(items):
        nxt = scores(*items[idx + 1]) if idx + 1 < len(items) else None
        out = attend(s, h, *pending)
        pending = nxt
        if h % 2 == 1:
            vs = slice((h // 2) * LANES, (h // 2 + 1) * LANES)
            y_ref[s, :, vs] = jnp.where(lane < C_V, prev_out, out).astype(BF16)
        prev_out = out


def _attn_call(qfull, kfull, vv, kc, vc, batch, seq):
    tq = min(ATTN_Q_ROWS, seq)
    nb = math.gcd(batch, max(1, ATTN_SEQ_ROWS // seq)) if tq == seq else 1
    has_cache = kc is not None
    kw = C_HEADS * LANES
    vw = C_HEADS * C_V
    in_specs = [
        pl.BlockSpec((nb, tq, kw), lambda b, i: (b, i, 0)),
        pl.BlockSpec((nb, seq, kw), lambda b, i: (b, 0, 0)),
        pl.BlockSpec((nb, seq, vw), lambda b, i: (b, 0, 0)),
    ]
    args = [qfull.reshape(batch, seq, kw), kfull.reshape(batch, seq, kw), vv.reshape(batch, seq, vw)]
    if has_cache:
        p = kc.shape[1]
        in_specs += [
            pl.BlockSpec((nb, p, kw), lambda b, i: (b, 0, 0)),
            pl.BlockSpec((nb, p, vw), lambda b, i: (b, 0, 0)),
        ]
        args += [kc, vc]
    y = pl.pallas_call(
        functools.partial(_attn_kernel, has_cache=has_cache),
        grid=(batch // nb, seq // tq),
        in_specs=in_specs,
        out_specs=pl.BlockSpec((nb, tq, vw), lambda b, i: (b, i, 0)),
        out_shape=jax.ShapeDtypeStruct((batch, seq, vw), BF16),
        compiler_params=_cparams(("arbitrary", "arbitrary")),
        name="mla_attn",
    )(*args)
    return y.reshape(batch * seq, vw)


def _conv3_rows(u, prev_row, next_row, cw_ref, seq):
    r, w = u.shape
    sub = min(seq, r)
    whole = seq <= r
    i8 = lax.broadcasted_iota(jnp.int32, (SUBLANES, w), 0)
    w0, w1, w2 = cw_ref[0:1, :], cw_ref[1:2, :], cw_ref[2:3, :]
    outs = []
    for s in range(r // sub):
        blk = u[s * sub:(s + 1) * sub, :]
        p = 0.0 if whole else prev_row
        n = 0.0 if whole else next_row
        up = pltpu.roll(blk, 1, axis=0)
        dn = pltpu.roll(blk, sub - 1, axis=0)
        up = jnp.concatenate([jnp.where(i8 == 0, p, up[:SUBLANES, :]), up[SUBLANES:, :]], axis=0)
        dn = jnp.concatenate([dn[:sub - SUBLANES, :], jnp.where(i8 == SUBLANES - 1, n, dn[sub - SUBLANES:, :])],
                             axis=0)
        outs.append(up * w0 + blk * w1 + dn * w2)
    return outs[0] if len(outs) == 1 else jnp.concatenate(outs, axis=0)


def _seq_edges(i, rows, seq):
    first = i * rows
    return first % seq == 0, (first + rows) % seq == 0


def _merge_kernel(x_ref, ya_ref, yc_ref, bb_ref, u_ref, up_ref, un_ref, sg_ref, mod_ref,
                  wa_ref, wb_ref, wc_ref, wout_ref, cw_ref, n2_ref, x1_ref, xn2_ref, *, seq):
    tm = x_ref.shape[0]
    at_start, at_end = _seq_edges(pl.program_id(0), tm, seq)
    prev_row = jnp.where(at_start, 0.0, up_ref[HALO - 1:HALO, :].astype(F32))
    next_row = jnp.where(at_end, 0.0, un_ref[0:1, :].astype(F32))
    conv = _conv3_rows(u_ref[...].astype(F32), prev_row, next_row, cw_ref, seq)
    yb = (bb_ref[...] * conv).astype(BF16)
    half = tm // 2
    hs = []
    for s in range(2):
        rs = slice(s * half, (s + 1) * half)
        pa = _dot(ya_ref[rs, :], wa_ref[...])
        pc = _dot(yc_ref[rs, :], wc_ref[...])
        pb = _dot(yb[rs, :], wb_ref[...])
        hs.append((sg_ref[rs, 0:D_MODEL] * pa + sg_ref[rs, D_MODEL:2 * D_MODEL] * pb
                   + sg_ref[rs, 2 * D_MODEL:3 * D_MODEL] * pc).astype(BF16))
    gate1 = mod_ref[2:3, :]
    shift2 = mod_ref[3:4, :]
    scale2 = mod_ref[4:5, :]
    for s in range(2):
        rs = slice(s * half, (s + 1) * half)
        x1 = x_ref[rs, :] + gate1 * _dot(hs[s], wout_ref[...])
        x1_ref[rs, :] = x1
        xn2_ref[rs, :] = (_rms(x1, n2_ref[...]) * (1.0 + scale2) + shift2).astype(BF16)


def _merge_call(x2d, ya, yc, bb, u, sg, layer, mod4, mod_row, pw, seq):
    n = x2d.shape[0]
    tm = MERGE_ROWS
    nbh = n // HALO
    per = tm // HALO

    def row(width):
        return pl.BlockSpec((tm, width), lambda i: (i, 0))

    def wspec(name):
        return _layer_spec(pw[name].shape[1:], layer, 1)

    return pl.pallas_call(
        functools.partial(_merge_kernel, seq=seq),
        grid=(n // tm,),
        in_specs=[
            row(D_MODEL), row(A_W), row(C_HEADS * C_V), row(B_W), row(B_W),
            pl.BlockSpec((HALO, B_W), lambda i: (jnp.maximum(i * per - 1, 0), 0)),
            pl.BlockSpec((HALO, B_W), lambda i: (jnp.minimum((i + 1) * per, nbh - 1), 0)),
            row(N_BRANCH * D_MODEL),
            _mod_spec(layer, mod_row, tm),
            wspec("w_o_hgrn"), wspec("w_o_conv"), wspec("w_o_mla"), wspec("w_out"), wspec("conv_w"),
            wspec("norm2"),
        ],
        out_specs=[row(D_MODEL), row(D_MODEL)],
        out_shape=[jax.ShapeDtypeStruct((n, D_MODEL), F32), jax.ShapeDtypeStruct((n, D_MODEL), BF16)],
        compiler_params=_cparams(("arbitrary",)),
        name="merge",
    )(x2d, ya, yc, bb, u, u, u, sg, mod4, pw["w_o_hgrn"], pw["w_o_conv"], pw["w_o_mla"], pw["w_out"],
      pw["conv_w"], pw["norm2"])


def _ffn_kernel(xn_ref, xp_ref, xx_ref, x1_ref, mod_ref, wup_ref, cw_ref, wd_ref,
                fn_ref, out_ref, h_scr, lhs_scr, *, seq, final):
    r = xn_ref.shape[0]
    tn = FFN_COLS
    at_start, at_end = _seq_edges(pl.program_id(0), r, seq)
    need_halo = seq > r
    if need_halo:
        lhs_scr[0:HALO, :] = xp_ref[...]
        lhs_scr[HALO:HALO + r, :] = xn_ref[...]
        lhs_scr[HALO + r:, :] = xx_ref[...]
        xn = lhs_scr[...]
    else:
        xn = xn_ref[...]

    def up(off):
        return _dot(xn, wup_ref[:, off:off + tn])

    def conv_part(um, off):
        prev_row = next_row = None
        if need_halo:
            prev_row = jnp.where(at_start, 0.0, um[HALO - 1:HALO, :])
            next_row = jnp.where(at_end, 0.0, um[HALO + r:HALO + r + 1, :])
            um = um[HALO:HALO + r, :]
        return _conv3_rows(um, prev_row, next_row, cw_ref.at[:, off:off + tn], seq)

    offs = [o for j in range(D_FF // tn) for o in (j * tn, D_FF + j * tn)]
    pending = up(offs[0])
    a = None
    for idx, off in enumerate(offs):
        nxt = up(offs[idx + 1]) if idx + 1 < len(offs) else None
        conv = conv_part(pending, off)
        if off < D_FF:
            a = conv
        else:
            j = (off - D_FF) // tn
            h_scr[:, j * tn:(j + 1) * tn] = (_silu(a) * conv).astype(BF16)
        pending = nxt

    gate2 = mod_ref[5:6, :]
    y = x1_ref[...] + gate2 * _dot(h_scr[...], wd_ref[...])
    if final:
        y = _rms(y, fn_ref[...])
    out_ref[...] = y


def _ffn_call(xn2, x1, layer, mod4, mod_row, pw, fnorm, seq, final):
    n = xn2.shape[0]
    r = FFN_ROWS
    nbh = n // HALO
    per = r // HALO
    return pl.pallas_call(
        functools.partial(_ffn_kernel, seq=seq, final=final),
        grid=(n // r,),
        in_specs=[
            pl.BlockSpec((r, D_MODEL), lambda i: (i, 0)),
            pl.BlockSpec((HALO, D_MODEL), lambda i: (jnp.maximum(i * per - 1, 0), 0)),
            pl.BlockSpec((HALO, D_MODEL), lambda i: (jnp.minimum((i + 1) * per, nbh - 1), 0)),
            pl.BlockSpec((r, D_MODEL), lambda i: (i, 0)),
            _mod_spec(layer, mod_row, r),
            _layer_spec(pw["w_up"].shape[1:], layer, 1),
            _layer_spec(pw["ffn_conv_w"].shape[1:], layer, 1),
            _layer_spec(pw["w_down"].shape[1:], layer, 1),
            pl.BlockSpec((1, D_MODEL), lambda i: (0, 0)),
        ],
        out_specs=pl.BlockSpec((r, D_MODEL), lambda i: (i, 0)),
        out_shape=jax.ShapeDtypeStruct((n, D_MODEL), F32),
        scratch_shapes=[pltpu.VMEM((r, D_FF), BF16),
                        pltpu.VMEM((r + 2 * HALO if seq > r else HALO, D_MODEL), BF16)],
        compiler_params=_cparams(("arbitrary",)),
        name="conv_mlp",
    )(xn2, xn2, xn2, x1, mod4, pw["w_up"], pw["ffn_conv_w"], pw["w_down"], fnorm)


def _rope_swap_perm():
    idx = np.arange(C_ROPE).reshape(2, 2, C_ROPE // 4)
    return idx[:, ::-1, :].reshape(-1)


def _split_w_in(w_in):
    depth = w_in.shape[0]
    swap = _rope_swap_perm()
    wb = lax.optimization_barrier(w_in.astype(BF16))
    kv = wb[:, :, OFF_KV:OFF_GATE]
    kr = kv[:, :, C_KV_LORA:]
    pad = jnp.zeros((depth, D_MODEL, KV_EXT - CACHE_W - C_ROPE), BF16)
    return wb, jnp.concatenate([kv, kr[:, :, swap], pad], axis=2), wb[:, :, OFF_GATE:]


def _pack_w_q_up(w):
    depth = w.shape[0]
    swap = _rope_swap_perm()
    w4 = w.astype(BF16).reshape(depth, C_Q_LORA, C_HEADS, C_NOPE + C_ROPE)
    rope = w4[..., C_NOPE:]
    return jnp.concatenate([w4, rope[..., swap]], axis=3).reshape(depth, C_Q_LORA, C_HEADS * LANES)


def _pack_w_kv_up(w):
    depth = w.shape[0]
    w4 = w.astype(BF16).reshape(depth, C_KV_LORA, C_HEADS, C_NOPE + C_V)
    kn = jnp.concatenate([w4[..., :C_NOPE], jnp.zeros((depth, C_KV_LORA, C_HEADS, LANES - C_NOPE), BF16)], axis=3)
    return jnp.concatenate(
        [kn.reshape(depth, C_KV_LORA, C_HEADS * LANES), w4[..., C_NOPE:].reshape(depth, C_KV_LORA, C_HEADS * C_V)],
        axis=2)


def _pad_axis(a, size, axis):
    shape = list(a.shape)
    shape[axis] = size - a.shape[axis]
    return jnp.concatenate([a, jnp.zeros(shape, a.dtype)], axis=axis)


def _rope_tables(seq, rotate):
    scale = (C_NOPE + C_ROPE) ** -0.5 * LOG2E
    n_freq = C_ROPE // 4
    if rotate:
        rows = seq // GRID_W
        r = jnp.repeat(jnp.arange(rows, dtype=F32), GRID_W)
        col = jnp.tile(jnp.arange(GRID_W, dtype=F32), rows)
        freq = ROPE_THETA ** (-jnp.arange(n_freq, dtype=F32) / n_freq)
        ang = jnp.stack([r[:, None] * freq, col[:, None] * freq], axis=1)
        cos, sin = jnp.cos(ang), jnp.sin(ang)
    else:
        cos = jnp.ones((seq, 2, n_freq), F32)
        sin = jnp.zeros((seq, 2, n_freq), F32)
    cos_full = jnp.stack([cos, cos], axis=2).reshape(seq, C_ROPE)
    sin_signed = jnp.stack([-sin, sin], axis=2).reshape(seq, C_ROPE)
    tabq = scale * jnp.concatenate([jnp.ones((seq, C_NOPE), F32), cos_full, sin_signed], axis=1)
    tabk = jnp.concatenate([cos_full, sin_signed, jnp.zeros((seq, LANES - 2 * C_ROPE), F32)], axis=1)
    return tabq, tabk


def _layer(x2d, batch, seq, layer, mod4, mod_row, pw, lb_all, s_init, cache, tabs, want_ctx_outputs,
           final, fnorm, prev_states=(), prev_caches=()):
    tabq, tabk = tabs
    res = _in_call(x2d, layer, mod4, mod_row, pw, lb_all, tabq, tabk, want_ctx_outputs, seq, prev_caches)
    q, gf, gb, kf, kb, v, ga, bb, u, qfull, kfull, vv, sg = res[:13]
    own_cache = None
    if want_ctx_outputs:
        own_cache = res[13] if prev_caches else res[13].reshape(batch, seq, CACHE_W)
    ya, states = _hgrn_call(q, gf, gb, kf, kb, v, ga, layer, pw["gnorm"], s_init, batch, seq, want_ctx_outputs,
                            prev_states)
    kc = vc = None
    if cache is not None:
        kc, vc = _cache_call(cache[0], cache[1], layer, pw["w_kv_up"])
    yc = _attn_call(qfull, kfull, vv, kc, vc, batch, seq)
    x1, xn2 = _merge_call(x2d, ya, yc, bb, u, sg, layer, mod4, mod_row, pw, seq)
    x2 = _ffn_call(xn2, x1, layer, mod4, mod_row, pw, fnorm, seq, final)
    return x2, states, own_cache


def kernel(x_prompt, x_sample, state_hgrn, cache_mla, c, c_ctx, w_ada, b_ada, norm1, w_in, hgrn_lb_logits, hgrn_gnorm, w_o_hgrn, conv_w, w_o_conv, mla_q_norm, w_q_up, mla_kv_norm, w_kv_up, w_o_mla, w_out, norm2, w_up, ffn_conv_w, w_down, final_norm):
    depth = w_in.shape[0]
    bp, sp, _ = x_prompt.shape
    bs, ss, _ = x_sample.shape
    assert 1 + bs <= MOD_ROWS
    for rows in (IN_ROWS, MERGE_ROWS, FFN_ROWS):
        assert (bp * sp) % rows == 0 and (bs * ss) % rows == 0
        assert (rows % sp == 0 or sp % rows == 0) and ss % rows == 0

    cvec = _pad_axis(jnp.concatenate([c_ctx[None, :], c], axis=0), MOD_ROWS, 0)
    mod4 = _ada_call(cvec, w_ada, b_ada).reshape(depth, MOD_ROWS, 6, D_MODEL)
    lb_all = _lb_call(hgrn_lb_logits)
    tabs_ctx = _rope_tables(IN_ROWS, rotate=False)
    tabs_lat = _rope_tables(ss, rotate=True)
    fnorm = final_norm.reshape(1, D_MODEL)
    cache = (cache_mla, _pad_axis(cache_mla[..., C_KV_LORA:], LANES, 3))
    w_in_a, w_in_kv, w_in_g = _split_w_in(w_in)

    def vec(a):
        return a.reshape(depth, 1, -1)

    pw = {
        "norm1": vec(norm1), "w_in_a": w_in_a, "w_in_kv": w_in_kv, "w_in_g": w_in_g, "gnorm": vec(hgrn_gnorm),
        "w_o_hgrn": w_o_hgrn.astype(BF16), "conv_w": _pad_axis(conv_w, SUBLANES, 1),
        "w_o_conv": w_o_conv.astype(BF16), "q_norm": vec(mla_q_norm), "w_q_up": _pack_w_q_up(w_q_up),
        "kv_norm": vec(mla_kv_norm), "w_kv_up": _pack_w_kv_up(w_kv_up), "w_o_mla": w_o_mla.astype(BF16),
        "w_out": w_out.astype(BF16), "norm2": vec(norm2), "w_up": w_up.astype(BF16),
        "ffn_conv_w": _pad_axis(ffn_conv_w, SUBLANES, 1), "w_down": w_down.astype(BF16),
    }

    xp = x_prompt.reshape(bp * sp, D_MODEL)
    xs = x_sample.reshape(bs * ss, D_MODEL)
    stack_in_kernel = depth > 1 and IN_ROWS % sp == 0
    new_states, new_caches = [], []
    for l in range(depth):
        final = l == depth - 1
        prev = (tuple(new_states), tuple(new_caches)) if (final and stack_in_kernel) else ((), ())
        xp, st, kvc = _layer(xp, bp, sp, l, mod4, lambda r: 0, pw, lb_all, None, None, tabs_ctx,
                             True, final, fnorm, *prev)
        new_states.append(st)
        new_caches.append(kvc)
        xs, _, _ = _layer(xs, bs, ss, l, mod4, lambda r: 1 + r // ss, pw, lb_all, state_hgrn,
                          cache, tabs_lat, False, final, fnorm)
    y_prompt = xp.reshape(bp, sp, D_MODEL)
    y_sample = xs.reshape(bs, ss, D_MODEL)
    if stack_in_kernel:
        return (y_prompt, y_sample, new_states[-1], new_caches[-1])
    return (y_prompt, y_sample, jnp.stack(new_states, axis=1), jnp.stack(new_caches, axis=1))
```

```python
import functools
import math

import numpy as np
import jax
import jax.numpy as jnp
from jax import lax
from jax.experimental import pallas as pl
from jax.experimental.pallas import tpu as pltpu

F32 = jnp.float32
BF16 = jnp.bfloat16

D_MODEL = 1024
GRID_W = 64
EPS = 1e-6
A_HEADS = 4
A_DK = 128
A_DV = 128
A_W = A_HEADS * A_DK
LB_FLOOR = 1e-30
B_W = 512
C_HEADS = 8
C_NOPE = 64
C_ROPE = 32
C_V = 64
C_Q_LORA = 384
C_KV_LORA = 256
ROPE_THETA = 10000.0
D_FF = 2816
N_BRANCH = 3
CACHE_W = C_KV_LORA + C_ROPE
LOG2E = 1.4426950408889634

LANES = 128
SUBLANES = 8
HALO = 16
VMEM_LIMIT_BYTES = 56 * 1024 * 1024

IN_ROWS = 512
IN_LOOKAHEAD = 1
HGRN_CHUNK = 128
HGRN_CHUNK_GROUP = 2
HGRN_VMEM_BUDGET = 32 * 1024 * 1024
ATTN_Q_ROWS = 512
ATTN_SEQ_ROWS = 1024
MERGE_ROWS = 512
FFN_ROWS = 1024
FFN_COLS = 256
MOD_ROWS = 16
ADA_COLS = 1536

OFF_CQ = 8 * A_W
OFF_KV = OFF_CQ + C_Q_LORA
KV_EXT = 3 * LANES
OFF_GATE = OFF_KV + CACHE_W


def _cparams(sem):
    return pltpu.CompilerParams(dimension_semantics=sem, vmem_limit_bytes=VMEM_LIMIT_BYTES)


def _dot(a, b):
    return jnp.dot(a, b, preferred_element_type=F32)


def _dot_nt(a, b):
    return lax.dot_general(a, b, (((1,), (1,)), ((), ())), preferred_element_type=F32)


def _dot_tn(a, b):
    return lax.dot_general(a, b, (((0,), (0,)), ((), ())), preferred_element_type=F32)


def _rms(x, g):
    ms = jnp.mean(x * x, axis=-1, keepdims=True)
    return x * lax.rsqrt(ms + EPS) * g


def _sigmoid_pair(z):
    t = jnp.exp(-jnp.abs(z))
    r = 1.0 / (1.0 + t)
    big, small = r, t * r
    pos = z >= 0
    return jnp.where(pos, big, small), jnp.where(pos, small, big)


def _sigmoid(z):
    return 1.0 / (1.0 + jnp.exp(-z))


def _silu(z):
    return z * _sigmoid(z)


def _layer_spec(shape, layer, grid_rank, single=True):
    nd = len(shape)
    if grid_rank == 1:
        imap = lambda i: (layer,) + (0,) * nd
    else:
        imap = lambda i, j: (layer,) + (0,) * nd
    if single:
        return pl.BlockSpec((None,) + tuple(shape), imap, pipeline_mode=pl.Buffered(1))
    return pl.BlockSpec((None,) + tuple(shape), imap)


def _mod_spec(layer, mod_row, rows, grid_rank=1):
    if grid_rank == 1:
        return pl.BlockSpec((None, None, 6, D_MODEL), lambda i: (layer, mod_row(i * rows), 0, 0))
    return pl.BlockSpec((None, None, 6, D_MODEL), lambda i, j: (layer, mod_row(i * rows), 0, 0))


def _ada_kernel(c_ref, w_ref, b_ref, o_ref):
    s = _silu(c_ref[...])
    o_ref[...] = jnp.dot(s, w_ref[...], preferred_element_type=F32,
                         precision=lax.Precision.HIGHEST) + b_ref[...]


def _ada_call(cvec, w_ada, b_ada):
    depth = w_ada.shape[0]
    ncol = w_ada.shape[2] // ADA_COLS
    return pl.pallas_call(
        _ada_kernel,
        grid=(depth, ncol),
        in_specs=[
            pl.BlockSpec((MOD_ROWS, D_MODEL), lambda l, j: (0, 0)),
            pl.BlockSpec((None, D_MODEL, ADA_COLS), lambda l, j: (l, 0, j)),
            pl.BlockSpec((None, 1, ADA_COLS), lambda l, j: (l, 0, j)),
        ],
        out_specs=pl.BlockSpec((None, MOD_ROWS, ADA_COLS), lambda l, j: (l, 0, j)),
        out_shape=jax.ShapeDtypeStruct((depth, MOD_ROWS, w_ada.shape[2]), F32),
        compiler_params=_cparams(("arbitrary", "arbitrary")),
        name="ada_mod",
    )(cvec, w_ada, b_ada.reshape(depth, 1, -1))


def _lb_kernel(x_ref, o_ref):
    depth = x_ref.shape[0]
    xs = [x_ref[l] for l in range(depth)]
    m = xs[0]
    for l in range(1, depth):
        m = jnp.maximum(m, xs[l])
    es = [jnp.exp(x - m) for x in xs]
    tot = es[0]
    for l in range(1, depth):
        tot = tot + es[l]
    ps = [e / tot for e in es]
    run = jnp.zeros_like(ps[0])
    for l in range(depth):
        run = run + ps[l]
        o_ref[l] = run - ps[0]


def _lb_call(logits):
    return pl.pallas_call(
        _lb_kernel,
        out_shape=jax.ShapeDtypeStruct(logits.shape, F32),
        name="hgrn_lb",
    )(logits)


def _place_rope_key(z3, tabk):
    tmp = z3 * tabk
    rot = tmp + pltpu.roll(tmp, LANES - C_ROPE, axis=1)
    lane = lax.broadcasted_iota(jnp.int32, rot.shape, 1)
    rot = jnp.where(lane < C_ROPE, rot, 0.0)
    return pltpu.roll(rot, C_NOPE, axis=1) + pltpu.roll(rot, C_NOPE + C_ROPE, axis=1)


def _in_kernel(*refs, n_prev, seq):
    (x_ref, mod_ref, n1_ref, w_ref, wkvd_ref, wg_ref, lb_ref, qn_ref, wq_ref, kvn_ref, wkv_ref,
     tabq_ref, tabk_ref) = refs[:13]
    prev_cache_refs = refs[13:13 + n_prev]
    (q_ref, gf_ref, gb_ref, kf_ref, kb_ref, v_ref, ga_ref, bb_ref, u_ref,
     qfull_ref, kfull_ref, vv_ref, sg_ref) = refs[13 + n_prev:26 + n_prev]
    cache_ref = refs[26 + n_prev:]
    x = x_ref[...]
    shift1 = mod_ref[0:1, :]
    scale1 = mod_ref[1:2, :]
    xn = _rms(x, n1_ref[...]) * (1.0 + scale1) + shift1
    xb = xn.astype(BF16)

    def proj(off, width):
        return lambda: _dot(xb, w_ref[:, off:off + width])

    env = {}
    stages = []

    def ep_q(z):
        q_ref[...] = (_silu(z) * (A_DK ** -0.5)).astype(BF16)
    stages.append((proj(0, A_W), ep_q))

    def ep_forget(d, g_ref, k_ref):
        def ep(z):
            lb = lb_ref[d:d + 1, :]
            sp, sn = _sigmoid_pair(z)
            g_ref[...] = jnp.log(jnp.maximum(lb, LB_FLOOR) + (1.0 - lb) * sp) * LOG2E
            k_ref[...] = ((1.0 - lb) * sn - jnp.maximum(LB_FLOOR - lb, 0.0)).astype(BF16)
        return ep
    stages.append((proj(1 * A_W, A_W), ep_forget(0, gf_ref, kf_ref)))
    stages.append((proj(2 * A_W, A_W), ep_forget(1, gb_ref, kb_ref)))

    def ep_v(z):
        v_ref[...] = z.astype(BF16)
    stages.append((proj(3 * A_W, A_W), ep_v))

    def ep_ga(z):
        ga_ref[...] = _silu(z).astype(BF16)
    stages.append((proj(4 * A_W, A_W), ep_ga))

    def ep_bb(z):
        bb_ref[...] = z.astype(BF16)
    stages.append((proj(5 * A_W, B_W), ep_bb))

    def ep_cb(z):
        env["conv_c"] = z
    stages.append((proj(6 * A_W, B_W), ep_cb))

    def ep_hb(z):
        u_ref[...] = (env["conv_c"] * z).astype(BF16)
    stages.append((proj(7 * A_W, B_W), ep_hb))

    def ep_cq(z):
        env["cqn"] = _rms(z, qn_ref[...]).astype(BF16)
    stages.append((proj(OFF_CQ, C_Q_LORA), ep_cq))

    def ep_kv(z):
        ckvn = _rms(z[:, :C_KV_LORA], kvn_ref[...])
        z3 = z[:, C_KV_LORA:]
        if cache_ref and n_prev == 0:
            cache_ref[0][:, :C_KV_LORA] = ckvn
            cache_ref[0][:, C_KV_LORA:] = z3[:, :C_ROPE]
        elif cache_ref:
            for s in range(x_ref.shape[0] // seq):
                rows = slice(s * seq, (s + 1) * seq)
                for l, prev in enumerate(prev_cache_refs):
                    cache_ref[0][s, l] = prev[s]
                cache_ref[0][s, n_prev, :, :C_KV_LORA] = ckvn[rows, :]
                cache_ref[0][s, n_prev, :, C_KV_LORA:] = z3[rows, :C_ROPE]
        krr = _place_rope_key(z3, tabk_ref[...])
        env["krr"] = jnp.concatenate([krr, krr], axis=1)
        env["cb"] = ckvn.astype(BF16)
    stages.append((lambda: _dot(xb, wkvd_ref[...]), ep_kv))

    def gate_stage(j):
        def ep(z):
            sg_ref[:, j * A_W:(j + 1) * A_W] = _sigmoid(z).astype(BF16)
        return (lambda: _dot(xb, wg_ref[:, j * A_W:(j + 1) * A_W]), ep)

    def q_up_stage(h):
        def ep(z):
            tabq = jnp.concatenate([tabq_ref[...]] * 2, axis=1)
            qfull_ref[:, h * LANES:(h + 2) * LANES] = (z * tabq).astype(BF16)
        return (lambda: _dot(env["cqn"], wq_ref[:, h * LANES:(h + 2) * LANES]), ep)

    def k_up_stage(h):
        def ep(z):
            kfull_ref[:, h * LANES:(h + 2) * LANES] = (z + env["krr"]).astype(BF16)
        return (lambda: _dot(env["cb"], wkv_ref[:, h * LANES:(h + 2) * LANES]), ep)

    def ep_vv(z):
        vv_ref[...] = z.astype(BF16)
    v_up_stage = (lambda: _dot(env["cb"], wkv_ref[:, C_HEADS * LANES:]), ep_vv)

    stages = stages[8:] + stages[:8]
    stages += [gate_stage(0), q_up_stage(0), q_up_stage(2), gate_stage(1), q_up_stage(4), q_up_stage(6),
               gate_stage(2), k_up_stage(0), k_up_stage(2), gate_stage(3), k_up_stage(4), k_up_stage(6),
               gate_stage(4), v_up_stage, gate_stage(5)]

    pending = [stages[j][0]() for j in range(IN_LOOKAHEAD)]
    for j, (_, epilogue) in enumerate(stages):
        if j + IN_LOOKAHEAD < len(stages):
            pending.append(stages[j + IN_LOOKAHEAD][0]())
        epilogue(pending.pop(0))


def _in_call(x2d, layer, mod4, mod_row, pw, lb_all, tabq, tabk, want_cache, seq, prev_caches=()):
    n = x2d.shape[0]
    tm = IN_ROWS
    tab_blocks = tabq.shape[0] // tm
    n_prev = len(prev_caches) if want_cache else 0

    def row(width):
        return pl.BlockSpec((tm, width), lambda i: (i, 0))

    outs = [
        (A_W, BF16),
        (A_W, F32), (A_W, F32),
        (A_W, BF16), (A_W, BF16),
        (A_W, BF16),
        (A_W, BF16),
        (B_W, BF16),
        (B_W, BF16),
        (C_HEADS * LANES, BF16),
        (C_HEADS * LANES, BF16),
        (C_HEADS * C_V, BF16),
        (N_BRANCH * D_MODEL, BF16),
    ]
    out_specs = [row(w) for w, _ in outs]
    out_shape = [jax.ShapeDtypeStruct((n, w), dt) for w, dt in outs]
    prev_specs = []
    if want_cache and n_prev == 0:
        out_specs.append(row(CACHE_W))
        out_shape.append(jax.ShapeDtypeStruct((n, CACHE_W), F32))
    elif want_cache:
        per = tm // seq
        prev_specs = [pl.BlockSpec((per, seq, CACHE_W), lambda i: (i, 0, 0)) for _ in prev_caches]
        out_specs.append(pl.BlockSpec((per, n_prev + 1, seq, CACHE_W), lambda i: (i, 0, 0, 0)))
        out_shape.append(jax.ShapeDtypeStruct((n // seq, n_prev + 1, seq, CACHE_W), F32))
    return pl.pallas_call(
        functools.partial(_in_kernel, n_prev=n_prev, seq=seq),
        grid=(n // tm,),
        in_specs=[
            row(D_MODEL),
            _mod_spec(layer, mod_row, tm),
            _layer_spec((1, D_MODEL), layer, 1),
            _layer_spec((D_MODEL, OFF_KV), layer, 1),
            _layer_spec((D_MODEL, KV_EXT), layer, 1),
            _layer_spec((D_MODEL, N_BRANCH * D_MODEL), layer, 1),
            _layer_spec((2, A_W), layer, 1),
            _layer_spec((1, C_Q_LORA), layer, 1),
            _layer_spec((C_Q_LORA, C_HEADS * LANES), layer, 1),
            _layer_spec((1, C_KV_LORA), layer, 1),
            _layer_spec((C_KV_LORA, C_HEADS * LANES + C_HEADS * C_V), layer, 1),
            pl.BlockSpec((tm, LANES), lambda i: (i % tab_blocks, 0)),
            pl.BlockSpec((tm, LANES), lambda i: (i % tab_blocks, 0)),
        ] + prev_specs,
        out_specs=out_specs,
        out_shape=out_shape,
        compiler_params=_cparams(("arbitrary",)),
        name="in_proj",
    )(x2d, mod4, pw["norm1"], pw["w_in_a"], pw["w_in_kv"], pw["w_in_g"], lb_all, pw["q_norm"], pw["w_q_up"],
      pw["kv_norm"], pw["w_kv_up"], tabq, tabk, *(prev_caches if n_prev else ()))


def _cache_kernel(c_ref, kr_ref, wkv_ref, kfull_ref, vv_ref):
    cb = c_ref[:, :C_KV_LORA].astype(BF16)
    kr = kr_ref[...]
    krr = pltpu.roll(kr, C_NOPE, axis=1) + pltpu.roll(kr, C_NOPE + C_ROPE, axis=1)
    krr = jnp.concatenate([krr, krr], axis=1)
    for h in range(0, C_HEADS, 2):
        kz = _dot(cb, wkv_ref[:, h * LANES:(h + 2) * LANES])
        kfull_ref[:, h * LANES:(h + 2) * LANES] = (kz + krr).astype(BF16)
    vv_ref[...] = _dot(cb, wkv_ref[:, C_HEADS * LANES:]).astype(BF16)


def _cache_call(cache, cache_kr, layer, wkv):
    b, _, p, w = cache.shape
    return pl.pallas_call(
        _cache_kernel,
        grid=(b,),
        in_specs=[
            pl.BlockSpec((None, None, p, w), lambda i: (i, layer, 0, 0)),
            pl.BlockSpec((None, None, p, LANES), lambda i: (i, layer, 0, 0)),
            _layer_spec(wkv.shape[1:], layer, 1, single=False),
        ],
        out_specs=[
            pl.BlockSpec((None, p, C_HEADS * LANES), lambda i: (i, 0, 0)),
            pl.BlockSpec((None, p, C_HEADS * C_V), lambda i: (i, 0, 0)),
        ],
        out_shape=[
            jax.ShapeDtypeStruct((b, p, C_HEADS * LANES), BF16),
            jax.ShapeDtypeStruct((b, p, C_HEADS * C_V), BF16),
        ],
        compiler_params=_cparams(("arbitrary",)),
        name="cache_expand",
    )(cache, cache_kr, wkv)


def _level_matrix(c, rev):
    t = np.arange(c)[:, None]
    s = np.arange(c)[None, :]
    lev = np.floor(np.log2(np.maximum(np.bitwise_xor(t, s), 1))).astype(np.int32)
    lev = np.where(s == t, -1, lev)
    lev = np.where((s < t) if rev else (s > t), -2, lev)
    return lev.astype(np.int32)


def _prefix_matrix(c, rev):
    t = np.arange(c)[:, None]
    s = np.arange(c)[None, :]
    incl = ((s >= t) if rev else (s <= t)).astype(np.float32)
    return np.concatenate([incl, incl], axis=1)


def _level_exponent(b, g, h, rev):
    c, dk = b.shape
    ref = h if rev else h - 1
    if 2 * h > SUBLANES:
        parts = []
        for gs in range(0, c, 2 * h):
            beta = jnp.broadcast_to(b[gs + ref:gs + ref + 1, :], (h, dk))
            lo = b[gs:gs + h, :]
            hi = b[gs + h:gs + 2 * h, :]
            parts += [lo - beta, beta - hi] if rev else [beta - lo, hi - beta]
        return jnp.concatenate(parts, axis=0)
    row = lax.broadcasted_iota(jnp.int32, (SUBLANES, dk), 0)
    upper = (row % (2 * h)) >= h
    keep = jnp.logical_not(upper) if rev else upper
    if h == 1:
        parts = [jnp.where(keep, g[j:j + SUBLANES, :], 0.0) for j in range(0, c, SUBLANES)]
        return jnp.concatenate(parts, axis=0)
    sign = jnp.where(keep, 1.0, -1.0)
    parts = []
    for j in range(0, c, SUBLANES):
        bj = b[j:j + SUBLANES, :]
        beta = jnp.broadcast_to(bj[ref:ref + 1, :], (SUBLANES, dk))
        for gs in range(2 * h, SUBLANES, 2 * h):
            beta = jnp.where(row >= gs, jnp.broadcast_to(bj[gs + ref:gs + ref + 1, :], (SUBLANES, dk)), beta)
        parts.append((bj - beta) * sign)
    return jnp.concatenate(parts, axis=0)


def _level_halves(c, h):
    return [(slice(gs, gs + h), slice(gs + h, gs + 2 * h)) for gs in range(0, c, 2 * h)]


def _level_operands(q, k, qf, kf, x, h, rev):
    c = q.shape[0]
    if h < SUBLANES:
        return (qf * x).astype(BF16), (kf * x).astype(BF16)
    packed = h >= HALO
    qparts, kparts = [], []
    for lo, hi in _level_halves(c, h):
        qs, ks = (lo, hi) if rev else (hi, lo)
        q_sc, k_sc = qf[qs, :] * x[qs, :], kf[ks, :] * x[ks, :]
        if packed:
            q_sc, k_sc = q_sc.astype(BF16), k_sc.astype(BF16)
            q_raw, k_raw = q[ks, :], k[qs, :]
        else:
            q_raw, k_raw = qf[ks, :], kf[qs, :]
        qparts += [q_sc, q_raw] if rev else [q_raw, q_sc]
        kparts += [k_raw, k_sc] if rev else [k_sc, k_raw]
    qh, kh = jnp.concatenate(qparts, axis=0), jnp.concatenate(kparts, axis=0)
    return (qh, kh) if packed else (qh.astype(BF16), kh.astype(BF16))


def _level_select(a, p, lev, hl, rev):
    c = a.shape[0]
    h = 1 << hl
    if h < SUBLANES:
        return jnp.where(lev == hl, p, a)
    parts = []
    for lo, hi in _level_halves(c, h):
        qs, other = (lo, hi) if rev else (hi, lo)
        sel = jnp.where(lev[qs, :] == hl, p[qs, :], a[qs, :])
        parts += [sel, a[other, :]] if rev else [a[other, :], sel]
    return jnp.concatenate(parts, axis=0)


def _hgrn_intra(streams, lev, pmat):
    c = streams[0][0].shape[0]
    bs = []
    for q, k, g, v, rev in streams:
        g_hi = g.astype(BF16)
        g_lo = (g - g_hi.astype(F32)).astype(BF16)
        bs.append(_dot(pmat[rev], jnp.concatenate([g_hi, g_lo], axis=0)))
    qfs = [s[0].astype(F32) for s in streams]
    kfs = [s[1].astype(F32) for s in streams]
    a_s = [jnp.zeros((c, c), F32) for _ in streams]
    for hl in range(int(np.log2(c))):
        ps = []
        for (q, k, g, v, rev), b, qf, kf in zip(streams, bs, qfs, kfs):
            x = jnp.exp2(_level_exponent(b, g, 1 << hl, rev))
            ps.append(_dot_nt(*_level_operands(q, k, qf, kf, x, 1 << hl, rev)))
        a_s = [_level_select(a, p, lev[s[4]], hl, s[4]) for s, p, a in zip(streams, ps, a_s)]
    ps = [_dot_nt(s[0], s[1]) for s in streams]
    a_s = [jnp.where(lev[s[4]] == -1, p, a).astype(BF16) for s, p, a in zip(streams, ps, a_s)]
    return bs, a_s, qfs, kfs


def _hgrn_inter(streams, intra, states):
    c = streams[0][0].shape[0]
    bs, a_s, qfs, kfs = intra
    outs = []
    for (q, k, g, v, rev), b, a, qf, st in zip(streams, bs, a_s, qfs, states):
        outs.append(_dot(a, v) + _dot_nt((qf * jnp.exp2(b)).astype(BF16), st.astype(BF16)))
    new_states = []
    for (q, k, g, v, rev), b, kf, st in zip(streams, bs, kfs, states):
        tot = b[0:1, :] if rev else b[c - 1:c, :]
        ks = (kf * jnp.exp2(tot - b)).astype(BF16)
        new_states.append(st * jnp.exp2(tot) + _dot_tn(v, ks))
    return outs, new_states


def _hgrn_kernel(*refs, has_init, want_state, n_prev):
    refs = list(refs)
    (q_ref, gf_ref, gb_ref, kf_ref, kb_ref, v_ref, ga_ref, gn_ref,
     levf_ref, levb_ref, pmf_ref, pmb_ref) = refs[:12]
    pos = 12
    s0_ref = None
    if has_init:
        s0_ref = refs[pos]
        pos += 1
    prev_state_refs = refs[pos:pos + n_prev]
    pos += n_prev
    y_ref = refs[pos]
    pos += 1
    sfin_ref = None
    if want_state:
        sfin_ref = refs[pos]
        pos += 1
    o_scr, st_scr = refs[pos], refs[pos + 1]

    t = q_ref.shape[0]
    c = HGRN_CHUNK
    n = t // c
    hp = q_ref.shape[1] // A_DK
    for d in range(2):
        for h in range(hp):
            st_scr[d, h] = s0_ref[d, h].T if has_init else jnp.zeros((A_DV, A_DK), F32)
    lev = (levf_ref, levb_ref)
    pmat = (pmf_ref, pmb_ref)
    g_refs = (gf_ref, gb_ref)
    k_refs = (kf_ref, kb_ref)

    group = math.gcd(n, HGRN_CHUNK_GROUP)

    def body(i, carry):
        levs = (lev[0][...], lev[1][...])
        pmats = (pmat[0][...], pmat[1][...])
        steps = []
        for u in range(group):
            streams, rows = [], []
            for d in range(2):
                ci = i * group + u if d == 0 else n - 1 - (i * group + u)
                sl = pl.ds(pl.multiple_of(ci * c, c), c)
                rows.append(sl)
                for h in range(hp):
                    cs = slice(h * A_DK, (h + 1) * A_DK)
                    streams.append((q_ref[sl, cs], k_refs[d][sl, cs], g_refs[d][sl, cs], v_ref[sl, cs], d))
            steps.append((streams, rows))
        all_streams = [s for streams, _ in steps for s in streams]
        bs, a_s, qfs, kfs = _hgrn_intra(all_streams, levs, pmats)
        per = 2 * hp
        states = [st_scr[d, h] for d in range(2) for h in range(hp)]
        for u, (streams, rows) in enumerate(steps):
            part = slice(u * per, (u + 1) * per)
            outs, states = _hgrn_inter(streams, (bs[part], a_s[part], qfs[part], kfs[part]), states)
            for idx, o in enumerate(outs):
                d, h = divmod(idx, hp)
                o_scr[d, rows[d], h * A_DV:(h + 1) * A_DV] = o
        for idx, st_new in enumerate(states):
            d, h = divmod(idx, hp)
            st_scr[d, h] = st_new
        return carry

    lax.fori_loop(0, n // group, body, 0)
    for l, prev in enumerate(prev_state_refs):
        sfin_ref[l] = prev[...]
    for h in range(hp):
        cs = slice(h * A_DV, (h + 1) * A_DV)
        if want_state:
            for d in range(2):
                if n_prev:
                    sfin_ref[n_prev, d, h] = st_scr[d, h].T
                else:
                    sfin_ref[d, h] = st_scr[d, h].T
        o = o_scr[0, :, cs] + o_scr[1, :, cs]
        y_ref[:, cs] = (_rms(o, gn_ref[...]) * ga_ref[:, cs]).astype(BF16)


def _hgrn_heads_per_step(seq):
    per_head_row = 2 * (5 * 2 + 2 * 4) * A_DK + 2 * 2 * A_DV + 2 * 4 * A_DV
    hp = A_HEADS
    while hp > 1 and hp * seq * per_head_row > HGRN_VMEM_BUDGET:
        hp //= 2
    return hp


def _hgrn_call(q, gf, gb, kf, kb, v, ga, layer, gnorm, s_init, batch, seq, want_state, prev_states=()):
    c = HGRN_CHUNK
    n_prev = len(prev_states) if want_state else 0
    hp = _hgrn_heads_per_step(seq)
    consts =[jnp.asarray(_level_matrix(c, False)), jnp.asarray(_level_matrix(c, True)),
              jnp.asarray(_prefix_matrix(c, False), BF16), jnp.asarray(_prefix_matrix(c, True), BF16)]

    def r3(a):
        return a.reshape(batch, seq, A_W)

    col = pl.BlockSpec((None, seq, hp * A_DK), lambda b, h: (b, 0, h))
    in_specs = [col] * 7 + [_layer_spec((1, A_DV), layer, 2, single=False)]
    in_specs += [pl.BlockSpec(m.shape, lambda b, h: (0, 0)) for m in consts]
    args = [r3(q), r3(gf), r3(gb), r3(kf), r3(kb), r3(v), r3(ga), gnorm] + consts
    has_init = s_init is not None
    if has_init:
        in_specs.append(pl.BlockSpec((None, None, 2, hp, A_DK, A_DV), lambda b, h: (b, layer, 0, h, 0, 0)))
        args.append(s_init)
    st_spec = pl.BlockSpec((None, 2, hp, A_DK, A_DV), lambda b, h: (b, 0, h, 0, 0))
    for prev in (prev_states if n_prev else ()):
        in_specs.append(st_spec)
        args.append(prev)
    out_specs = [col]
    out_shape = [jax.ShapeDtypeStruct((batch, seq, A_W), BF16)]
    if want_state and n_prev:
        out_specs.append(pl.BlockSpec((None, n_prev + 1, 2, hp, A_DK, A_DV), lambda b, h: (b, 0, 0, h, 0, 0)))
        out_shape.append(jax.ShapeDtypeStruct((batch, n_prev + 1, 2, A_HEADS, A_DK, A_DV), F32))
    elif want_state:
        out_specs.append(st_spec)
        out_shape.append(jax.ShapeDtypeStruct((batch, 2, A_HEADS, A_DK, A_DV), F32))
    res = pl.pallas_call(
        functools.partial(_hgrn_kernel, has_init=has_init, want_state=want_state, n_prev=n_prev),
        grid=(batch, A_HEADS // hp),
        in_specs=in_specs,
        out_specs=out_specs,
        out_shape=out_shape,
        scratch_shapes=[pltpu.VMEM((2, seq, hp * A_DV), F32), pltpu.VMEM((2, hp, A_DV, A_DK), F32)],
        compiler_params=_cparams(("arbitrary", "arbitrary")),
        name="hgrn",
    )(*args)
    y = res[0].reshape(batch * seq, A_W)
    return y, (res[1] if want_state else None)


def _attn_kernel(*refs, has_cache):
    if has_cache:
        q_ref, k_ref, v_ref, kc_ref, vc_ref, y_ref = refs
    else:
        q_ref, k_ref, v_ref, y_ref = refs
    nseq = q_ref.shape[0]
    lane = lax.broadcasted_iota(jnp.int32, (q_ref.shape[1], LANES), 1)

    def scores(s, h):
        hs = slice(h * LANES, (h + 1) * LANES)
        qh = q_ref[s, :, hs]
        s1 = _dot_nt(qh, k_ref[s, :, hs])
        s2 = _dot_nt(qh, kc_ref[s, :, hs]) if has_cache else None
        return s1, s2

    def attend(s, h, s1, s2):
        vs = slice((h // 2) * LANES, (h // 2 + 1) * LANES)
        m = jnp.max(s1, axis=-1, keepdims=True)
        if has_cache:
            m = jnp.maximum(m, jnp.max(s2, axis=-1, keepdims=True))
        p1 = jnp.exp2(s1 - m)
        l = jnp.sum(p1, axis=-1, keepdims=True)
        o = _dot(p1.astype(BF16), v_ref[s, :, vs])
        if has_cache:
            p2 = jnp.exp2(s2 - m)
            l = l + jnp.sum(p2, axis=-1, keepdims=True)
            o = o + _dot(p2.astype(BF16), vc_ref[s, :, vs])
        return o / l

    items = [(s, h) for s in range(nseq) for h in range(C_HEADS)]
    pending = scores(*items[0])
    prev_out = None
    for idx, (s, h) in enumerate(items):
        nxt = scores(*items[idx + 1]) if idx + 1 < len(items) else None
        out = attend(s, h, *pending)
        pending = nxt
        if h % 2 == 1:
            vs = slice((h // 2) * LANES, (h // 2 + 1) * LANES)
            y_ref[s, :, vs] = jnp.where(lane < C_V, prev_out, out).astype(BF16)
        prev_out = out


def _attn_call(qfull, kfull, vv, kc, vc, batch, seq):
    tq = min(ATTN_Q_ROWS, seq)
    nb = math.gcd(batch, max(1, ATTN_SEQ_ROWS // seq)) if tq == seq else 1
    has_cache = kc is not None
    kw = C_HEADS * LANES
    vw = C_HEADS * C_V
    in_specs = [
        pl.BlockSpec((nb, tq, kw), lambda b, i: (b, i, 0)),
        pl.BlockSpec((nb, seq, kw), lambda b, i: (b, 0, 0)),
        pl.BlockSpec((nb, seq, vw), lambda b, i: (b, 0, 0)),
    ]
    args = [qfull.reshape(batch, seq, kw), kfull.reshape(batch, seq, kw), vv.reshape(batch, seq, vw)]
    if has_cache:
        p = kc.shape[1]
        in_specs += [
            pl.BlockSpec((nb, p, kw), lambda b, i: (b, 0, 0)),
            pl.BlockSpec((nb, p, vw), lambda b, i: (b, 0, 0)),
        ]
        args += [kc, vc]
    y = pl.pallas_call(
        functools.partial(_attn_kernel, has_cache=has_cache),
        grid=(batch // nb, seq // tq),
        in_specs=in_specs,
        out_specs=pl.BlockSpec((nb, tq, vw), lambda b, i: (b, i, 0)),
        out_shape=jax.ShapeDtypeStruct((batch, seq, vw), BF16),
        compiler_params=_cparams(("arbitrary", "arbitrary")),
        name="mla_attn",
    )(*args)
    return y.reshape(batch * seq, vw)


def _conv3_rows(u, prev_row, next_row, cw_ref, seq):
    r, w = u.shape
    sub = min(seq, r)
    whole = seq <= r
    i8 = lax.broadcasted_iota(jnp.int32, (SUBLANES, w), 0)
    w0, w1, w2 = cw_ref[0:1, :], cw_ref[1:2, :], cw_ref[2:3, :]
    outs = []
    for s in range(r // sub):
        blk = u[s * sub:(s + 1) * sub, :]
        p = 0.0 if whole else prev_row
        n = 0.0 if whole else next_row
        up = pltpu.roll(blk, 1, axis=0)
        dn = pltpu.roll(blk, sub - 1, axis=0)
        up = jnp.concatenate([jnp.where(i8 == 0, p, up[:SUBLANES, :]), up[SUBLANES:, :]], axis=0)
        dn = jnp.concatenate([dn[:sub - SUBLANES, :], jnp.where(i8 == SUBLANES - 1, n, dn[sub - SUBLANES:, :])],
                             axis=0)
        outs.append(up * w0 + blk * w1 + dn * w2)
    return outs[0] if len(outs) == 1 else jnp.concatenate(outs, axis=0)


def _seq_edges(i, rows, seq):
    first = i * rows
    return first % seq == 0, (first + rows) % seq == 0


def _merge_kernel(x_ref, ya_ref, yc_ref, bb_ref, u_ref, up_ref, un_ref, sg_ref, mod_ref,
                  wa_ref, wb_ref, wc_ref, wout_ref, cw_ref, n2_ref, x1_ref, xn2_ref, *, seq):
    tm = x_ref.shape[0]
    at_start, at_end = _seq_edges(pl.program_id(0), tm, seq)
    prev_row = jnp.where(at_start, 0.0, up_ref[HALO - 1:HALO, :].astype(F32))
    next_row = jnp.where(at_end, 0.0, un_ref[0:1, :].astype(F32))
    conv = _conv3_rows(u_ref[...].astype(F32), prev_row, next_row, cw_ref, seq)
    yb = (bb_ref[...] * conv).astype(BF16)
    half = tm // 2
    hs = []
    for s in range(2):
        rs = slice(s * half, (s + 1) * half)
        pa = _dot(ya_ref[rs, :], wa_ref[...])
        pc = _dot(yc_ref[rs, :], wc_ref[...])
        pb = _dot(yb[rs, :], wb_ref[...])
        hs.append((sg_ref[rs, 0:D_MODEL] * pa + sg_ref[rs, D_MODEL:2 * D_MODEL] * pb
                   + sg_ref[rs, 2 * D_MODEL:3 * D_MODEL] * pc).astype(BF16))
    gate1 = mod_ref[2:3, :]
    shift2 = mod_ref[3:4, :]
    scale2 = mod_ref[4:5, :]
    for s in range(2):
        rs = slice(s * half, (s + 1) * half)
        x1 = x_ref[rs, :] + gate1 * _dot(hs[s], wout_ref[...])
        x1_ref[rs, :] = x1
        xn2_ref[rs, :] = (_rms(x1, n2_ref[...]) * (1.0 + scale2) + shift2).astype(BF16)


def _merge_call(x2d, ya, yc, bb, u, sg, layer, mod4, mod_row, pw, seq):
    n = x2d.shape[0]
    tm = MERGE_ROWS
    nbh = n // HALO
    per = tm // HALO

    def row(width):
        return pl.BlockSpec((tm, width), lambda i: (i, 0))

    def wspec(name):
        return _layer_spec(pw[name].shape[1:], layer, 1)

    return pl.pallas_call(
        functools.partial(_merge_kernel, seq=seq),
        grid=(n // tm,),
        in_specs=[
            row(D_MODEL), row(A_W), row(C_HEADS * C_V), row(B_W), row(B_W),
            pl.BlockSpec((HALO, B_W), lambda i: (jnp.maximum(i * per - 1, 0), 0)),
            pl.BlockSpec((HALO, B_W), lambda i: (jnp.minimum((i + 1) * per, nbh - 1), 0)),
            row(N_BRANCH * D_MODEL),
            _mod_spec(layer, mod_row, tm),
            wspec("w_o_hgrn"), wspec("w_o_conv"), wspec("w_o_mla"), wspec("w_out"), wspec("conv_w"),
            wspec("norm2"),
        ],
        out_specs=[row(D_MODEL), row(D_MODEL)],
        out_shape=[jax.ShapeDtypeStruct((n, D_MODEL), F32), jax.ShapeDtypeStruct((n, D_MODEL), BF16)],
        compiler_params=_cparams(("arbitrary",)),
        name="merge",
    )(x2d, ya, yc, bb, u, u, u, sg, mod4, pw["w_o_hgrn"], pw["w_o_conv"], pw["w_o_mla"], pw["w_out"],
      pw["conv_w"], pw["norm2"])


def _ffn_kernel(xn_ref, xp_ref, xx_ref, x1_ref, mod_ref, wup_ref, cw_ref, wd_ref,
                fn_ref, out_ref, h_scr, lhs_scr, *, seq, final):
    r = xn_ref.shape[0]
    tn = FFN_COLS
    at_start, at_end = _seq_edges(pl.program_id(0), r, seq)
    need_halo = seq > r
    if need_halo:
        lhs_scr[0:HALO, :] = xp_ref[...]
        lhs_scr[HALO:HALO + r, :] = xn_ref[...]
        lhs_scr[HALO + r:, :] = xx_ref[...]
        xn = lhs_scr[...]
    else:
        xn = xn_ref[...]

    def up(off):
        return _dot(xn, wup_ref[:, off:off + tn])

    def conv_part(um, off):
        prev_row = next_row = None
        if need_halo:
            prev_row = jnp.where(at_start, 0.0, um[HALO - 1:HALO, :])
            next_row = jnp.where(at_end, 0.0, um[HALO + r:HALO + r + 1, :])
            um = um[HALO:HALO + r, :]
        return _conv3_rows(um, prev_row, next_row, cw_ref.at[:, off:off + tn], seq)

    offs = [o for j in range(D_FF // tn) for o in (j * tn, D_FF + j * tn)]
    pending = up(offs[0])
    a = None
    for idx, off in enumerate(offs):
        nxt = up(offs[idx + 1]) if idx + 1 < len(offs) else None
        conv = conv_part(pending, off)
        if off < D_FF:
            a = conv
        else:
            j = (off - D_FF) // tn
            h_scr[:, j * tn:(j + 1) * tn] = (_silu(a) * conv).astype(BF16)
        pending = nxt

    gate2 = mod_ref[5:6, :]
    y = x1_ref[...] + gate2 * _dot(h_scr[...], wd_ref[...])
    if final:
        y = _rms(y, fn_ref[...])
    out_ref[...] = y


def _ffn_call(xn2, x1, layer, mod4, mod_row, pw, fnorm, seq, final):
    n = xn2.shape[0]
    r = FFN_ROWS
    nbh = n // HALO
    per = r // HALO
    return pl.pallas_call(
        functools.partial(_ffn_kernel, seq=seq, final=final),
        grid=(n // r,),
        in_specs=[
            pl.BlockSpec((r, D_MODEL), lambda i: (i, 0)),
            pl.BlockSpec((HALO, D_MODEL), lambda i: (jnp.maximum(i * per - 1, 0), 0)),
            pl.BlockSpec((HALO, D_MODEL), lambda i: (jnp.minimum((i + 1) * per, nbh - 1), 0)),
            pl.BlockSpec((r, D_MODEL), lambda i: (i, 0)),
            _mod_spec(layer, mod_row, r),
            _layer_spec(pw["w_up"].shape[1:], layer, 1),
            _layer_spec(pw["ffn_conv_w"].shape[1:], layer, 1),
            _layer_spec(pw["w_down"].shape[1:], layer, 1),
            pl.BlockSpec((1, D_MODEL), lambda i: (0, 0)),
        ],
        out_specs=pl.BlockSpec((r, D_MODEL), lambda i: (i, 0)),
        out_shape=jax.ShapeDtypeStruct((n, D_MODEL), F32),
        scratch_shapes=[pltpu.VMEM((r, D_FF), BF16),
                        pltpu.VMEM((r + 2 * HALO if seq > r else HALO, D_MODEL), BF16)],
        compiler_params=_cparams(("arbitrary",)),
        name="conv_mlp",
    )(xn2, xn2, xn2, x1, mod4, pw["w_up"], pw["ffn_conv_w"], pw["w_down"], fnorm)


def _rope_swap_perm():
    idx = np.arange(C_ROPE).reshape(2, 2, C_ROPE // 4)
    return idx[:, ::-1, :].reshape(-1)


def _split_w_in(w_in):
    depth = w_in.shape[0]
    swap = _rope_swap_perm()
    kv = w_in[:, :, OFF_KV:OFF_GATE].astype(BF16)
    kr = kv[:, :, C_KV_LORA:]
    pad = jnp.zeros((depth, D_MODEL, KV_EXT - CACHE_W - C_ROPE), BF16)
    return (w_in[:, :, :OFF_KV].astype(BF16), jnp.concatenate([kv, kr[:, :, swap], pad], axis=2),
            w_in[:, :, OFF_GATE:].astype(BF16))


def _pack_w_q_up(w):
    depth = w.shape[0]
    swap = _rope_swap_perm()
    w4 = w.astype(BF16).reshape(depth, C_Q_LORA, C_HEADS, C_NOPE + C_ROPE)
    rope = w4[..., C_NOPE:]
    return jnp.concatenate([w4, rope[..., swap]], axis=3).reshape(depth, C_Q_LORA, C_HEADS * LANES)


def _pack_w_kv_up(w):
    depth = w.shape[0]
    w4 = w.astype(BF16).reshape(depth, C_KV_LORA, C_HEADS, C_NOPE + C_V)
    kn = jnp.concatenate([w4[..., :C_NOPE], jnp.zeros((depth, C_KV_LORA, C_HEADS, LANES - C_NOPE), BF16)], axis=3)
    return jnp.concatenate(
        [kn.reshape(depth, C_KV_LORA, C_HEADS * LANES), w4[..., C_NOPE:].reshape(depth, C_KV_LORA, C_HEADS * C_V)],
        axis=2)


def _pad_axis(a, size, axis):
    shape = list(a.shape)
    shape[axis] = size - a.shape[axis]
    return jnp.concatenate([a, jnp.zeros(shape, a.dtype)], axis=axis)


def _rope_tables(seq, rotate):
    scale = (C_NOPE + C_ROPE) ** -0.5 * LOG2E
    n_freq = C_ROPE // 4
    if rotate:
        rows = seq // GRID_W
        r = jnp.repeat(jnp.arange(rows, dtype=F32), GRID_W)
        col = jnp.tile(jnp.arange(GRID_W, dtype=F32), rows)
        freq = ROPE_THETA ** (-jnp.arange(n_freq, dtype=F32) / n_freq)
        ang = jnp.stack([r[:, None] * freq, col[:, None] * freq], axis=1)
        cos, sin = jnp.cos(ang), jnp.sin(ang)
    else:
        cos = jnp.ones((seq, 2, n_freq), F32)
        sin = jnp.zeros((seq, 2, n_freq), F32)
    cos_full = jnp.stack([cos, cos], axis=2).reshape(seq, C_ROPE)
    sin_signed = jnp.stack([-sin, sin], axis=2).reshape(seq, C_ROPE)
    tabq = scale * jnp.concatenate([jnp.ones((seq, C_NOPE), F32), cos_full, sin_signed], axis=1)
    tabk = jnp.concatenate([cos_full, sin_signed, jnp.zeros((seq, LANES - 2 * C_ROPE), F32)], axis=1)
    return tabq, tabk


def _layer(x2d, batch, seq, layer, mod4, mod_row, pw, lb_all, s_init, cache, tabs, want_ctx_outputs,
           final, fnorm, prev_states=(), prev_caches=()):
    tabq, tabk = tabs
    res = _in_call(x2d, layer, mod4, mod_row, pw, lb_all, tabq, tabk, want_ctx_outputs, seq, prev_caches)
    q, gf, gb, kf, kb, v, ga, bb, u, qfull, kfull, vv, sg = res[:13]
    own_cache = None
    if want_ctx_outputs:
        own_cache = res[13] if prev_caches else res[13].reshape(batch, seq, CACHE_W)
    ya, states = _hgrn_call(q, gf, gb, kf, kb, v, ga, layer, pw["gnorm"], s_init, batch, seq, want_ctx_outputs,
                            prev_states)
    kc = vc = None
    if cache is not None:
        kc, vc = _cache_call(cache[0], cache[1], layer, pw["w_kv_up"])
    yc = _attn_call(qfull, kfull, vv, kc, vc, batch, seq)
    x1, xn2 = _merge_call(x2d, ya, yc, bb, u, sg, layer, mod4, mod_row, pw, seq)
    x2 = _ffn_call(xn2, x1, layer, mod4, mod_row, pw, fnorm, seq, final)
    return x2, states, own_cache


def kernel(x_prompt, x_sample, state_hgrn, cache_mla, c, c_ctx, w_ada, b_ada, norm1, w_in, hgrn_lb_logits, hgrn_gnorm, w_o_hgrn, conv_w, w_o_conv, mla_q_norm, w_q_up, mla_kv_norm, w_kv_up, w_o_mla, w_out, norm2, w_up, ffn_conv_w, w_down, final_norm):
    depth = w_in.shape[0]
    bp, sp, _ = x_prompt.shape
    bs, ss, _ = x_sample.shape
    assert 1 + bs <= MOD_ROWS
    for rows in (IN_ROWS, MERGE_ROWS, FFN_ROWS):
        assert (bp * sp) % rows == 0 and (bs * ss) % rows == 0
        assert (rows % sp == 0 or sp % rows == 0) and ss % rows == 0

    cvec = _pad_axis(jnp.concatenate([c_ctx[None, :], c], axis=0), MOD_ROWS, 0)
    mod4 = _ada_call(cvec, w_ada, b_ada).reshape(depth, MOD_ROWS, 6, D_MODEL)
    lb_all = _lb_call(hgrn_lb_logits)
    tabs_ctx = _rope_tables(IN_ROWS, rotate=False)
    tabs_lat = _rope_tables(ss, rotate=True)
    fnorm = final_norm.reshape(1, D_MODEL)
    cache = (cache_mla, _pad_axis(cache_mla[..., C_KV_LORA:], LANES, 3))
    w_in_a, w_in_kv, w_in_g = _split_w_in(w_in)

    def vec(a):
        return a.reshape(depth, 1, -1)

    pw = {
        "norm1": vec(norm1), "w_in_a": w_in_a, "w_in_kv": w_in_kv, "w_in_g": w_in_g, "gnorm": vec(hgrn_gnorm),
        "w_o_hgrn": w_o_hgrn.astype(BF16), "conv_w": _pad_axis(conv_w, SUBLANES, 1),
        "w_o_conv": w_o_conv.astype(BF16), "q_norm": vec(mla_q_norm), "w_q_up": _pack_w_q_up(w_q_up),
        "kv_norm": vec(mla_kv_norm), "w_kv_up": _pack_w_kv_up(w_kv_up), "w_o_mla": w_o_mla.astype(BF16),
        "w_out": w_out.astype(BF16), "norm2": vec(norm2), "w_up": w_up.astype(BF16),
        "ffn_conv_w": _pad_axis(ffn_conv_w, SUBLANES, 1), "w_down": w_down.astype(BF16),
    }

    xp = x_prompt.reshape(bp * sp, D_MODEL)
    xs = x_sample.reshape(bs * ss, D_MODEL)
    stack_in_kernel = depth > 1 and IN_ROWS % sp == 0
    new_states, new_caches = [], []
    for l in range(depth):
        final = l == depth - 1
        prev = (tuple(new_states), tuple(new_caches)) if (final and stack_in_kernel) else ((), ())
        xp, st, kvc = _layer(xp, bp, sp, l, mod4, lambda r: 0, pw, lb_all, None, None, tabs_ctx,
                             True, final, fnorm, *prev)
        new_states.append(st)
        new_caches.append(kvc)
        xs, _, _ = _layer(xs, bs, ss, l, mod4, lambda r: 1 + r // ss, pw, lb_all, state_hgrn,
                          cache, tabs_lat, False, final, fnorm)
    y_prompt = xp.reshape(bp, sp, D_MODEL)
    y_sample = xs.reshape(bs, ss, D_MODEL)
    if stack_in_kernel:
        return (y_prompt, y_sample, new_states[-1], new_caches[-1])
    return (y_prompt, y_sample, jnp.stack(new_states, axis=1), jnp.stack(new_caches, axis=1))
```

```python
import functools
import math

import numpy as np
import jax
import jax.numpy as jnp
from jax import lax
from jax.experimental import pallas as pl
from jax.experimental.pallas import tpu as pltpu

F32 = jnp.float32
BF16 = jnp.bfloat16

D_MODEL = 1024
GRID_W = 64
EPS = 1e-6
A_HEADS = 4
A_DK = 128
A_DV = 128
A_W = A_HEADS * A_DK
LB_FLOOR = 1e-30
B_W = 512
C_HEADS = 8
C_NOPE = 64
C_ROPE = 32
C_V = 64
C_Q_LORA = 384
C_KV_LORA = 256
ROPE_THETA = 10000.0
D_FF = 2816
N_BRANCH = 3
CACHE_W = C_KV_LORA + C_ROPE
LOG2E = 1.4426950408889634

LANES = 128
SUBLANES = 8
HALO = 16
VMEM_LIMIT_BYTES = 56 * 1024 * 1024

IN_ROWS = 512
IN_LOOKAHEAD = 1
HGRN_CHUNK = 128
HGRN_CHUNK_GROUP = 2
HGRN_VMEM_BUDGET = 52 * 1024 * 1024
ATTN_Q_ROWS = 512
ATTN_SEQ_ROWS = 1024
MERGE_ROWS = 512
FFN_ROWS = 1024
FFN_COLS = 256
MOD_ROWS = 16
ADA_COLS = 1536

OFF_CQ = 8 * A_W
OFF_KV = OFF_CQ + C_Q_LORA
KV_EXT = 3 * LANES
OFF_GATE = OFF_KV + CACHE_W


def _cparams(sem):
    return pltpu.CompilerParams(dimension_semantics=sem, vmem_limit_bytes=VMEM_LIMIT_BYTES)


def _dot(a, b):
    return jnp.dot(a, b, preferred_element_type=F32)


def _dot_nt(a, b):
    return lax.dot_general(a, b, (((1,), (1,)), ((), ())), preferred_element_type=F32)


def _dot_tn(a, b):
    return lax.dot_general(a, b, (((0,), (0,)), ((), ())), preferred_element_type=F32)


def _rms(x, g):
    ms = jnp.mean(x * x, axis=-1, keepdims=True)
    return x * lax.rsqrt(ms + EPS) * g


def _sigmoid_pair(z):
    t = jnp.exp(-jnp.abs(z))
    r = 1.0 / (1.0 + t)
    big, small = r, t * r
    pos = z >= 0
    return jnp.where(pos, big, small), jnp.where(pos, small, big)


def _sigmoid(z):
    return 1.0 / (1.0 + jnp.exp(-z))


def _silu(z):
    return z * _sigmoid(z)


def _layer_spec(shape, layer, grid_rank, single=True):
    nd = len(shape)
    if grid_rank == 1:
        imap = lambda i: (layer,) + (0,) * nd
    else:
        imap = lambda i, j: (layer,) + (0,) * nd
    if single:
        return pl.BlockSpec((None,) + tuple(shape), imap, pipeline_mode=pl.Buffered(1))
    return pl.BlockSpec((None,) + tuple(shape), imap)


def _mod_spec(layer, mod_row, rows, grid_rank=1):
    if grid_rank == 1:
        return pl.BlockSpec((None, None, 6, D_MODEL), lambda i: (layer, mod_row(i * rows), 0, 0))
    return pl.BlockSpec((None, None, 6, D_MODEL), lambda i, j: (layer, mod_row(i * rows), 0, 0))


def _ada_kernel(c_ref, w_ref, b_ref, o_ref):
    s = _silu(c_ref[...])
    o_ref[...] = jnp.dot(s, w_ref[...], preferred_element_type=F32,
                         precision=lax.Precision.HIGHEST) + b_ref[...]


def _ada_call(cvec, w_ada, b_ada):
    depth = w_ada.shape[0]
    ncol = w_ada.shape[2] // ADA_COLS
    return pl.pallas_call(
        _ada_kernel,
        grid=(depth, ncol),
        in_specs=[
            pl.BlockSpec((MOD_ROWS, D_MODEL), lambda l, j: (0, 0)),
            pl.BlockSpec((None, D_MODEL, ADA_COLS), lambda l, j: (l, 0, j)),
            pl.BlockSpec((None, 1, ADA_COLS), lambda l, j: (l, 0, j)),
        ],
        out_specs=pl.BlockSpec((None, MOD_ROWS, ADA_COLS), lambda l, j: (l, 0, j)),
        out_shape=jax.ShapeDtypeStruct((depth, MOD_ROWS, w_ada.shape[2]), F32),
        compiler_params=_cparams(("arbitrary", "arbitrary")),
        name="ada_mod",
    )(cvec, w_ada, b_ada.reshape(depth, 1, -1))


def _lb_kernel(x_ref, o_ref):
    depth = x_ref.shape[0]
    xs = [x_ref[l] for l in range(depth)]
    m = xs[0]
    for l in range(1, depth):
        m = jnp.maximum(m, xs[l])
    es = [jnp.exp(x - m) for x in xs]
    tot = es[0]
    for l in range(1, depth):
        tot = tot + es[l]
    ps = [e / tot for e in es]
    run = jnp.zeros_like(ps[0])
    for l in range(depth):
        run = run + ps[l]
        o_ref[l] = run - ps[0]


def _lb_call(logits):
    return pl.pallas_call(
        _lb_kernel,
        out_shape=jax.ShapeDtypeStruct(logits.shape, F32),
        name="hgrn_lb",
    )(logits)


def _place_rope_key(z3, tabk):
    tmp = z3 * tabk
    rot = tmp + pltpu.roll(tmp, LANES - C_ROPE, axis=1)
    lane = lax.broadcasted_iota(jnp.int32, rot.shape, 1)
    rot = jnp.where(lane < C_ROPE, rot, 0.0)
    return pltpu.roll(rot, C_NOPE, axis=1) + pltpu.roll(rot, C_NOPE + C_ROPE, axis=1)


def _in_kernel(*refs, n_prev, seq):
    (x_ref, mod_ref, n1_ref, w_ref, wkvd_ref, wg_ref, lb_ref, qn_ref, wq_ref, kvn_ref, wkv_ref,
     tabq_ref, tabk_ref) = refs[:13]
    prev_cache_refs = refs[13:13 + n_prev]
    (q_ref, gf_ref, gb_ref, kf_ref, kb_ref, v_ref, ga_ref, bb_ref, u_ref,
     qfull_ref, kfull_ref, vv_ref, sg_ref) = refs[13 + n_prev:26 + n_prev]
    cache_ref = refs[26 + n_prev:]
    x = x_ref[...]
    shift1 = mod_ref[0:1, :]
    scale1 = mod_ref[1:2, :]
    xn = _rms(x, n1_ref[...]) * (1.0 + scale1) + shift1
    xb = xn.astype(BF16)

    def proj(off, width):
        return lambda: _dot(xb, w_ref[:, off:off + width])

    env = {}
    stages = []

    def ep_q(z):
        q_ref[...] = (_silu(z) * (A_DK ** -0.5)).astype(BF16)
    stages.append((proj(0, A_W), ep_q))

    def ep_forget(d, g_ref, k_ref):
        def ep(z):
            lb = lb_ref[d:d + 1, :]
            sp, sn = _sigmoid_pair(z)
            g_ref[...] = jnp.log(jnp.maximum(lb, LB_FLOOR) + (1.0 - lb) * sp) * LOG2E
            k_ref[...] = ((1.0 - lb) * sn - jnp.maximum(LB_FLOOR - lb, 0.0)).astype(BF16)
        return ep
    stages.append((proj(1 * A_W, A_W), ep_forget(0, gf_ref, kf_ref)))
    stages.append((proj(2 * A_W, A_W), ep_forget(1, gb_ref, kb_ref)))

    def ep_v(z):
        v_ref[...] = z.astype(BF16)
    stages.append((proj(3 * A_W, A_W), ep_v))

    def ep_ga(z):
        ga_ref[...] = _silu(z).astype(BF16)
    stages.append((proj(4 * A_W, A_W), ep_ga))

    def ep_bb(z):
        bb_ref[...] = z.astype(BF16)
    stages.append((proj(5 * A_W, B_W), ep_bb))

    def ep_cb(z):
        env["conv_c"] = z
    stages.append((proj(6 * A_W, B_W), ep_cb))

    def ep_hb(z):
        u_ref[...] = (env["conv_c"] * z).astype(BF16)
    stages.append((proj(7 * A_W, B_W), ep_hb))

    def ep_cq(z):
        env["cqn"] = _rms(z, qn_ref[...]).astype(BF16)
    stages.append((proj(OFF_CQ, C_Q_LORA), ep_cq))

    def ep_kv(z):
        ckvn = _rms(z[:, :C_KV_LORA], kvn_ref[...])
        z3 = z[:, C_KV_LORA:]
        if cache_ref and n_prev == 0:
            cache_ref[0][:, :C_KV_LORA] = ckvn
            cache_ref[0][:, C_KV_LORA:] = z3[:, :C_ROPE]
        elif cache_ref:
            for s in range(x_ref.shape[0] // seq):
                rows = slice(s * seq, (s + 1) * seq)
                for l, prev in enumerate(prev_cache_refs):
                    cache_ref[0][s, l] = prev[s]
                cache_ref[0][s, n_prev, :, :C_KV_LORA] = ckvn[rows, :]
                cache_ref[0][s, n_prev, :, C_KV_LORA:] = z3[rows, :C_ROPE]
        krr = _place_rope_key(z3, tabk_ref[...])
        env["krr"] = jnp.concatenate([krr, krr], axis=1)
        env["cb"] = ckvn.astype(BF16)
    stages.append((lambda: _dot(xb, wkvd_ref[...]), ep_kv))

    def gate_stage(j):
        def ep(z):
            sg_ref[:, j * A_W:(j + 1) * A_W] = _sigmoid(z).astype(BF16)
        return (lambda: _dot(xb, wg_ref[:, j * A_W:(j + 1) * A_W]), ep)

    def q_up_stage(h):
        def ep(z):
            tabq = jnp.concatenate([tabq_ref[...]] * 2, axis=1)
            qfull_ref[:, h * LANES:(h + 2) * LANES] = (z * tabq).astype(BF16)
        return (lambda: _dot(env["cqn"], wq_ref[:, h * LANES:(h + 2) * LANES]), ep)

    def k_up_stage(h):
        def ep(z):
            kfull_ref[:, h * LANES:(h + 2) * LANES] = (z + env["krr"]).astype(BF16)
        return (lambda: _dot(env["cb"], wkv_ref[:, h * LANES:(h + 2) * LANES]), ep)

    def ep_vv(z):
        vv_ref[...] = z.astype(BF16)
    v_up_stage = (lambda: _dot(env["cb"], wkv_ref[:, C_HEADS * LANES:]), ep_vv)

    stages = stages[8:] + stages[:8]
    stages += [gate_stage(0), q_up_stage(0), q_up_stage(2), gate_stage(1), q_up_stage(4), q_up_stage(6),
               gate_stage(2), k_up_stage(0), k_up_stage(2), gate_stage(3), k_up_stage(4), k_up_stage(6),
               gate_stage(4), v_up_stage, gate_stage(5)]

    pending = [stages[j][0]() for j in range(IN_LOOKAHEAD)]
    for j, (_, epilogue) in enumerate(stages):
        if j + IN_LOOKAHEAD < len(stages):
            pending.append(stages[j + IN_LOOKAHEAD][0]())
        epilogue(pending.pop(0))


def _in_call(x2d, layer, mod4, mod_row, pw, lb_all, tabq, tabk, want_cache, seq, prev_caches=()):
    n = x2d.shape[0]
    tm = IN_ROWS
    tab_blocks = tabq.shape[0] // tm
    n_prev = len(prev_caches) if want_cache else 0

    def row(width):
        return pl.BlockSpec((tm, width), lambda i: (i, 0))

    outs = [
        (A_W, BF16),
        (A_W, F32), (A_W, F32),
        (A_W, BF16), (A_W, BF16),
        (A_W, BF16),
        (A_W, BF16),
        (B_W, BF16),
        (B_W, BF16),
        (C_HEADS * LANES, BF16),
        (C_HEADS * LANES, BF16),
        (C_HEADS * C_V, BF16),
        (N_BRANCH * D_MODEL, BF16),
    ]
    out_specs = [row(w) for w, _ in outs]
    out_shape = [jax.ShapeDtypeStruct((n, w), dt) for w, dt in outs]
    prev_specs = []
    if want_cache and n_prev == 0:
        out_specs.append(row(CACHE_W))
        out_shape.append(jax.ShapeDtypeStruct((n, CACHE_W), F32))
    elif want_cache:
        per = tm // seq
        prev_specs = [pl.BlockSpec((per, seq, CACHE_W), lambda i: (i, 0, 0)) for _ in prev_caches]
        out_specs.append(pl.BlockSpec((per, n_prev + 1, seq, CACHE_W), lambda i: (i, 0, 0, 0)))
        out_shape.append(jax.ShapeDtypeStruct((n // seq, n_prev + 1, seq, CACHE_W), F32))
    return pl.pallas_call(
        functools.partial(_in_kernel, n_prev=n_prev, seq=seq),
        grid=(n // tm,),
        in_specs=[
            row(D_MODEL),
            _mod_spec(layer, mod_row, tm),
            _layer_spec((1, D_MODEL), layer, 1),
            _layer_spec((D_MODEL, OFF_KV), layer, 1),
            _layer_spec((D_MODEL, KV_EXT), layer, 1),
            _layer_spec((D_MODEL, N_BRANCH * D_MODEL), layer, 1),
            _layer_spec((2, A_W), layer, 1),
            _layer_spec((1, C_Q_LORA), layer, 1),
            _layer_spec((C_Q_LORA, C_HEADS * LANES), layer, 1),
            _layer_spec((1, C_KV_LORA), layer, 1),
            _layer_spec((C_KV_LORA, C_HEADS * LANES + C_HEADS * C_V), layer, 1),
            pl.BlockSpec((tm, LANES), lambda i: (i % tab_blocks, 0)),
            pl.BlockSpec((tm, LANES), lambda i: (i % tab_blocks, 0)),
        ] + prev_specs,
        out_specs=out_specs,
        out_shape=out_shape,
        compiler_params=_cparams(("arbitrary",)),
        name="in_proj",
    )(x2d, mod4, pw["norm1"], pw["w_in_a"], pw["w_in_kv"], pw["w_in_g"], lb_all, pw["q_norm"], pw["w_q_up"],
      pw["kv_norm"], pw["w_kv_up"], tabq, tabk, *(prev_caches if n_prev else ()))


def _cache_kernel(c_ref, kr_ref, wkv_ref, kfull_ref, vv_ref):
    cb = c_ref[:, :C_KV_LORA].astype(BF16)
    kr = kr_ref[...]
    krr = pltpu.roll(kr, C_NOPE, axis=1) + pltpu.roll(kr, C_NOPE + C_ROPE, axis=1)
    krr = jnp.concatenate([krr, krr], axis=1)
    for h in range(0, C_HEADS, 2):
        kz = _dot(cb, wkv_ref[:, h * LANES:(h + 2) * LANES])
        kfull_ref[:, h * LANES:(h + 2) * LANES] = (kz + krr).astype(BF16)
    vv_ref[...] = _dot(cb, wkv_ref[:, C_HEADS * LANES:]).astype(BF16)


def _cache_call(cache, cache_kr, layer, wkv):
    b, _, p, w = cache.shape
    return pl.pallas_call(
        _cache_kernel,
        grid=(b,),
        in_specs=[
            pl.BlockSpec((None, None, p, w), lambda i: (i, layer, 0, 0)),
            pl.BlockSpec((None, None, p, LANES), lambda i: (i, layer, 0, 0)),
            _layer_spec(wkv.shape[1:], layer, 1, single=False),
        ],
        out_specs=[
            pl.BlockSpec((None, p, C_HEADS * LANES), lambda i: (i, 0, 0)),
            pl.BlockSpec((None, p, C_HEADS * C_V), lambda i: (i, 0, 0)),
        ],
        out_shape=[
            jax.ShapeDtypeStruct((b, p, C_HEADS * LANES), BF16),
            jax.ShapeDtypeStruct((b, p, C_HEADS * C_V), BF16),
        ],
        compiler_params=_cparams(("arbitrary",)),
        name="cache_expand",
    )(cache, cache_kr, wkv)


def _level_matrix(c, rev):
    t = np.arange(c)[:, None]
    s = np.arange(c)[None, :]
    lev = np.floor(np.log2(np.maximum(np.bitwise_xor(t, s), 1))).astype(np.int32)
    lev = np.where(s == t, -1, lev)
    lev = np.where((s < t) if rev else (s > t), -2, lev)
    return lev.astype(np.int32)


def _prefix_matrix(c, rev):
    t = np.arange(c)[:, None]
    s = np.arange(c)[None, :]
    incl = ((s >= t) if rev else (s <= t)).astype(np.float32)
    return np.concatenate([incl, incl], axis=1)


def _level_exponent(b, g, h, rev):
    c, dk = b.shape
    ref = h if rev else h - 1
    if 2 * h > SUBLANES:
        parts = []
        for gs in range(0, c, 2 * h):
            beta = jnp.broadcast_to(b[gs + ref:gs + ref + 1, :], (h, dk))
            lo = b[gs:gs + h, :]
            hi = b[gs + h:gs + 2 * h, :]
            parts += [lo - beta, beta - hi] if rev else [beta - lo, hi - beta]
        return jnp.concatenate(parts, axis=0)
    row = lax.broadcasted_iota(jnp.int32, (SUBLANES, dk), 0)
    upper = (row % (2 * h)) >= h
    keep = jnp.logical_not(upper) if rev else upper
    if h == 1:
        parts = [jnp.where(keep, g[j:j + SUBLANES, :], 0.0) for j in range(0, c, SUBLANES)]
        return jnp.concatenate(parts, axis=0)
    sign = jnp.where(keep, 1.0, -1.0)
    parts = []
    for j in range(0, c, SUBLANES):
        bj = b[j:j + SUBLANES, :]
        beta = jnp.broadcast_to(bj[ref:ref + 1, :], (SUBLANES, dk))
        for gs in range(2 * h, SUBLANES, 2 * h):
            beta = jnp.where(row >= gs, jnp.broadcast_to(bj[gs + ref:gs + ref + 1, :], (SUBLANES, dk)), beta)
        parts.append((bj - beta) * sign)
    return jnp.concatenate(parts, axis=0)


def _level_halves(c, h):
    return [(slice(gs, gs + h), slice(gs + h, gs + 2 * h)) for gs in range(0, c, 2 * h)]


def _level_operands(q, k, qf, kf, x, h, rev):
    c = q.shape[0]
    if h < SUBLANES:
        return (qf * x).astype(BF16), (kf * x).astype(BF16)
    packed = h >= HALO
    qparts, kparts = [], []
    for lo, hi in _level_halves(c, h):
        qs, ks = (lo, hi) if rev else (hi, lo)
        q_sc, k_sc = qf[qs, :] * x[qs, :], kf[ks, :] * x[ks, :]
        if packed:
            q_sc, k_sc = q_sc.astype(BF16), k_sc.astype(BF16)
            q_raw, k_raw = q[ks, :], k[qs, :]
        else:
            q_raw, k_raw = qf[ks, :], kf[qs, :]
        qparts += [q_sc, q_raw] if rev else [q_raw, q_sc]
        kparts += [k_raw, k_sc] if rev else [k_sc, k_raw]
    qh, kh = jnp.concatenate(qparts, axis=0), jnp.concatenate(kparts, axis=0)
    return (qh, kh) if packed else (qh.astype(BF16), kh.astype(BF16))


def _level_select(a, p, lev, hl, rev):
    c = a.shape[0]
    h = 1 << hl
    if h < SUBLANES:
        return jnp.where(lev == hl, p, a)
    parts = []
    for lo, hi in _level_halves(c, h):
        qs, other = (lo, hi) if rev else (hi, lo)
        sel = jnp.where(lev[qs, :] == hl, p[qs, :], a[qs, :])
        parts += [sel, a[other, :]] if rev else [a[other, :], sel]
    return jnp.concatenate(parts, axis=0)


def _hgrn_intra(streams, lev, pmat):
    c = streams[0][0].shape[0]
    bs = []
    for q, k, g, v, rev in streams:
        g_hi = g.astype(BF16)
        g_lo = (g - g_hi.astype(F32)).astype(BF16)
        bs.append(_dot(pmat[rev], jnp.concatenate([g_hi, g_lo], axis=0)))
    qfs = [s[0].astype(F32) for s in streams]
    kfs = [s[1].astype(F32) for s in streams]
    a_s = [jnp.zeros((c, c), F32) for _ in streams]
    for hl in range(int(np.log2(c))):
        ps = []
        for (q, k, g, v, rev), b, qf, kf in zip(streams, bs, qfs, kfs):
            x = jnp.exp2(_level_exponent(b, g, 1 << hl, rev))
            ps.append(_dot_nt(*_level_operands(q, k, qf, kf, x, 1 << hl, rev)))
        a_s = [_level_select(a, p, lev[s[4]], hl, s[4]) for s, p, a in zip(streams, ps, a_s)]
    ps = [_dot_nt(s[0], s[1]) for s in streams]
    a_s = [jnp.where(lev[s[4]] == -1, p, a).astype(BF16) for s, p, a in zip(streams, ps, a_s)]
    return bs, a_s, qfs, kfs


def _hgrn_inter(streams, intra, states):
    c = streams[0][0].shape[0]
    bs, a_s, qfs, kfs = intra
    outs = []
    for (q, k, g, v, rev), b, a, qf, st in zip(streams, bs, a_s, qfs, states):
        outs.append(_dot(a, v) + _dot_nt((qf * jnp.exp2(b)).astype(BF16), st.astype(BF16)))
    new_states = []
    for (q, k, g, v, rev), b, kf, st in zip(streams, bs, kfs, states):
        tot = b[0:1, :] if rev else b[c - 1:c, :]
        ks = (kf * jnp.exp2(tot - b)).astype(BF16)
        new_states.append(st * jnp.exp2(tot) + _dot_tn(v, ks))
    return outs, new_states


def _hgrn_kernel(*refs, has_init, want_state, n_prev):
    refs = list(refs)
    (q_ref, gf_ref, gb_ref, kf_ref, kb_ref, v_ref, ga_ref, gn_ref,
     levf_ref, levb_ref, pmf_ref, pmb_ref) = refs[:12]
    pos = 12
    s0_ref = None
    if has_init:
        s0_ref = refs[pos]
        pos += 1
    prev_state_refs = refs[pos:pos + n_prev]
    pos += n_prev
    y_ref = refs[pos]
    pos += 1
    sfin_ref = None
    if want_state:
        sfin_ref = refs[pos]
        pos += 1
    o_scr, st_scr = refs[pos], refs[pos + 1]

    t = q_ref.shape[0]
    c = HGRN_CHUNK
    n = t // c
    hp = q_ref.shape[1] // A_DK
    for d in range(2):
        for h in range(hp):
            st_scr[d, h] = s0_ref[d, h].T if has_init else jnp.zeros((A_DV, A_DK), F32)
    lev = (levf_ref, levb_ref)
    pmat = (pmf_ref, pmb_ref)
    g_refs = (gf_ref, gb_ref)
    k_refs = (kf_ref, kb_ref)

    group = math.gcd(n, HGRN_CHUNK_GROUP)

    def body(i, carry):
        levs = (lev[0][...], lev[1][...])
        pmats = (pmat[0][...], pmat[1][...])
        steps = []
        for u in range(group):
            streams, rows = [], []
            for d in range(2):
                ci = i * group + u if d == 0 else n - 1 - (i * group + u)
                sl = pl.ds(pl.multiple_of(ci * c, c), c)
                rows.append(sl)
                for h in range(hp):
                    cs = slice(h * A_DK, (h + 1) * A_DK)
                    streams.append((q_ref[sl, cs], k_refs[d][sl, cs], g_refs[d][sl, cs], v_ref[sl, cs], d))
            steps.append((streams, rows))
        all_streams = [s for streams, _ in steps for s in streams]
        bs, a_s, qfs, kfs = _hgrn_intra(all_streams, levs, pmats)
        per = 2 * hp
        states = [st_scr[d, h] for d in range(2) for h in range(hp)]
        for u, (streams, rows) in enumerate(steps):
            part = slice(u * per, (u + 1) * per)
            outs, states = _hgrn_inter(streams, (bs[part], a_s[part], qfs[part], kfs[part]), states)
            for idx, o in enumerate(outs):
                d, h = divmod(idx, hp)
                o_scr[d, rows[d], h * A_DV:(h + 1) * A_DV] = o
        for idx, st_new in enumerate(states):
            d, h = divmod(idx, hp)
            st_scr[d, h] = st_new
        return carry

    lax.fori_loop(0, n // group, body, 0)
    for l, prev in enumerate(prev_state_refs):
        sfin_ref[l] = prev[...]
    for h in range(hp):
        cs = slice(h * A_DV, (h + 1) * A_DV)
        if want_state:
            for d in range(2):
                if n_prev:
                    sfin_ref[n_prev, d, h] = st_scr[d, h].T
                else:
                    sfin_ref[d, h] = st_scr[d, h].T
        o = o_scr[0, :, cs] + o_scr[1, :, cs]
        y_ref[:, cs] = (_rms(o, gn_ref[...]) * ga_ref[:, cs]).astype(BF16)


def _hgrn_heads_per_step(seq):
    per_head_row = 2 * (5 * 2 + 2 * 4) * A_DK + 2 * 2 * A_DV + 2 * 4 * A_DV
    hp = A_HEADS
    while hp > 1 and hp * seq * per_head_row > HGRN_VMEM_BUDGET:
        hp //= 2
    return hp


def _hgrn_call(q, gf, gb, kf, kb, v, ga, layer, gnorm, s_init, batch, seq, want_state, prev_states=()):
    c = HGRN_CHUNK
    n_prev = len(prev_states) if want_state else 0
    hp = _hgrn_heads_per_step(seq)
    consts =[jnp.asarray(_level_matrix(c, False)), jnp.asarray(_level_matrix(c, True)),
              jnp.asarray(_prefix_matrix(c, False), BF16), jnp.asarray(_prefix_matrix(c, True), BF16)]

    def r3(a):
        return a.reshape(batch, seq, A_W)

    col = pl.BlockSpec((None, seq, hp * A_DK), lambda b, h: (b, 0, h))
    in_specs = [col] * 7 + [_layer_spec((1, A_DV), layer, 2, single=False)]
    in_specs += [pl.BlockSpec(m.shape, lambda b, h: (0, 0)) for m in consts]
    args = [r3(q), r3(gf), r3(gb), r3(kf), r3(kb), r3(v), r3(ga), gnorm] + consts
    has_init = s_init is not None
    if has_init:
        in_specs.append(pl.BlockSpec((None, None, 2, hp, A_DK, A_DV), lambda b, h: (b, layer, 0, h, 0, 0)))
        args.append(s_init)
    st_spec = pl.BlockSpec((None, 2, hp, A_DK, A_DV), lambda b, h: (b, 0, h, 0, 0))
    for prev in (prev_states if n_prev else ()):
        in_specs.append(st_spec)
        args.append(prev)
    out_specs = [col]
    out_shape = [jax.ShapeDtypeStruct((batch, seq, A_W), BF16)]
    if want_state and n_prev:
        out_specs.append(pl.BlockSpec((None, n_prev + 1, 2, hp, A_DK, A_DV), lambda b, h: (b, 0, 0, h, 0, 0)))
        out_shape.append(jax.ShapeDtypeStruct((batch, n_prev + 1, 2, A_HEADS, A_DK, A_DV), F32))
    elif want_state:
        out_specs.append(st_spec)
        out_shape.append(jax.ShapeDtypeStruct((batch, 2, A_HEADS, A_DK, A_DV), F32))
    res = pl.pallas_call(
        functools.partial(_hgrn_kernel, has_init=has_init, want_state=want_state, n_prev=n_prev),
        grid=(batch, A_HEADS // hp),
        in_specs=in_specs,
        out_specs=out_specs,
        out_shape=out_shape,
        scratch_shapes=[pltpu.VMEM((2, seq, hp * A_DV), F32), pltpu.VMEM((2, hp, A_DV, A_DK), F32)],
        compiler_params=_cparams(("arbitrary", "arbitrary")),
        name="hgrn",
    )(*args)
    y = res[0].reshape(batch * seq, A_W)
    return y, (res[1] if want_state else None)


def _attn_kernel(*refs, has_cache):
    if has_cache:
        q_ref, k_ref, v_ref, kc_ref, vc_ref, y_ref = refs
    else:
        q_ref, k_ref, v_ref, y_ref = refs
    nseq = q_ref.shape[0]
    lane = lax.broadcasted_iota(jnp.int32, (q_ref.shape[1], LANES), 1)

    def scores(s, h):
        hs = slice(h * LANES, (h + 1) * LANES)
        qh = q_ref[s, :, hs]
        s1 = _dot_nt(qh, k_ref[s, :, hs])
        s2 = _dot_nt(qh, kc_ref[s, :, hs]) if has_cache else None
        return s1, s2

    def attend(s, h, s1, s2):
        vs = slice((h // 2) * LANES, (h // 2 + 1) * LANES)
        m = jnp.max(s1, axis=-1, keepdims=True)
        if has_cache:
            m = jnp.maximum(m, jnp.max(s2, axis=-1, keepdims=True))
        p1 = jnp.exp2(s1 - m)
        l = jnp.sum(p1, axis=-1, keepdims=True)
        o = _dot(p1.astype(BF16), v_ref[s, :, vs])
        if has_cache:
            p2 = jnp.exp2(s2 - m)
            l = l + jnp.sum(p2, axis=-1, keepdims=True)
            o = o + _dot(p2.astype(BF16), vc_ref[s, :, vs])
        return o / l

    items = [(s, h) for s in range(nseq) for h in range(C_HEADS)]
    pending = scores(*items[0])
    prev_out = None
    for idx, (s, h) in enumerate(items):
        nxt = scores(*items[idx + 1]) if idx + 1 < len(items) else None
        out = attend(s, h, *pending)
        pending = nxt
        if h % 2 == 1:
            vs = slice((h // 2) * LANES, (h // 2 + 1) * LANES)
            y_ref[s, :, vs] = jnp.where(lane < C_V, prev_out, out).astype(BF16)
        prev_out = out


def _attn_call(qfull, kfull, vv, kc, vc, batch, seq):
    tq = min(ATTN_Q_ROWS, seq)
    nb = math.gcd(batch, max(1, ATTN_SEQ_ROWS // seq)) if tq == seq else 1
    has_cache = kc is not None
    kw = C_HEADS * LANES
    vw = C_HEADS * C_V
    in_specs = [
        pl.BlockSpec((nb, tq, kw), lambda b, i: (b, i, 0)),
        pl.BlockSpec((nb, seq, kw), lambda b, i: (b, 0, 0)),
        pl.BlockSpec((nb, seq, vw), lambda b, i: (b, 0, 0)),
    ]
    args = [qfull.reshape(batch, seq, kw), kfull.reshape(batch, seq, kw), vv.reshape(batch, seq, vw)]
    if has_cache:
        p = kc.shape[1]
        in_specs += [
            pl.BlockSpec((nb, p, kw), lambda b, i: (b, 0, 0)),
            pl.BlockSpec((nb, p, vw), lambda b, i: (b, 0, 0)),
        ]
        args += [kc, vc]
    y = pl.pallas_call(
        functools.partial(_attn_kernel, has_cache=has_cache),
        grid=(batch // nb, seq // tq),
        in_specs=in_specs,
        out_specs=pl.BlockSpec((nb, tq, vw), lambda b, i: (b, i, 0)),
        out_shape=jax.ShapeDtypeStruct((batch, seq, vw), BF16),
        compiler_params=_cparams(("arbitrary", "arbitrary")),
        name="mla_attn",
    )(*args)
    return y.reshape(batch * seq, vw)


def _conv3_rows(u, prev_row, next_row, cw_ref, seq):
    r, w = u.shape
    sub = min(seq, r)
    whole = seq <= r
    i8 = lax.broadcasted_iota(jnp.int32, (SUBLANES, w), 0)
    w0, w1, w2 = cw_ref[0:1, :], cw_ref[1:2, :], cw_ref[2:3, :]
    outs = []
    for s in range(r // sub):
        blk = u[s * sub:(s + 1) * sub, :]
        p = 0.0 if whole else prev_row
        n = 0.0 if whole else next_row
        up = pltpu.roll(blk, 1, axis=0)
        dn = pltpu.roll(blk, sub - 1, axis=0)
        up = jnp.concatenate([jnp.where(i8 == 0, p, up[:SUBLANES, :]), up[SUBLANES:, :]], axis=0)
        dn = jnp.concatenate([dn[:sub - SUBLANES, :], jnp.where(i8 == SUBLANES - 1, n, dn[sub - SUBLANES:, :])],
                             axis=0)
        outs.append(up * w0 + blk * w1 + dn * w2)
    return outs[0] if len(outs) == 1 else jnp.concatenate(outs, axis=0)


def _seq_edges(i, rows, seq):
    first = i * rows
    return first % seq == 0, (first + rows) % seq == 0


def _merge_kernel(x_ref, ya_ref, yc_ref, bb_ref, u_ref, up_ref, un_ref, sg_ref, mod_ref,
                  wa_ref, wb_ref, wc_ref, wout_ref, cw_ref, n2_ref, x1_ref, xn2_ref, *, seq):
    tm = x_ref.shape[0]
    at_start, at_end = _seq_edges(pl.program_id(0), tm, seq)
    prev_row = jnp.where(at_start, 0.0, up_ref[HALO - 1:HALO, :].astype(F32))
    next_row = jnp.where(at_end, 0.0, un_ref[0:1, :].astype(F32))
    conv = _conv3_rows(u_ref[...].astype(F32), prev_row, next_row, cw_ref, seq)
    yb = (bb_ref[...] * conv).astype(BF16)
    half = tm // 2
    hs = []
    for s in range(2):
        rs = slice(s * half, (s + 1) * half)
        pa = _dot(ya_ref[rs, :], wa_ref[...])
        pc = _dot(yc_ref[rs, :], wc_ref[...])
        pb = _dot(yb[rs, :], wb_ref[...])
        hs.append((sg_ref[rs, 0:D_MODEL] * pa + sg_ref[rs, D_MODEL:2 * D_MODEL] * pb
                   + sg_ref[rs, 2 * D_MODEL:3 * D_MODEL] * pc).astype(BF16))
    gate1 = mod_ref[2:3, :]
    shift2 = mod_ref[3:4, :]
    scale2 = mod_ref[4:5, :]
    for s in range(2):
        rs = slice(s * half, (s + 1) * half)
        x1 = x_ref[rs, :] + gate1 * _dot(hs[s], wout_ref[...])
        x1_ref[rs, :] = x1
        xn2_ref[rs, :] = (_rms(x1, n2_ref[...]) * (1.0 + scale2) + shift2).astype(BF16)


def _merge_call(x2d, ya, yc, bb, u, sg, layer, mod4, mod_row, pw, seq):
    n = x2d.shape[0]
    tm = MERGE_ROWS
    nbh = n // HALO
    per = tm // HALO

    def row(width):
        return pl.BlockSpec((tm, width), lambda i: (i, 0))

    def wspec(name):
        return _layer_spec(pw[name].shape[1:], layer, 1)

    return pl.pallas_call(
        functools.partial(_merge_kernel, seq=seq),
        grid=(n // tm,),
        in_specs=[
            row(D_MODEL), row(A_W), row(C_HEADS * C_V), row(B_W), row(B_W),
            pl.BlockSpec((HALO, B_W), lambda i: (jnp.maximum(i * per - 1, 0), 0)),
            pl.BlockSpec((HALO, B_W), lambda i: (jnp.minimum((i + 1) * per, nbh - 1), 0)),
            row(N_BRANCH * D_MODEL),
            _mod_spec(layer, mod_row, tm),
            wspec("w_o_hgrn"), wspec("w_o_conv"), wspec("w_o_mla"), wspec("w_out"), wspec("conv_w"),
            wspec("norm2"),
        ],
        out_specs=[row(D_MODEL), row(D_MODEL)],
        out_shape=[jax.ShapeDtypeStruct((n, D_MODEL), F32), jax.ShapeDtypeStruct((n, D_MODEL), BF16)],
        compiler_params=_cparams(("arbitrary",)),
        name="merge",
    )(x2d, ya, yc, bb, u, u, u, sg, mod4, pw["w_o_hgrn"], pw["w_o_conv"], pw["w_o_mla"], pw["w_out"],
      pw["conv_w"], pw["norm2"])


def _ffn_kernel(xn_ref, xp_ref, xx_ref, x1_ref, mod_ref, wup_ref, cw_ref, wd_ref,
                fn_ref, out_ref, h_scr, lhs_scr, *, seq, final):
    r = xn_ref.shape[0]
    tn = FFN_COLS
    at_start, at_end = _seq_edges(pl.program_id(0), r, seq)
    need_halo = seq > r
    if need_halo:
        lhs_scr[0:HALO, :] = xp_ref[...]
        lhs_scr[HALO:HALO + r, :] = xn_ref[...]
        lhs_scr[HALO + r:, :] = xx_ref[...]
        xn = lhs_scr[...]
    else:
        xn = xn_ref[...]

    def up(off):
        return _dot(xn, wup_ref[:, off:off + tn])

    def conv_part(um, off):
        prev_row = next_row = None
        if need_halo:
            prev_row = jnp.where(at_start, 0.0, um[HALO - 1:HALO, :])
            next_row = jnp.where(at_end, 0.0, um[HALO + r:HALO + r + 1, :])
            um = um[HALO:HALO + r, :]
        return _conv3_rows(um, prev_row, next_row, cw_ref.at[:, off:off + tn], seq)

    offs = [o for j in range(D_FF // tn) for o in (j * tn, D_FF + j * tn)]
    pending = up(offs[0])
    a = None
    for idx, off in enumerate(offs):
        nxt = up(offs[idx + 1]) if idx + 1 < len(offs) else None
        conv = conv_part(pending, off)
        if off < D_FF:
            a = conv
        else:
            j = (off - D_FF) // tn
            h_scr[:, j * tn:(j + 1) * tn] = (_silu(a) * conv).astype(BF16)
        pending = nxt

    gate2 = mod_ref[5:6, :]
    y = x1_ref[...] + gate2 * _dot(h_scr[...], wd_ref[...])
    if final:
        y = _rms(y, fn_ref[...])
    out_ref[...] = y


def _ffn_call(xn2, x1, layer, mod4, mod_row, pw, fnorm, seq, final):
    n = xn2.shape[0]
    r = FFN_ROWS
    nbh = n // HALO
    per = r // HALO
    return pl.pallas_call(
        functools.partial(_ffn_kernel, seq=seq, final=final),
        grid=(n // r,),
        in_specs=[
            pl.BlockSpec((r, D_MODEL), lambda i: (i, 0)),
            pl.BlockSpec((HALO, D_MODEL), lambda i: (jnp.maximum(i * per - 1, 0), 0)),
            pl.BlockSpec((HALO, D_MODEL), lambda i: (jnp.minimum((i + 1) * per, nbh - 1), 0)),
            pl.BlockSpec((r, D_MODEL), lambda i: (i, 0)),
            _mod_spec(layer, mod_row, r),
            _layer_spec(pw["w_up"].shape[1:], layer, 1),
            _layer_spec(pw["ffn_conv_w"].shape[1:], layer, 1),
            _layer_spec(pw["w_down"].shape[1:], layer, 1),
            pl.BlockSpec((1, D_MODEL), lambda i: (0, 0)),
        ],
        out_specs=pl.BlockSpec((r, D_MODEL), lambda i: (i, 0)),
        out_shape=jax.ShapeDtypeStruct((n, D_MODEL), F32),
        scratch_shapes=[pltpu.VMEM((r, D_FF), BF16),
                        pltpu.VMEM((r + 2 * HALO if seq > r else HALO, D_MODEL), BF16)],
        compiler_params=_cparams(("arbitrary",)),
        name="conv_mlp",
    )(xn2, xn2, xn2, x1, mod4, pw["w_up"], pw["ffn_conv_w"], pw["w_down"], fnorm)


def _rope_swap_perm():
    idx = np.arange(C_ROPE).reshape(2, 2, C_ROPE // 4)
    return idx[:, ::-1, :].reshape(-1)


def _split_w_in(w_in):
    depth = w_in.shape[0]
    swap = _rope_swap_perm()
    wb = lax.optimization_barrier(w_in.astype(BF16))
    kv = wb[:, :, OFF_KV:OFF_GATE]
    kr = kv[:, :, C_KV_LORA:]
    pad = jnp.zeros((depth, D_MODEL, KV_EXT - CACHE_W - C_ROPE), BF16)
    return wb, jnp.concatenate([kv, kr[:, :, swap], pad], axis=2), wb[:, :, OFF_GATE:]


def _pack_w_q_up(w):
    depth = w.shape[0]
    swap = _rope_swap_perm()
    w4 = w.astype(BF16).reshape(depth, C_Q_LORA, C_HEADS, C_NOPE + C_ROPE)
    rope = w4[..., C_NOPE:]
    return jnp.concatenate([w4, rope[..., swap]], axis=3).reshape(depth, C_Q_LORA, C_HEADS * LANES)


def _pack_w_kv_up(w):
    depth = w.shape[0]
    w4 = w.astype(BF16).reshape(depth, C_KV_LORA, C_HEADS, C_NOPE + C_V)
    kn = jnp.concatenate([w4[..., :C_NOPE], jnp.zeros((depth, C_KV_LORA, C_HEADS, LANES - C_NOPE), BF16)], axis=3)
    return jnp.concatenate(
        [kn.reshape(depth, C_KV_LORA, C_HEADS * LANES), w4[..., C_NOPE:].reshape(depth, C_KV_LORA, C_HEADS * C_V)],
        axis=2)


def _pad_axis(a, size, axis):
    shape = list(a.shape)
    shape[axis] = size - a.shape[axis]
    return jnp.concatenate([a, jnp.zeros(shape, a.dtype)], axis=axis)


def _rope_tables(seq, rotate):
    scale = (C_NOPE + C_ROPE) ** -0.5 * LOG2E
    n_freq = C_ROPE // 4
    if rotate:
        rows = seq // GRID_W
        r = jnp.repeat(jnp.arange(rows, dtype=F32), GRID_W)
        col = jnp.tile(jnp.arange(GRID_W, dtype=F32), rows)
        freq = ROPE_THETA ** (-jnp.arange(n_freq, dtype=F32) / n_freq)
        ang = jnp.stack([r[:, None] * freq, col[:, None] * freq], axis=1)
        cos, sin = jnp.cos(ang), jnp.sin(ang)
    else:
        cos = jnp.ones((seq, 2, n_freq), F32)
        sin = jnp.zeros((seq, 2, n_freq), F32)
    cos_full = jnp.stack([cos, cos], axis=2).reshape(seq, C_ROPE)
    sin_signed = jnp.stack([-sin, sin], axis=2).reshape(seq, C_ROPE)
    tabq = scale * jnp.concatenate([jnp.ones((seq, C_NOPE), F32), cos_full, sin_signed], axis=1)
    tabk = jnp.concatenate([cos_full, sin_signed, jnp.zeros((seq, LANES - 2 * C_ROPE), F32)], axis=1)
    return tabq, tabk


def _layer(x2d, batch, seq, layer, mod4, mod_row, pw, lb_all, s_init, cache, tabs, want_ctx_outputs,
           final, fnorm, prev_states=(), prev_caches=()):
    tabq, tabk = tabs
    res = _in_call(x2d, layer, mod4, mod_row, pw, lb_all, tabq, tabk, want_ctx_outputs, seq, prev_caches)
    q, gf, gb, kf, kb, v, ga, bb, u, qfull, kfull, vv, sg = res[:13]
    own_cache = None
    if want_ctx_outputs:
        own_cache = res[13] if prev_caches else res[13].reshape(batch, seq, CACHE_W)
    ya, states = _hgrn_call(q, gf, gb, kf, kb, v, ga, layer, pw["gnorm"], s_init, batch, seq, want_ctx_outputs,
                            prev_states)
    kc = vc = None
    if cache is not None:
        kc, vc = _cache_call(cache[0], cache[1], layer, pw["w_kv_up"])
    yc = _attn_call(qfull, kfull, vv, kc, vc, batch, seq)
    x1, xn2 = _merge_call(x2d, ya, yc, bb, u, sg, layer, mod4, mod_row, pw, seq)
    x2 = _ffn_call(xn2, x1, layer, mod4, mod_row, pw, fnorm, seq, final)
    return x2, states, own_cache


def kernel(x_prompt, x_sample, state_hgrn, cache_mla, c, c_ctx, w_ada, b_ada, norm1, w_in, hgrn_lb_logits, hgrn_gnorm, w_o_hgrn, conv_w, w_o_conv, mla_q_norm, w_q_up, mla_kv_norm, w_kv_up, w_o_mla, w_out, norm2, w_up, ffn_conv_w, w_down, final_norm):
    depth = w_in.shape[0]
    bp, sp, _ = x_prompt.shape
    bs, ss, _ = x_sample.shape
    assert 1 + bs <= MOD_ROWS
    for rows in (IN_ROWS, MERGE_ROWS, FFN_ROWS):
        assert (bp * sp) % rows == 0 and (bs * ss) % rows == 0
        assert (rows % sp == 0 or sp % rows == 0) and ss % rows == 0

    cvec = _pad_axis(jnp.concatenate([c_ctx[None, :], c], axis=0), MOD_ROWS, 0)
    mod4 = _ada_call(cvec, w_ada, b_ada).reshape(depth, MOD_ROWS, 6, D_MODEL)
    lb_all = _lb_call(hgrn_lb_logits)
    tabs_ctx = _rope_tables(IN_ROWS, rotate=False)
    tabs_lat = _rope_tables(ss, rotate=True)
    fnorm = final_norm.reshape(1, D_MODEL)
    cache = (cache_mla, _pad_axis(cache_mla[..., C_KV_LORA:], LANES, 3))
    w_in_a, w_in_kv, w_in_g = _split_w_in(w_in)

    def vec(a):
        return a.reshape(depth, 1, -1)

    pw = {
        "norm1": vec(norm1), "w_in_a": w_in_a, "w_in_kv": w_in_kv, "w_in_g": w_in_g, "gnorm": vec(hgrn_gnorm),
        "w_o_hgrn": w_o_hgrn.astype(BF16), "conv_w": _pad_axis(conv_w, SUBLANES, 1),
        "w_o_conv": w_o_conv.astype(BF16), "q_norm": vec(mla_q_norm), "w_q_up": _pack_w_q_up(w_q_up),
        "kv_norm": vec(mla_kv_norm), "w_kv_up": _pack_w_kv_up(w_kv_up), "w_o_mla": w_o_mla.astype(BF16),
        "w_out": w_out.astype(BF16), "norm2": vec(norm2), "w_up": w_up.astype(BF16),
        "ffn_conv_w": _pad_axis(ffn_conv_w, SUBLANES, 1), "w_down": w_down.astype(BF16),
    }

    xp = x_prompt.reshape(bp * sp, D_MODEL)
    xs = x_sample.reshape(bs * ss, D_MODEL)
    stack_in_kernel = depth > 1 and IN_ROWS % sp == 0
    new_states, new_caches = [], []
    for l in range(depth):
        final = l == depth - 1
        prev = (tuple(new_states), tuple(new_caches)) if (final and stack_in_kernel) else ((), ())
        xp, st, kvc = _layer(xp, bp, sp, l, mod4, lambda r: 0, pw, lb_all, None, None, tabs_ctx,
                             True, final, fnorm, *prev)
        new_states.append(st)
        new_caches.append(kvc)
        xs, _, _ = _layer(xs, bs, ss, l, mod4, lambda r: 1 + r // ss, pw, lb_all, state_hgrn,
                          cache, tabs_lat, False, final, fnorm)
    y_prompt = xp.reshape(bp, sp, D_MODEL)
    y_sample = xs.reshape(bs, ss, D_MODEL)
    if stack_in_kernel:
        return (y_prompt, y_sample, new_states[-1], new_caches[-1])
    return (y_prompt, y_sample, jnp.stack(new_states, axis=1), jnp.stack(new_caches, axis=1))
```

```python
import functools
import math

import numpy as np
import jax
import jax.numpy as jnp
from jax import lax
from jax.experimental import pallas as pl
from jax.experimental.pallas import tpu as pltpu

F32 = jnp.float32
BF16 = jnp.bfloat16

D_MODEL = 1024
GRID_W = 64
EPS = 1e-6
A_HEADS = 4
A_DK = 128
A_DV = 128
A_W = A_HEADS * A_DK
LB_FLOOR = 1e-30
B_W = 512
C_HEADS = 8
C_NOPE = 64
C_ROPE = 32
C_V = 64
C_Q_LORA = 384
C_KV_LORA = 256
ROPE_THETA = 10000.0
D_FF = 2816
N_BRANCH = 3
CACHE_W = C_KV_LORA + C_ROPE
LOG2E = 1.4426950408889634

LANES = 128
SUBLANES = 8
HALO = 16
VMEM_LIMIT_BYTES = 56 * 1024 * 1024

IN_ROWS = 512
IN_LOOKAHEAD = 1
HGRN_CHUNK = 128
HGRN_CHUNK_GROUP = 2
HGRN_VMEM_BUDGET = 32 * 1024 * 1024
ATTN_Q_ROWS = 512
ATTN_SEQ_ROWS = 1024
MERGE_ROWS = 512
FFN_ROWS = 1024
FFN_COLS = 256
MOD_ROWS = 16
ADA_COLS = 1536

OFF_CQ = 8 * A_W
OFF_KV = OFF_CQ + C_Q_LORA
KV_EXT = 3 * LANES
OFF_GATE = OFF_KV + CACHE_W


def _cparams(sem):
    return pltpu.CompilerParams(dimension_semantics=sem, vmem_limit_bytes=VMEM_LIMIT_BYTES)


def _dot(a, b):
    return jnp.dot(a, b, preferred_element_type=F32)


def _dot_nt(a, b):
    return lax.dot_general(a, b, (((1,), (1,)), ((), ())), preferred_element_type=F32)


def _dot_tn(a, b):
    return lax.dot_general(a, b, (((0,), (0,)), ((), ())), preferred_element_type=F32)


def _rms(x, g):
    ms = jnp.mean(x * x, axis=-1, keepdims=True)
    return x * lax.rsqrt(ms + EPS) * g


def _sigmoid_pair(z):
    t = jnp.exp(-jnp.abs(z))
    r = 1.0 / (1.0 + t)
    big, small = r, t * r
    pos = z >= 0
    return jnp.where(pos, big, small), jnp.where(pos, small, big)


def _sigmoid(z):
    return 1.0 / (1.0 + jnp.exp(-z))


def _silu(z):
    return z * _sigmoid(z)


def _layer_spec(shape, layer, grid_rank, single=True):
    nd = len(shape)
    if grid_rank == 1:
        imap = lambda i: (layer,) + (0,) * nd
    else:
        imap = lambda i, j: (layer,) + (0,) * nd
    if single:
        return pl.BlockSpec((None,) + tuple(shape), imap, pipeline_mode=pl.Buffered(1))
    return pl.BlockSpec((None,) + tuple(shape), imap)


def _mod_spec(layer, mod_row, rows, grid_rank=1):
    if grid_rank == 1:
        return pl.BlockSpec((None, None, 6, D_MODEL), lambda i: (layer, mod_row(i * rows), 0, 0))
    return pl.BlockSpec((None, None, 6, D_MODEL), lambda i, j: (layer, mod_row(i * rows), 0, 0))


def _ada_kernel(c_ref, w_ref, b_ref, o_ref):
    s = _silu(c_ref[...])
    o_ref[...] = jnp.dot(s, w_ref[...], preferred_element_type=F32,
                         precision=lax.Precision.HIGHEST) + b_ref[...]


def _ada_call(cvec, w_ada, b_ada):
    depth = w_ada.shape[0]
    ncol = w_ada.shape[2] // ADA_COLS
    return pl.pallas_call(
        _ada_kernel,
        grid=(depth, ncol),
        in_specs=[
            pl.BlockSpec((MOD_ROWS, D_MODEL), lambda l, j: (0, 0)),
            pl.BlockSpec((None, D_MODEL, ADA_COLS), lambda l, j: (l, 0, j)),
            pl.BlockSpec((None, 1, ADA_COLS), lambda l, j: (l, 0, j)),
        ],
        out_specs=pl.BlockSpec((None, MOD_ROWS, ADA_COLS), lambda l, j: (l, 0, j)),
        out_shape=jax.ShapeDtypeStruct((depth, MOD_ROWS, w_ada.shape[2]), F32),
        compiler_params=_cparams(("arbitrary", "arbitrary")),
        name="ada_mod",
    )(cvec, w_ada, b_ada.reshape(depth, 1, -1))


def _lb_kernel(x_ref, o_ref):
    depth = x_ref.shape[0]
    xs = [x_ref[l] for l in range(depth)]
    m = xs[0]
    for l in range(1, depth):
        m = jnp.maximum(m, xs[l])
    es = [jnp.exp(x - m) for x in xs]
    tot = es[0]
    for l in range(1, depth):
        tot = tot + es[l]
    ps = [e / tot for e in es]
    run = jnp.zeros_like(ps[0])
    for l in range(depth):
        run = run + ps[l]
        o_ref[l] = run - ps[0]


def _lb_call(logits):
    return pl.pallas_call(
        _lb_kernel,
        out_shape=jax.ShapeDtypeStruct(logits.shape, F32),
        name="hgrn_lb",
    )(logits)


def _place_rope_key(z3, tabk):
    tmp = z3 * tabk
    rot = tmp + pltpu.roll(tmp, LANES - C_ROPE, axis=1)
    lane = lax.broadcasted_iota(jnp.int32, rot.shape, 1)
    rot = jnp.where(lane < C_ROPE, rot, 0.0)
    return pltpu.roll(rot, C_NOPE, axis=1) + pltpu.roll(rot, C_NOPE + C_ROPE, axis=1)


def _in_kernel(*refs, n_prev, seq):
    (x_ref, mod_ref, n1_ref, w_ref, wkvd_ref, wg_ref, lb_ref, qn_ref, wq_ref, kvn_ref, wkv_ref,
     tabq_ref, tabk_ref) = refs[:13]
    prev_cache_refs = refs[13:13 + n_prev]
    (q_ref, gf_ref, gb_ref, kf_ref, kb_ref, v_ref, ga_ref, bb_ref, u_ref,
     qfull_ref, kfull_ref, vv_ref, sg_ref) = refs[13 + n_prev:26 + n_prev]
    cache_ref = refs[26 + n_prev:]
    x = x_ref[...]
    shift1 = mod_ref[0:1, :]
    scale1 = mod_ref[1:2, :]
    xn = _rms(x, n1_ref[...]) * (1.0 + scale1) + shift1
    xb = xn.astype(BF16)

    def proj(off, width):
        return lambda: _dot(xb, w_ref[:, off:off + width])

    env = {}
    stages = []

    def ep_q(z):
        q_ref[...] = (_silu(z) * (A_DK ** -0.5)).astype(BF16)
    stages.append((proj(0, A_W), ep_q))

    def ep_forget(d, g_ref, k_ref):
        def ep(z):
            lb = lb_ref[d:d + 1, :]
            sp, sn = _sigmoid_pair(z)
            g_ref[...] = jnp.log(jnp.maximum(lb, LB_FLOOR) + (1.0 - lb) * sp) * LOG2E
            k_ref[...] = ((1.0 - lb) * sn - jnp.maximum(LB_FLOOR - lb, 0.0)).astype(BF16)
        return ep
    stages.append((proj(1 * A_W, A_W), ep_forget(0, gf_ref, kf_ref)))
    stages.append((proj(2 * A_W, A_W), ep_forget(1, gb_ref, kb_ref)))

    def ep_v(z):
        v_ref[...] = z.astype(BF16)
    stages.append((proj(3 * A_W, A_W), ep_v))

    def ep_ga(z):
        ga_ref[...] = _silu(z).astype(BF16)
    stages.append((proj(4 * A_W, A_W), ep_ga))

    def ep_bb(z):
        bb_ref[...] = z.astype(BF16)
    stages.append((proj(5 * A_W, B_W), ep_bb))

    def ep_cb(z):
        env["conv_c"] = z
    stages.append((proj(6 * A_W, B_W), ep_cb))

    def ep_hb(z):
        u_ref[...] = (env["conv_c"] * z).astype(BF16)
    stages.append((proj(7 * A_W, B_W), ep_hb))

    def ep_cq(z):
        env["cqn"] = _rms(z, qn_ref[...]).astype(BF16)
    stages.append((proj(OFF_CQ, C_Q_LORA), ep_cq))

    def ep_kv(z):
        ckvn = _rms(z[:, :C_KV_LORA], kvn_ref[...])
        z3 = z[:, C_KV_LORA:]
        if cache_ref and n_prev == 0:
            cache_ref[0][:, :C_KV_LORA] = ckvn
            cache_ref[0][:, C_KV_LORA:] = z3[:, :C_ROPE]
        elif cache_ref:
            for s in range(x_ref.shape[0] // seq):
                rows = slice(s * seq, (s + 1) * seq)
                for l, prev in enumerate(prev_cache_refs):
                    cache_ref[0][s, l] = prev[s]
                cache_ref[0][s, n_prev, :, :C_KV_LORA] = ckvn[rows, :]
                cache_ref[0][s, n_prev, :, C_KV_LORA:] = z3[rows, :C_ROPE]
        krr = _place_rope_key(z3, tabk_ref[...])
        env["krr"] = jnp.concatenate([krr, krr], axis=1)
        env["cb"] = ckvn.astype(BF16)
    stages.append((lambda: _dot(xb, wkvd_ref[...]), ep_kv))

    def gate_stage(j):
        def ep(z):
            sg_ref[:, j * A_W:(j + 1) * A_W] = _sigmoid(z).astype(BF16)
        return (lambda: _dot(xb, wg_ref[:, j * A_W:(j + 1) * A_W]), ep)

    def q_up_stage(h):
        def ep(z):
            tabq = jnp.concatenate([tabq_ref[...]] * 2, axis=1)
            qfull_ref[:, h * LANES:(h + 2) * LANES] = (z * tabq).astype(BF16)
        return (lambda: _dot(env["cqn"], wq_ref[:, h * LANES:(h + 2) * LANES]), ep)

    def k_up_stage(h):
        def ep(z):
            kfull_ref[:, h * LANES:(h + 2) * LANES] = (z + env["krr"]).astype(BF16)
        return (lambda: _dot(env["cb"], wkv_ref[:, h * LANES:(h + 2) * LANES]), ep)

    def ep_vv(z):
        vv_ref[...] = z.astype(BF16)
    v_up_stage = (lambda: _dot(env["cb"], wkv_ref[:, C_HEADS * LANES:]), ep_vv)

    stages = stages[8:] + stages[:8]
    stages += [gate_stage(0), q_up_stage(0), q_up_stage(2), gate_stage(1), q_up_stage(4), q_up_stage(6),
               gate_stage(2), k_up_stage(0), k_up_stage(2), gate_stage(3), k_up_stage(4), k_up_stage(6),
               gate_stage(4), v_up_stage, gate_stage(5)]

    pending = [stages[j][0]() for j in range(IN_LOOKAHEAD)]
    for j, (_, epilogue) in enumerate(stages):
        if j + IN_LOOKAHEAD < len(stages):
            pending.append(stages[j + IN_LOOKAHEAD][0]())
        epilogue(pending.pop(0))


def _in_call(x2d, layer, mod4, mod_row, pw, lb_all, tabq, tabk, want_cache, seq, prev_caches=()):
    n = x2d.shape[0]
    tm = IN_ROWS
    tab_blocks = tabq.shape[0] // tm
    n_prev = len(prev_caches) if want_cache else 0

    def row(width):
        return pl.BlockSpec((tm, width), lambda i: (i, 0))

    outs = [
        (A_W, BF16),
        (A_W, F32), (A_W, F32),
        (A_W, BF16), (A_W, BF16),
        (A_W, BF16),
        (A_W, BF16),
        (B_W, BF16),
        (B_W, BF16),
        (C_HEADS * LANES, BF16),
        (C_HEADS * LANES, BF16),
        (C_HEADS * C_V, BF16),
        (N_BRANCH * D_MODEL, BF16),
    ]
    out_specs = [row(w) for w, _ in outs]
    out_shape = [jax.ShapeDtypeStruct((n, w), dt) for w, dt in outs]
    prev_specs = []
    if want_cache and n_prev == 0:
        out_specs.append(row(CACHE_W))
        out_shape.append(jax.ShapeDtypeStruct((n, CACHE_W), F32))
    elif want_cache:
        per = tm // seq
        prev_specs = [pl.BlockSpec((per, seq, CACHE_W), lambda i: (i, 0, 0)) for _ in prev_caches]
        out_specs.append(pl.BlockSpec((per, n_prev + 1, seq, CACHE_W), lambda i: (i, 0, 0, 0)))
        out_shape.append(jax.ShapeDtypeStruct((n // seq, n_prev + 1, seq, CACHE_W), F32))
    return pl.pallas_call(
        functools.partial(_in_kernel, n_prev=n_prev, seq=seq),
        grid=(n // tm,),
        in_specs=[
            row(D_MODEL),
            _mod_spec(layer, mod_row, tm),
            _layer_spec((1, D_MODEL), layer, 1),
            _layer_spec((D_MODEL, OFF_KV), layer, 1),
            _layer_spec((D_MODEL, KV_EXT), layer, 1),
            _layer_spec((D_MODEL, N_BRANCH * D_MODEL), layer, 1),
            _layer_spec((2, A_W), layer, 1),
            _layer_spec((1, C_Q_LORA), layer, 1),
            _layer_spec((C_Q_LORA, C_HEADS * LANES), layer, 1),
            _layer_spec((1, C_KV_LORA), layer, 1),
            _layer_spec((C_KV_LORA, C_HEADS * LANES + C_HEADS * C_V), layer, 1),
            pl.BlockSpec((tm, LANES), lambda i: (i % tab_blocks, 0)),
            pl.BlockSpec((tm, LANES), lambda i: (i % tab_blocks, 0)),
        ] + prev_specs,
        out_specs=out_specs,
        out_shape=out_shape,
        compiler_params=_cparams(("arbitrary",)),
        name="in_proj",
    )(x2d, mod4, pw["norm1"], pw["w_in_a"], pw["w_in_kv"], pw["w_in_g"], lb_all, pw["q_norm"], pw["w_q_up"],
      pw["kv_norm"], pw["w_kv_up"], tabq, tabk, *(prev_caches if n_prev else ()))


def _cache_kernel(c_ref, kr_ref, wkv_ref, kfull_ref, vv_ref):
    cb = c_ref[:, :C_KV_LORA].astype(BF16)
    kr = kr_ref[...]
    krr = pltpu.roll(kr, C_NOPE, axis=1) + pltpu.roll(kr, C_NOPE + C_ROPE, axis=1)
    krr = jnp.concatenate([krr, krr], axis=1)
    for h in range(0, C_HEADS, 2):
        kz = _dot(cb, wkv_ref[:, h * LANES:(h + 2) * LANES])
        kfull_ref[:, h * LANES:(h + 2) * LANES] = (kz + krr).astype(BF16)
    vv_ref[...] = _dot(cb, wkv_ref[:, C_HEADS * LANES:]).astype(BF16)


def _cache_call(cache, cache_kr, layer, wkv):
    b, _, p, w = cache.shape
    return pl.pallas_call(
        _cache_kernel,
        grid=(b,),
        in_specs=[
            pl.BlockSpec((None, None, p, w), lambda i: (i, layer, 0, 0)),
            pl.BlockSpec((None, None, p, LANES), lambda i: (i, layer, 0, 0)),
            _layer_spec(wkv.shape[1:], layer, 1, single=False),
        ],
        out_specs=[
            pl.BlockSpec((None, p, C_HEADS * LANES), lambda i: (i, 0, 0)),
            pl.BlockSpec((None, p, C_HEADS * C_V), lambda i: (i, 0, 0)),
        ],
        out_shape=[
            jax.ShapeDtypeStruct((b, p, C_HEADS * LANES), BF16),
            jax.ShapeDtypeStruct((b, p, C_HEADS * C_V), BF16),
        ],
        compiler_params=_cparams(("arbitrary",)),
        name="cache_expand",
    )(cache, cache_kr, wkv)


def _level_matrix(c, rev):
    t = np.arange(c)[:, None]
    s = np.arange(c)[None, :]
    lev = np.floor(np.log2(np.maximum(np.bitwise_xor(t, s), 1))).astype(np.int32)
    lev = np.where(s == t, -1, lev)
    lev = np.where((s < t) if rev else (s > t), -2, lev)
    return lev.astype(np.int32)


def _prefix_matrix(c, rev):
    t = np.arange(c)[:, None]
    s = np.arange(c)[None, :]
    incl = ((s >= t) if rev else (s <= t)).astype(np.float32)
    return np.concatenate([incl, incl], axis=1)


def _level_exponent(b, g, h, rev):
    c, dk = b.shape
    ref = h if rev else h - 1
    if 2 * h > SUBLANES:
        parts = []
        for gs in range(0, c, 2 * h):
            beta = jnp.broadcast_to(b[gs + ref:gs + ref + 1, :], (h, dk))
            lo = b[gs:gs + h, :]
            hi = b[gs + h:gs + 2 * h, :]
            parts += [lo - beta, beta - hi] if rev else [beta - lo, hi - beta]
        return jnp.concatenate(parts, axis=0)
    row = lax.broadcasted_iota(jnp.int32, (SUBLANES, dk), 0)
    upper = (row % (2 * h)) >= h
    keep = jnp.logical_not(upper) if rev else upper
    if h == 1:
        parts = [jnp.where(keep, g[j:j + SUBLANES, :], 0.0) for j in range(0, c, SUBLANES)]
        return jnp.concatenate(parts, axis=0)
    sign = jnp.where(keep, 1.0, -1.0)
    parts = []
    for j in range(0, c, SUBLANES):
        bj = b[j:j + SUBLANES, :]
        beta = jnp.broadcast_to(bj[ref:ref + 1, :], (SUBLANES, dk))
        for gs in range(2 * h, SUBLANES, 2 * h):
            beta = jnp.where(row >= gs, jnp.broadcast_to(bj[gs + ref:gs + ref + 1, :], (SUBLANES, dk)), beta)
        parts.append((bj - beta) * sign)
    return jnp.concatenate(parts, axis=0)


def _level_halves(c, h):
    return [(slice(gs, gs + h), slice(gs + h, gs + 2 * h)) for gs in range(0, c, 2 * h)]


def _level_operands(q, k, qf, kf, x, h, rev):
    c = q.shape[0]
    if h < SUBLANES:
        return (qf * x).astype(BF16), (kf * x).astype(BF16)
    packed = h >= HALO
    qparts, kparts = [], []
    for lo, hi in _level_halves(c, h):
        qs, ks = (lo, hi) if rev else (hi, lo)
        q_sc, k_sc = qf[qs, :] * x[qs, :], kf[ks, :] * x[ks, :]
        if packed:
            q_sc, k_sc = q_sc.astype(BF16), k_sc.astype(BF16)
            q_raw, k_raw = q[ks, :], k[qs, :]
        else:
            q_raw, k_raw = qf[ks, :], kf[qs, :]
        qparts += [q_sc, q_raw] if rev else [q_raw, q_sc]
        kparts += [k_raw, k_sc] if rev else [k_sc, k_raw]
    qh, kh = jnp.concatenate(qparts, axis=0), jnp.concatenate(kparts, axis=0)
    return (qh, kh) if packed else (qh.astype(BF16), kh.astype(BF16))


def _level_select(a, p, lev, hl, rev):
    c = a.shape[0]
    h = 1 << hl
    if h < SUBLANES:
        return jnp.where(lev == hl, p, a)
    parts = []
    for lo, hi in _level_halves(c, h):
        qs, other = (lo, hi) if rev else (hi, lo)
        sel = jnp.where(lev[qs, :] == hl, p[qs, :], a[qs, :])
        parts += [sel, a[other, :]] if rev else [a[other, :], sel]
    return jnp.concatenate(parts, axis=0)


def _hgrn_intra(streams, lev, pmat):
    c = streams[0][0].shape[0]
    bs = []
    for q, k, g, v, rev in streams:
        g_hi = g.astype(BF16)
        g_lo = (g - g_hi.astype(F32)).astype(BF16)
        bs.append(_dot(pmat[rev], jnp.concatenate([g_hi, g_lo], axis=0)))
    qfs = [s[0].astype(F32) for s in streams]
    kfs = [s[1].astype(F32) for s in streams]
    a_s = [jnp.zeros((c, c), F32) for _ in streams]
    for hl in range(int(np.log2(c))):
        ps = []
        for (q, k, g, v, rev), b, qf, kf in zip(streams, bs, qfs, kfs):
            x = jnp.exp2(_level_exponent(b, g, 1 << hl, rev))
            ps.append(_dot_nt(*_level_operands(q, k, qf, kf, x, 1 << hl, rev)))
        a_s = [_level_select(a, p, lev[s[4]], hl, s[4]) for s, p, a in zip(streams, ps, a_s)]
    ps = [_dot_nt(s[0], s[1]) for s in streams]
    a_s = [jnp.where(lev[s[4]] == -1, p, a).astype(BF16) for s, p, a in zip(streams, ps, a_s)]
    return bs, a_s, qfs, kfs


def _hgrn_inter(streams, intra, states):
    c = streams[0][0].shape[0]
    bs, a_s, qfs, kfs = intra
    outs = []
    for (q, k, g, v, rev), b, a, qf, st in zip(streams, bs, a_s, qfs, states):
        outs.append(_dot(a, v) + _dot_nt((qf * jnp.exp2(b)).astype(BF16), st.astype(BF16)))
    new_states = []
    for (q, k, g, v, rev), b, kf, st in zip(streams, bs, kfs, states):
        tot = b[0:1, :] if rev else b[c - 1:c, :]
        ks = (kf * jnp.exp2(tot - b)).astype(BF16)
        new_states.append(st * jnp.exp2(tot) + _dot_tn(v, ks))
    return outs, new_states


def _hgrn_kernel(*refs, has_init, want_state, n_prev):
    refs = list(refs)
    (q_ref, gf_ref, gb_ref, kf_ref, kb_ref, v_ref, ga_ref, gn_ref,
     levf_ref, levb_ref, pmf_ref, pmb_ref) = refs[:12]
    pos = 12
    s0_ref = None
    if has_init:
        s0_ref = refs[pos]
        pos += 1
    prev_state_refs = refs[pos:pos + n_prev]
    pos += n_prev
    y_ref = refs[pos]
    pos += 1
    sfin_ref = None
    if want_state:
        sfin_ref = refs[pos]
        pos += 1
    o_scr, st_scr = refs[pos], refs[pos + 1]

    t = q_ref.shape[0]
    c = HGRN_CHUNK
    n = t // c
    hp = q_ref.shape[1] // A_DK
    for d in range(2):
        for h in range(hp):
            st_scr[d, h] = s0_ref[d, h].T if has_init else jnp.zeros((A_DV, A_DK), F32)
    lev = (levf_ref, levb_ref)
    pmat = (pmf_ref, pmb_ref)
    g_refs = (gf_ref, gb_ref)
    k_refs = (kf_ref, kb_ref)

    group = math.gcd(n, HGRN_CHUNK_GROUP)

    def body(i, carry):
        levs = (lev[0][...], lev[1][...])
        pmats = (pmat[0][...], pmat[1][...])
        steps = []
        for u in range(group):
            streams, rows = [], []
            for d in range(2):
                ci = i * group + u if d == 0 else n - 1 - (i * group + u)
                sl = pl.ds(pl.multiple_of(ci * c, c), c)
                rows.append(sl)
                for h in range(hp):
                    cs = slice(h * A_DK, (h + 1) * A_DK)
                    streams.append((q_ref[sl, cs], k_refs[d][sl, cs], g_refs[d][sl, cs], v_ref[sl, cs], d))
            steps.append((streams, rows))
        all_streams = [s for streams, _ in steps for s in streams]
        bs, a_s, qfs, kfs = _hgrn_intra(all_streams, levs, pmats)
        per = 2 * hp
        states = [st_scr[d, h] for d in range(2) for h in range(hp)]
        for u, (streams, rows) in enumerate(steps):
            part = slice(u * per, (u + 1) * per)
            outs, states = _hgrn_inter(streams, (bs[part], a_s[part], qfs[part], kfs[part]), states)
            for idx, o in enumerate(outs):
                d, h = divmod(idx, hp)
                o_scr[d, rows[d], h * A_DV:(h + 1) * A_DV] = o
        for idx, st_new in enumerate(states):
            d, h = divmod(idx, hp)
            st_scr[d, h] = st_new
        return carry

    lax.fori_loop(0, n // group, body, 0)
    for l, prev in enumerate(prev_state_refs):
        sfin_ref[l] = prev[...]
    for h in range(hp):
        cs = slice(h * A_DV, (h + 1) * A_DV)
        if want_state:
            for d in range(2):
                if n_prev:
                    sfin_ref[n_prev, d, h] = st_scr[d, h].T
                else:
                    sfin_ref[d, h] = st_scr[d, h].T
        o = o_scr[0, :, cs] + o_scr[1, :, cs]
        y_ref[:, cs] = (_rms(o, gn_ref[...]) * ga_ref[:, cs]).astype(BF16)


def _hgrn_heads_per_step(seq):
    per_head_row = 2 * (5 * 2 + 2 * 4) * A_DK + 2 * 2 * A_DV + 2 * 4 * A_DV
    hp = A_HEADS
    while hp > 1 and hp * seq * per_head_row > HGRN_VMEM_BUDGET:
        hp //= 2
    return hp


def _hgrn_call(q, gf, gb, kf, kb, v, ga, layer, gnorm, s_init, batch, seq, want_state, prev_states=()):
    c = HGRN_CHUNK
    n_prev = len(prev_states) if want_state else 0
    hp = _hgrn_heads_per_step(seq)
    consts =[jnp.asarray(_level_matrix(c, False)), jnp.asarray(_level_matrix(c, True)),
              jnp.asarray(_prefix_matrix(c, False), BF16), jnp.asarray(_prefix_matrix(c, True), BF16)]

    def r3(a):
        return a.reshape(batch, seq, A_W)

    col = pl.BlockSpec((None, seq, hp * A_DK), lambda b, h: (b, 0, h))
    in_specs = [col] * 7 + [_layer_spec((1, A_DV), layer, 2, single=False)]
    in_specs += [pl.BlockSpec(m.shape, lambda b, h: (0, 0)) for m in consts]
    args = [r3(q), r3(gf), r3(gb), r3(kf), r3(kb), r3(v), r3(ga), gnorm] + consts
    has_init = s_init is not None
    if has_init:
        in_specs.append(pl.BlockSpec((None, None, 2, hp, A_DK, A_DV), lambda b, h: (b, layer, 0, h, 0, 0)))
        args.append(s_init)
    st_spec = pl.BlockSpec((None, 2, hp, A_DK, A_DV), lambda b, h: (b, 0, h, 0, 0))
    for prev in (prev_states if n_prev else ()):
        in_specs.append(st_spec)
        args.append(prev)
    out_specs = [col]
    out_shape = [jax.ShapeDtypeStruct((batch, seq, A_W), BF16)]
    if want_state and n_prev:
        out_specs.append(pl.BlockSpec((None, n_prev + 1, 2, hp, A_DK, A_DV), lambda b, h: (b, 0, 0, h, 0, 0)))
        out_shape.append(jax.ShapeDtypeStruct((batch, n_prev + 1, 2, A_HEADS, A_DK, A_DV), F32))
    elif want_state:
        out_specs.append(st_spec)
        out_shape.append(jax.ShapeDtypeStruct((batch, 2, A_HEADS, A_DK, A_DV), F32))
    res = pl.pallas_call(
        functools.partial(_hgrn_kernel, has_init=has_init, want_state=want_state, n_prev=n_prev),
        grid=(batch, A_HEADS // hp),
        in_specs=in_specs,
        out_specs=out_specs,
        out_shape=out_shape,
        scratch_shapes=[pltpu.VMEM((2, seq, hp * A_DV), F32), pltpu.VMEM((2, hp, A_DV, A_DK), F32)],
        compiler_params=_cparams(("arbitrary", "arbitrary")),
        name="hgrn",
    )(*args)
    y = res[0].reshape(batch * seq, A_W)
    return y, (res[1] if want_state else None)


def _attn_kernel(*refs, has_cache):
    if has_cache:
        q_ref, k_ref, v_ref, kc_ref, vc_ref, y_ref = refs
    else:
        q_ref, k_ref, v_ref, y_ref = refs
    nseq = q_ref.shape[0]
    lane = lax.broadcasted_iota(jnp.int32, (q_ref.shape[1], LANES), 1)

    def scores(s, h):
        hs = slice(h * LANES, (h + 1) * LANES)
        qh = q_ref[s, :, hs]
        s1 = _dot_nt(qh, k_ref[s, :, hs])
        s2 = _dot_nt(qh, kc_ref[s, :, hs]) if has_cache else None
        return s1, s2

    def attend(s, h, s1, s2):
        vs = slice((h // 2) * LANES, (h // 2 + 1) * LANES)
        m = jnp.max(s1, axis=-1, keepdims=True)
        if has_cache:
            m = jnp.maximum(m, jnp.max(s2, axis=-1, keepdims=True))
        p1 = jnp.exp2(s1 - m)
        l = jnp.sum(p1, axis=-1, keepdims=True)
        o = _dot(p1.astype(BF16), v_ref[s, :, vs])
        if has_cache:
            p2 = jnp.exp2(s2 - m)
            l = l + jnp.sum(p2, axis=-1, keepdims=True)
            o = o + _dot(p2.astype(BF16), vc_ref[s, :, vs])
        return o / l

    items = [(s, h) for s in range(nseq) for h in range(C_HEADS)]
    pending = scores(*items[0])
    prev_out = None
    for idx, (s, h) in enumerate(items):
        nxt = scores(*items[idx + 1]) if idx + 1 < len(items) else None
        out = attend(s, h, *pending)
        pending = nxt
        if h % 2 == 1:
            vs = slice((h // 2) * LANES, (h // 2 + 1) * LANES)
            y_ref[s, :, vs] = jnp.where(lane < C_V, prev_out, out).astype(BF16)
        prev_out = out


def _attn_call(qfull, kfull, vv, kc, vc, batch, seq):
    tq = min(ATTN_Q_ROWS, seq)
    nb = math.gcd(batch, max(1, ATTN_SEQ_ROWS // seq)) if tq == seq else 1
    has_cache = kc is not None
    kw = C_HEADS * LANES
    vw = C_HEADS * C_V
    in_specs = [
        pl.BlockSpec((nb, tq, kw), lambda b, i: (b, i, 0)),
        pl.BlockSpec((nb, seq, kw), lambda b, i: (b, 0, 0)),
        pl.BlockSpec((nb, seq, vw), lambda b, i: (b, 0, 0)),
    ]
    args = [qfull.reshape(batch, seq, kw), kfull.reshape(batch, seq, kw), vv.reshape(batch, seq, vw)]
    if has_cache:
        p = kc.shape[1]
        in_specs += [
            pl.BlockSpec((nb, p, kw), lambda b, i: (b, 0, 0)),
            pl.BlockSpec((nb, p, vw), lambda b, i: (b, 0, 0)),
        ]
        args += [kc, vc]
    y = pl.pallas_call(
        functools.partial(_attn_kernel, has_cache=has_cache),
        grid=(batch // nb, seq // tq),
        in_specs=in_specs,
        out_specs=pl.BlockSpec((nb, tq, vw), lambda b, i: (b, i, 0)),
        out_shape=jax.ShapeDtypeStruct((batch, seq, vw), BF16),
        compiler_params=_cparams(("arbitrary", "arbitrary")),
        name="mla_attn",
    )(*args)
    return y.reshape(batch * seq, vw)


def _conv3_rows(u, prev_row, next_row, cw_ref, seq):
    r, w = u.shape
    sub = min(seq, r)
    whole = seq <= r
    i8 = lax.broadcasted_iota(jnp.int32, (SUBLANES, w), 0)
    w0, w1, w2 = cw_ref[0:1, :], cw_ref[1:2, :], cw_ref[2:3, :]
    outs = []
    for s in range(r // sub):
        blk = u[s * sub:(s + 1) * sub, :]
        p = 0.0 if whole else prev_row
        n = 0.0 if whole else next_row
        up = pltpu.roll(blk, 1, axis=0)
        dn = pltpu.roll(blk, sub - 1, axis=0)
        up = jnp.concatenate([jnp.where(i8 == 0, p, up[:SUBLANES, :]), up[SUBLANES:, :]], axis=0)
        dn = jnp.concatenate([dn[:sub - SUBLANES, :], jnp.where(i8 == SUBLANES - 1, n, dn[sub - SUBLANES:, :])],
                             axis=0)
        outs.append(up * w0 + blk * w1 + dn * w2)
    return outs[0] if len(outs) == 1 else jnp.concatenate(outs, axis=0)


def _seq_edges(i, rows, seq):
    first = i * rows
    return first % seq == 0, (first + rows) % seq == 0


def _merge_kernel(x_ref, ya_ref, yc_ref, bb_ref, u_ref, up_ref, un_ref, sg_ref, mod_ref,
                  wa_ref, wb_ref, wc_ref, wout_ref, cw_ref, n2_ref, x1_ref, xn2_ref, *, seq):
    tm = x_ref.shape[0]
    at_start, at_end = _seq_edges(pl.program_id(0), tm, seq)
    prev_row = jnp.where(at_start, 0.0, up_ref[HALO - 1:HALO, :].astype(F32))
    next_row = jnp.where(at_end, 0.0, un_ref[0:1, :].astype(F32))
    conv = _conv3_rows(u_ref[...].astype(F32), prev_row, next_row, cw_ref, seq)
    yb = (bb_ref[...] * conv).astype(BF16)
    half = tm // 2
    hs = []
    for s in range(2):
        rs = slice(s * half, (s + 1) * half)
        pa = _dot(ya_ref[rs, :], wa_ref[...])
        pc = _dot(yc_ref[rs, :], wc_ref[...])
        pb = _dot(yb[rs, :], wb_ref[...])
        hs.append((sg_ref[rs, 0:D_MODEL] * pa + sg_ref[rs, D_MODEL:2 * D_MODEL] * pb
                   + sg_ref[rs, 2 * D_MODEL:3 * D_MODEL] * pc).astype(BF16))
    gate1 = mod_ref[2:3, :]
    shift2 = mod_ref[3:4, :]
    scale2 = mod_ref[4:5, :]
    for s in range(2):
        rs = slice(s * half, (s + 1) * half)
        x1 = x_ref[rs, :] + gate1 * _dot(hs[s], wout_ref[...])
        x1_ref[rs, :] = x1
        xn2_ref[rs, :] = (_rms(x1, n2_ref[...]) * (1.0 + scale2) + shift2).astype(BF16)


def _merge_call(x2d, ya, yc, bb, u, sg, layer, mod4, mod_row, pw, seq):
    n = x2d.shape[0]
    tm = MERGE_ROWS
    nbh = n // HALO
    per = tm // HALO

    def row(width):
        return pl.BlockSpec((tm, width), lambda i: (i, 0))

    def wspec(name):
        return _layer_spec(pw[name].shape[1:], layer, 1)

    return pl.pallas_call(
        functools.partial(_merge_kernel, seq=seq),
        grid=(n // tm,),
        in_specs=[
            row(D_MODEL), row(A_W), row(C_HEADS * C_V), row(B_W), row(B_W),
            pl.BlockSpec((HALO, B_W), lambda i: (jnp.maximum(i * per - 1, 0), 0)),
            pl.BlockSpec((HALO, B_W), lambda i: (jnp.minimum((i + 1) * per, nbh - 1), 0)),
            row(N_BRANCH * D_MODEL),
            _mod_spec(layer, mod_row, tm),
            wspec("w_o_hgrn"), wspec("w_o_conv"), wspec("w_o_mla"), wspec("w_out"), wspec("conv_w"),
            wspec("norm2"),
        ],
        out_specs=[row(D_MODEL), row(D_MODEL)],
        out_shape=[jax.ShapeDtypeStruct((n, D_MODEL), F32), jax.ShapeDtypeStruct((n, D_MODEL), BF16)],
        compiler_params=_cparams(("arbitrary",)),
        name="merge",
    )(x2d, ya, yc, bb, u, u, u, sg, mod4, pw["w_o_hgrn"], pw["w_o_conv"], pw["w_o_mla"], pw["w_out"],
      pw["conv_w"], pw["norm2"])


def _ffn_kernel(xn_ref, xp_ref, xx_ref, x1_ref, mod_ref, wup_ref, cw_ref, wd_ref,
                fn_ref, out_ref, h_scr, lhs_scr, *, seq, final):
    r = xn_ref.shape[0]
    tn = FFN_COLS
    at_start, at_end = _seq_edges(pl.program_id(0), r, seq)
    need_halo = seq > r
    if need_halo:
        lhs_scr[0:HALO, :] = xp_ref[...]
        lhs_scr[HALO:HALO + r, :] = xn_ref[...]
        lhs_scr[HALO + r:, :] = xx_ref[...]
        xn = lhs_scr[...]
    else:
        xn = xn_ref[...]

    def up(off):
        return _dot(xn, wup_ref[:, off:off + tn])

    def conv_part(um, off):
        prev_row = next_row = None
        if need_halo:
            prev_row = jnp.where(at_start, 0.0, um[HALO - 1:HALO, :])
            next_row = jnp.where(at_end, 0.0, um[HALO + r:HALO + r + 1, :])
            um = um[HALO:HALO + r, :]
        return _conv3_rows(um, prev_row, next_row, cw_ref.at[:, off:off + tn], seq)

    offs = [o for j in range(D_FF // tn) for o in (j * tn, D_FF + j * tn)]
    pending = up(offs[0])
    a = None
    for idx, off in enumerate(offs):
        nxt = up(offs[idx + 1]) if idx + 1 < len(offs) else None
        conv = conv_part(pending, off)
        if off < D_FF:
            a = conv
        else:
            j = (off - D_FF) // tn
            h_scr[:, j * tn:(j + 1) * tn] = (_silu(a) * conv).astype(BF16)
        pending = nxt

    gate2 = mod_ref[5:6, :]
    half = r // 2
    for s in range(2):
        rs = slice(s * half, (s + 1) * half)
        y = x1_ref[rs, :] + gate2 * _dot(h_scr[rs, :], wd_ref[...])
        if final:
            y = _rms(y, fn_ref[...])
        out_ref[rs, :] = y


def _ffn_call(xn2, x1, layer, mod4, mod_row, pw, fnorm, seq, final):
    n = xn2.shape[0]
    r = FFN_ROWS
    nbh = n // HALO
    per = r // HALO
    return pl.pallas_call(
        functools.partial(_ffn_kernel, seq=seq, final=final),
        grid=(n // r,),
        in_specs=[
            pl.BlockSpec((r, D_MODEL), lambda i: (i, 0)),
            pl.BlockSpec((HALO, D_MODEL), lambda i: (jnp.maximum(i * per - 1, 0), 0)),
            pl.BlockSpec((HALO, D_MODEL), lambda i: (jnp.minimum((i + 1) * per, nbh - 1), 0)),
            pl.BlockSpec((r, D_MODEL), lambda i: (i, 0)),
            _mod_spec(layer, mod_row, r),
            _layer_spec(pw["w_up"].shape[1:], layer, 1),
            _layer_spec(pw["ffn_conv_w"].shape[1:], layer, 1),
            _layer_spec(pw["w_down"].shape[1:], layer, 1),
            pl.BlockSpec((1, D_MODEL), lambda i: (0, 0)),
        ],
        out_specs=pl.BlockSpec((r, D_MODEL), lambda i: (i, 0)),
        out_shape=jax.ShapeDtypeStruct((n, D_MODEL), F32),
        scratch_shapes=[pltpu.VMEM((r, D_FF), BF16),
                        pltpu.VMEM((r + 2 * HALO if seq > r else HALO, D_MODEL), BF16)],
        compiler_params=_cparams(("arbitrary",)),
        name="conv_mlp",
    )(xn2, xn2, xn2, x1, mod4, pw["w_up"], pw["ffn_conv_w"], pw["w_down"], fnorm)


def _rope_swap_perm():
    idx = np.arange(C_ROPE).reshape(2, 2, C_ROPE // 4)
    return idx[:, ::-1, :].reshape(-1)


def _split_w_in(w_in):
    depth = w_in.shape[0]
    swap = _rope_swap_perm()
    wb = lax.optimization_barrier(w_in.astype(BF16))
    kv = wb[:, :, OFF_KV:OFF_GATE]
    kr = kv[:, :, C_KV_LORA:]
    pad = jnp.zeros((depth, D_MODEL, KV_EXT - CACHE_W - C_ROPE), BF16)
    return wb, jnp.concatenate([kv, kr[:, :, swap], pad], axis=2), wb[:, :, OFF_GATE:]


def _pack_w_q_up(w):
    depth = w.shape[0]
    swap = _rope_swap_perm()
    w4 = w.astype(BF16).reshape(depth, C_Q_LORA, C_HEADS, C_NOPE + C_ROPE)
    rope = w4[..., C_NOPE:]
    return jnp.concatenate([w4, rope[..., swap]], axis=3).reshape(depth, C_Q_LORA, C_HEADS * LANES)


def _pack_w_kv_up(w):
    depth = w.shape[0]
    w4 = w.astype(BF16).reshape(depth, C_KV_LORA, C_HEADS, C_NOPE + C_V)
    kn = jnp.concatenate([w4[..., :C_NOPE], jnp.zeros((depth, C_KV_LORA, C_HEADS, LANES - C_NOPE), BF16)], axis=3)
    return jnp.concatenate(
        [kn.reshape(depth, C_KV_LORA, C_HEADS * LANES), w4[..., C_NOPE:].reshape(depth, C_KV_LORA, C_HEADS * C_V)],
        axis=2)


def _pad_axis(a, size, axis):
    shape = list(a.shape)
    shape[axis] = size - a.shape[axis]
    return jnp.concatenate([a, jnp.zeros(shape, a.dtype)], axis=axis)


def _rope_tables(seq, rotate):
    scale = (C_NOPE + C_ROPE) ** -0.5 * LOG2E
    n_freq = C_ROPE // 4
    if rotate:
        rows = seq // GRID_W
        r = jnp.repeat(jnp.arange(rows, dtype=F32), GRID_W)
        col = jnp.tile(jnp.arange(GRID_W, dtype=F32), rows)
        freq = ROPE_THETA ** (-jnp.arange(n_freq, dtype=F32) / n_freq)
        ang = jnp.stack([r[:, None] * freq, col[:, None] * freq], axis=1)
        cos, sin = jnp.cos(ang), jnp.sin(ang)
    else:
        cos = jnp.ones((seq, 2, n_freq), F32)
        sin = jnp.zeros((seq, 2, n_freq), F32)
    cos_full = jnp.stack([cos, cos], axis=2).reshape(seq, C_ROPE)
    sin_signed = jnp.stack([-sin, sin], axis=2).reshape(seq, C_ROPE)
    tabq = scale * jnp.concatenate([jnp.ones((seq, C_NOPE), F32), cos_full, sin_signed], axis=1)
    tabk = jnp.concatenate([cos_full, sin_signed, jnp.zeros((seq, LANES - 2 * C_ROPE), F32)], axis=1)
    return tabq, tabk


def _layer(x2d, batch, seq, layer, mod4, mod_row, pw, lb_all, s_init, cache, tabs, want_ctx_outputs,
           final, fnorm, prev_states=(), prev_caches=()):
    tabq, tabk = tabs
    res = _in_call(x2d, layer, mod4, mod_row, pw, lb_all, tabq, tabk, want_ctx_outputs, seq, prev_caches)
    q, gf, gb, kf, kb, v, ga, bb, u, qfull, kfull, vv, sg = res[:13]
    own_cache = None
    if want_ctx_outputs:
        own_cache = res[13] if prev_caches else res[13].reshape(batch, seq, CACHE_W)
    ya, states = _hgrn_call(q, gf, gb, kf, kb, v, ga, layer, pw["gnorm"], s_init, batch, seq, want_ctx_outputs,
                            prev_states)
    kc = vc = None
    if cache is not None:
        kc, vc = _cache_call(cache[0], cache[1], layer, pw["w_kv_up"])
    yc = _attn_call(qfull, kfull, vv, kc, vc, batch, seq)
    x1, xn2 = _merge_call(x2d, ya, yc, bb, u, sg, layer, mod4, mod_row, pw, seq)
    x2 = _ffn_call(xn2, x1, layer, mod4, mod_row, pw, fnorm, seq, final)
    return x2, states, own_cache


def kernel(x_prompt, x_sample, state_hgrn, cache_mla, c, c_ctx, w_ada, b_ada, norm1, w_in, hgrn_lb_logits, hgrn_gnorm, w_o_hgrn, conv_w, w_o_conv, mla_q_norm, w_q_up, mla_kv_norm, w_kv_up, w_o_mla, w_out, norm2, w_up, ffn_conv_w, w_down, final_norm):
    depth = w_in.shape[0]
    bp, sp, _ = x_prompt.shape
    bs, ss, _ = x_sample.shape
    assert 1 + bs <= MOD_ROWS
    for rows in (IN_ROWS, MERGE_ROWS, FFN_ROWS):
        assert (bp * sp) % rows == 0 and (bs * ss) % rows == 0
        assert (rows % sp == 0 or sp % rows == 0) and ss % rows == 0

    cvec = _pad_axis(jnp.concatenate([c_ctx[None, :], c], axis=0), MOD_ROWS, 0)
    mod4 = _ada_call(cvec, w_ada, b_ada).reshape(depth, MOD_ROWS, 6, D_MODEL)
    lb_all = _lb_call(hgrn_lb_logits)
    tabs_ctx = _rope_tables(IN_ROWS, rotate=False)
    tabs_lat = _rope_tables(ss, rotate=True)
    fnorm = final_norm.reshape(1, D_MODEL)
    cache = (cache_mla, _pad_axis(cache_mla[..., C_KV_LORA:], LANES, 3))
    w_in_a, w_in_kv, w_in_g = _split_w_in(w_in)

    def vec(a):
        return a.reshape(depth, 1, -1)

    pw = {
        "norm1": vec(norm1), "w_in_a": w_in_a, "w_in_kv": w_in_kv, "w_in_g": w_in_g, "gnorm": vec(hgrn_gnorm),
        "w_o_hgrn": w_o_hgrn.astype(BF16), "conv_w": _pad_axis(conv_w, SUBLANES, 1),
        "w_o_conv": w_o_conv.astype(BF16), "q_norm": vec(mla_q_norm), "w_q_up": _pack_w_q_up(w_q_up),
        "kv_norm": vec(mla_kv_norm), "w_kv_up": _pack_w_kv_up(w_kv_up), "w_o_mla": w_o_mla.astype(BF16),
        "w_out": w_out.astype(BF16), "norm2": vec(norm2), "w_up": w_up.astype(BF16),
        "ffn_conv_w": _pad_axis(ffn_conv_w, SUBLANES, 1), "w_down": w_down.astype(BF16),
    }

    xp = x_prompt.reshape(bp * sp, D_MODEL)
    xs = x_sample.reshape(bs * ss, D_MODEL)
    stack_in_kernel = depth > 1 and IN_ROWS % sp == 0
    new_states, new_caches = [], []
    for l in range(depth):
        final = l == depth - 1
        prev = (tuple(new_states), tuple(new_caches)) if (final and stack_in_kernel) else ((), ())
        xp, st, kvc = _layer(xp, bp, sp, l, mod4, lambda r: 0, pw, lb_all, None, None, tabs_ctx,
                             True, final, fnorm, *prev)
        new_states.append(st)
        new_caches.append(kvc)
        xs, _, _ = _layer(xs, bs, ss, l, mod4, lambda r: 1 + r // ss, pw, lb_all, state_hgrn,
                          cache, tabs_lat, False, final, fnorm)
    y_prompt = xp.reshape(bp, sp, D_MODEL)
    y_sample = xs.reshape(bs, ss, D_MODEL)
    if stack_in_kernel:
        return (y_prompt, y_sample, new_states[-1], new_caches[-1])
    return (y_prompt, y_sample, jnp.stack(new_states, axis=1), jnp.stack(new_caches, axis=1))
```
